```python
import math
import numpy as np
import jax, jax.numpy as jnp
from jax import lax

D_MODEL = 2048
BATCH = 4
SEQ = 2048
DEPTH = 4

N_MIXERS = 2
NSA_HEADS = 16
NSA_KV_GROUPS = 2
NSA_HEAD_DIM = D_MODEL // NSA_HEADS
NSA_Q_PER_GROUP = NSA_HEADS // NSA_KV_GROUPS
CMP_BLOCK = 32
CMP_STRIDE = 16
SLC_BLOCK = 64
N_SELECT = 8
WINDOW = 512
Q_BLOCK = 128
ROPE_THETA = 500000.0
ROPE_DIM = NSA_HEAD_DIM // 4
NSA_IN_DIM = NSA_HEADS * NSA_HEAD_DIM + 3 * 2 * NSA_KV_GROUPS * NSA_HEAD_DIM + 3 * NSA_HEADS
MLSTM_HEADS = 8
MLSTM_QK_DIM = D_MODEL // (2 * MLSTM_HEADS)
MLSTM_V_DIM = D_MODEL // MLSTM_HEADS
MLSTM_CHUNK = 64
GATE_SOFTCAP = 15.0
MLSTM_IN_DIM = 2 * MLSTM_HEADS * MLSTM_QK_DIM + 2 * MLSTM_HEADS * MLSTM_V_DIM + 2 * MLSTM_HEADS
D_FF = 5632
CONV_WIDTH = 3
NORM_EPS = 1e-6
NEG_INF = -1e30

kernel_name = "nsa_mlstm_interleaved_hybrid"


def rms_norm(x, g):
    xf = x.astype(jnp.float32)
    y = xf * lax.rsqrt(jnp.mean(xf * xf, axis=-1, keepdims=True) + NORM_EPS)
    return (y * g).astype(x.dtype)


def partial_rope(x, positions):
    half = ROPE_DIM // 2
    inv = jnp.power(ROPE_THETA, -jnp.arange(0, ROPE_DIM, 2, dtype=jnp.float32) / ROPE_DIM)
    ang = positions.astype(jnp.float32)[..., None] * inv
    cos, sin = jnp.cos(ang)[:, :, None, :], jnp.sin(ang)[:, :, None, :]
    xr = x[..., :ROPE_DIM].astype(jnp.float32)
    x1, x2 = xr[..., :half], xr[..., half:]
    rot = jnp.concatenate([x1 * cos - x2 * sin, x2 * cos + x1 * sin], axis=-1)
    return jnp.concatenate([rot.astype(x.dtype), x[..., ROPE_DIM:]], axis=-1)


def gather_blocks(blocks, idx):
    return jax.vmap(jax.vmap(lambda kb, ix: kb[ix]))(blocks, idx)


def nsa_mixer(xn, positions, w_in, gate_b, cmp_pe, cmp_w1, cmp_w2, w_out):
    B, S, _ = xn.shape
    H, G, R, hd = NSA_HEADS, NSA_KV_GROUPS, NSA_Q_PER_GROUP, NSA_HEAD_DIM
    scale = hd ** -0.5
    proj = xn @ w_in
    o1 = H * hd
    o2 = o1 + 2 * G * hd
    o3 = o2 + 2 * G * hd
    o4 = o3 + 2 * G * hd
    q = proj[..., :o1].reshape(B, S, H, hd)
    kv_c = proj[..., o1:o2].reshape(B, S, 2, G, hd)
    kv_s = proj[..., o2:o3].reshape(B, S, 2, G, hd)
    kv_w = proj[..., o3:o4].reshape(B, S, 2, G, hd)
    gates = jax.nn.sigmoid(proj[..., o4:].reshape(B, S, 3, H) + gate_b)
    t = np.arange(S)

    n_cmp = (S - CMP_BLOCK) // CMP_STRIDE + 1
    cmp_idx = np.arange(n_cmp)[:, None] * CMP_STRIDE + np.arange(CMP_BLOCK)[None, :]
    blocks = kv_c[:, cmp_idx] + cmp_pe.transpose(1, 0, 2)[:, :, None, :]
    hid = jax.nn.gelu(jnp.einsum('bjlcgd,cldh->bjcgh', blocks, cmp_w1))
    kv_cmp = jnp.einsum('bjcgh,che->bjcge', hid, cmp_w2)
    k_cmp, v_cmp = kv_cmp[:, :, 0], kv_cmp[:, :, 1]
    qg = q.reshape(B, S, G, R, hd)
    cmp_mask = cmp_idx[:, -1][None, :] <= t[:, None]
    s_c = jnp.einsum('bsgrd,bjgd->bgrsj', qg, k_cmp, preferred_element_type=jnp.float32) * scale
    p_c = jax.nn.softmax(jnp.where(cmp_mask, s_c, NEG_INF), axis=-1) * cmp_mask
    o_cmp = jnp.einsum('bgrsj,bjgd->bsgrd', p_c.astype(v_cmp.dtype), v_cmp).reshape(B, S, H, hd)

    n_slc = S // SLC_BLOCK
    jj = np.arange(n_cmp)[:, None]
    ss = np.arange(n_slc)[None, :]
    lo = np.maximum(jj * CMP_STRIDE, ss * SLC_BLOCK)
    hi = np.minimum(jj * CMP_STRIDE + CMP_BLOCK, ss * SLC_BLOCK + SLC_BLOCK)
    overlap = jnp.asarray(np.clip(hi - lo, 0, None).astype(np.float32) / CMP_BLOCK)
    imp = jnp.einsum('bgrsj,jn->bgsn', p_c, overlap)
    q_blk_id = t // SLC_BLOCK
    sid = np.arange(n_slc)
    forced = (sid[None, :] == 0) | (sid[None, :] == q_blk_id[:, None])
    valid = sid[None, :] <= q_blk_id[:, None]
    score = jnp.where(forced, jnp.inf, jnp.where(valid, imp, -jnp.inf))
    n_sel = min(N_SELECT, n_slc)
    _, sel_idx = lax.top_k(score, n_sel)

    q_r = partial_rope(q, positions).reshape(B, S, G, R, hd)
    k_s = partial_rope(kv_s[:, :, 0], positions).transpose(0, 2, 1, 3)
    v_s = kv_s[:, :, 1].transpose(0, 2, 1, 3)
    ks_blocks = k_s.reshape(B, G, n_slc, SLC_BLOCK, hd)
    vs_blocks = v_s.reshape(B, G, n_slc, SLC_BLOCK, hd)
    pad = ((0, 0), (0, 0), (WINDOW, 0), (0, 0))
    kw_pad = jnp.pad(partial_rope(kv_w[:, :, 0], positions).transpose(0, 2, 1, 3), pad)
    vw_pad = jnp.pad(kv_w[:, :, 1].transpose(0, 2, 1, 3), pad)

    nqb = S // Q_BLOCK
    q_blocks = q_r.reshape(B, nqb, Q_BLOCK, G, R, hd).transpose(1, 0, 3, 4, 2, 5)
    idx_blocks = sel_idx.reshape(B, G, nqb, Q_BLOCK, n_sel).transpose(2, 0, 1, 3, 4)

    def block_step(args):
        q_b, idx_b, qb = args
        tq = qb * Q_BLOCK + jnp.arange(Q_BLOCK)
        k_g = gather_blocks(ks_blocks, idx_b).reshape(B, G, Q_BLOCK, n_sel * SLC_BLOCK, hd)
        v_g = gather_blocks(vs_blocks, idx_b).reshape(B, G, Q_BLOCK, n_sel * SLC_BLOCK, hd)
        kpos = (idx_b[..., None] * SLC_BLOCK + jnp.arange(SLC_BLOCK)).reshape(B, G, Q_BLOCK, n_sel * SLC_BLOCK)
        m_s = (kpos <= tq[:, None])[:, :, None]
        s_s = jnp.einsum('bgrqd,bgqkd->bgrqk', q_b, k_g, preferred_element_type=jnp.float32) * scale
        p_s = jax.nn.softmax(jnp.where(m_s, s_s, NEG_INF), axis=-1)
        o_s = jnp.einsum('bgrqk,bgqkd->bgrqd', p_s.astype(v_g.dtype), v_g)
        kw = lax.dynamic_slice_in_dim(kw_pad, qb * Q_BLOCK, Q_BLOCK + WINDOW, axis=2)
        vw = lax.dynamic_slice_in_dim(vw_pad, qb * Q_BLOCK, Q_BLOCK + WINDOW, axis=2)
        wpos = qb * Q_BLOCK - WINDOW + jnp.arange(Q_BLOCK + WINDOW)
        dist = tq[:, None] - wpos[None, :]
        m_w = (wpos[None, :] >= 0) & (dist >= 0) & (dist < WINDOW)
        s_w = jnp.einsum('bgrqd,bgkd->bgrqk', q_b, kw, preferred_element_type=jnp.float32) * scale
        p_w = jax.nn.softmax(jnp.where(m_w, s_w, NEG_INF), axis=-1)
        o_w = jnp.einsum('bgrqk,bgkd->bgrqd', p_w.astype(vw.dtype), vw)
        return o_s, o_w

    o_sel, o_win = lax.map(block_step, (q_blocks, idx_blocks, jnp.arange(nqb)))
    o_sel = o_sel.transpose(1, 0, 4, 2, 3, 5).reshape(B, S, H, hd)
    o_win = o_win.transpose(1, 0, 4, 2, 3, 5).reshape(B, S, H, hd)
    o = (gates[:, :, 0, :, None] * o_cmp + gates[:, :, 1, :, None] * o_sel
         + gates[:, :, 2, :, None] * o_win)
    return o.reshape(B, S, H * hd) @ w_out


def mlstm_chunkwise(q, k, v, li, lf):
    B, H, S, dk = q.shape
    dv = v.shape[-1]
    L = MLSTM_CHUNK
    nc = S // L
    qc = q.reshape(B, H, nc, L, dk)
    kc = k.reshape(B, H, nc, L, dk)
    vc = v.reshape(B, H, nc, L, dv)
    lic = li.reshape(B, H, nc, L)
    F = jnp.cumsum(lf.reshape(B, H, nc, L), axis=-1)
    F_end = F[..., -1]
    a = F_end[..., None] - F + lic
    a_max = a.max(axis=-1)
    w = jnp.exp(a - a_max[..., None])
    dC = jnp.einsum('bhcl,bhclk,bhclv->bhckv', w, kc, vc)
    dn = jnp.einsum('bhcl,bhclk->bhck', w, kc)

    def step(carry, inp):
        C, n, m = carry
        f_end, am, dC_c, dn_c = inp
        m_new = jnp.maximum(f_end + m, am)
        decay = jnp.exp(f_end + m - m_new)
        inject = jnp.exp(am - m_new)
        C_new = decay[..., None, None] * C + inject[..., None, None] * dC_c
        n_new = decay[..., None] * n + inject[..., None] * dn_c
        return (C_new, n_new, m_new), (C, n, m)

    init = (jnp.zeros((B, H, dk, dv), jnp.float32), jnp.zeros((B, H, dk), jnp.float32),
            jnp.zeros((B, H), jnp.float32))
    xs = (F_end.transpose(2, 0, 1), a_max.transpose(2, 0, 1),
          dC.transpose(2, 0, 1, 3, 4), dn.transpose(2, 0, 1, 3))
    _, (C0, n0, m0) = lax.scan(step, init, xs)
    C0 = C0.transpose(1, 2, 0, 3, 4)
    n0 = n0.transpose(1, 2, 0, 3)
    m0 = m0.transpose(1, 2, 0)

    causal = np.tril(np.ones((L, L), dtype=bool))
    D = jnp.where(causal, F[..., :, None] - F[..., None, :] + lic[..., None, :], -jnp.inf)
    inter = F + m0[..., None]
    m_t = jnp.maximum(inter, D.max(axis=-1))
    Dw = jnp.exp(D - m_t[..., None])
    g_inter = jnp.exp(inter - m_t)
    Sqk = jnp.einsum('bhclk,bhcsk->bhcls', qc, kc) * Dw
    num = (jnp.einsum('bhcls,bhcsv->bhclv', Sqk, vc)
           + g_inter[..., None] * jnp.einsum('bhclk,bhckv->bhclv', qc, C0))
    den = Sqk.sum(axis=-1) + g_inter * jnp.einsum('bhclk,bhck->bhcl', qc, n0)
    h = num / jnp.maximum(jnp.abs(den), jnp.exp(-m_t))[..., None]
    return h.reshape(B, H, S, dv)


def mlstm_mixer(xn, w_in, gate_b, head_norm, w_out):
    B, S, _ = xn.shape
    H, dk, dv = MLSTM_HEADS, MLSTM_QK_DIM, MLSTM_V_DIM
    proj = xn @ w_in
    o1 = H * dk
    o2 = o1 + H * dk
    o3 = o2 + H * dv
    o4 = o3 + H * dv
    o5 = o4 + H
    to_heads = lambda z, d: z.reshape(B, S, H, d).transpose(0, 2, 1, 3).astype(jnp.float32)
    q = to_heads(proj[..., :o1], dk) * (dk ** -0.5)
    k = to_heads(proj[..., o1:o2], dk)
    v = to_heads(proj[..., o2:o3], dv)
    og = proj[..., o3:o4]
    i_pre = (proj[..., o4:o5] + gate_b[0]).astype(jnp.float32)
    f_pre = (proj[..., o5:] + gate_b[1]).astype(jnp.float32)
    li = (GATE_SOFTCAP * jnp.tanh(i_pre / GATE_SOFTCAP)).transpose(0, 2, 1)
    lf = jax.nn.log_sigmoid(GATE_SOFTCAP * jnp.tanh(f_pre / GATE_SOFTCAP)).transpose(0, 2, 1)
    h = mlstm_chunkwise(q, k, v, li, lf)
    h = h * lax.rsqrt(jnp.mean(h * h, axis=-1, keepdims=True) + NORM_EPS) * head_norm[:, None, :]
    h = h.transpose(0, 2, 1, 3).reshape(B, S, H * dv).astype(xn.dtype) * jax.nn.sigmoid(og)
    return h @ w_out


def conv_glu_ffn(xn, w_up, conv_w, conv_b, w_down):
    S = xn.shape[1]
    u = xn @ w_up
    up = jnp.pad(u, ((0, 0), (CONV_WIDTH - 1, 0), (0, 0)))
    c = conv_b + conv_w[0] * up[:, 0:S]
    for tap in range(1, CONV_WIDTH):
        c = c + conv_w[tap] * up[:, tap:tap + S]
    gate, val = c[..., :D_FF], c[..., D_FF:]
    return (jax.nn.silu(gate) * val) @ w_down


def setup_inputs(seed: int = 0) -> dict:
    key = jax.random.key(seed)
    ks = jax.random.split(key, 24)
    n_a = len(range(0, DEPTH, N_MIXERS))
    n_b = DEPTH - n_a
    nrm = lambda k, shape, s: jax.random.normal(k, shape, jnp.float32) * s
    H, G, hd = NSA_HEADS, NSA_KV_GROUPS, NSA_HEAD_DIM
    Hm = MLSTM_HEADS
    offset = jax.random.randint(ks[1], (BATCH,), 0, 1024, dtype=jnp.int32)
    positions = offset[:, None] + jnp.arange(SEQ, dtype=jnp.int32)[None, :]
    f_bias = jnp.linspace(3.0, 6.0, Hm, dtype=jnp.float32)[None, :] + nrm(ks[11], (n_b, Hm), 0.1)
    i_bias = nrm(ks[12], (n_b, Hm), 0.1)
    return {
        "x": nrm(ks[0], (BATCH, SEQ, D_MODEL), 1.0),
        "positions": positions,
        "nsa_w_in": nrm(ks[2], (n_a, D_MODEL, NSA_IN_DIM), D_MODEL ** -0.5),
        "nsa_gate_b": nrm(ks[3], (n_a, 3, H), 0.1),
        "nsa_cmp_pe": nrm(ks[4], (n_a, 2, CMP_BLOCK, hd), 0.5),
        "nsa_cmp_w1": nrm(ks[5], (n_a, 2, CMP_BLOCK, hd, hd), (CMP_BLOCK * hd) ** -0.5),
        "nsa_cmp_w2": nrm(ks[6], (n_a, 2, hd, hd), hd ** -0.5),
        "nsa_w_out": nrm(ks[7], (n_a, H * hd, D_MODEL), (H * hd) ** -0.5),
        "mlstm_w_in": nrm(ks[8], (n_b, D_MODEL, MLSTM_IN_DIM), D_MODEL ** -0.5),
        "mlstm_gate_b": jnp.stack([i_bias, f_bias], axis=1),
        "mlstm_head_norm": 1.0 + nrm(ks[9], (n_b, Hm, MLSTM_V_DIM), 0.02),
        "mlstm_w_out": nrm(ks[10], (n_b, Hm * MLSTM_V_DIM, D_MODEL), (Hm * MLSTM_V_DIM) ** -0.5),
        "norm_mix": 1.0 + nrm(ks[13], (DEPTH, D_MODEL), 0.02),
        "norm_ffn": 1.0 + nrm(ks[14], (DEPTH, D_MODEL), 0.02),
        "ffn_w_up": nrm(ks[15], (DEPTH, D_MODEL, 2 * D_FF), D_MODEL ** -0.5),
        "ffn_conv_w": nrm(ks[16], (DEPTH, CONV_WIDTH, 2 * D_FF), CONV_WIDTH ** -0.5),
        "ffn_conv_b": nrm(ks[17], (DEPTH, 2 * D_FF), 0.02),
        "ffn_w_down": nrm(ks[18], (DEPTH, D_FF, D_MODEL), D_FF ** -0.5),
        "norm_final": 1.0 + nrm(ks[19], (D_MODEL,), 0.02),
    }


def reference(x, positions, nsa_w_in, nsa_gate_b, nsa_cmp_pe, nsa_cmp_w1, nsa_cmp_w2, nsa_w_out,
              mlstm_w_in, mlstm_gate_b, mlstm_head_norm, mlstm_w_out,
              norm_mix, norm_ffn, ffn_w_up, ffn_conv_w, ffn_conv_b, ffn_w_down, norm_final):
    for i in range(DEPTH):
        xn = rms_norm(x, norm_mix[i])
        j = i // N_MIXERS
        if i % N_MIXERS == 0:
            y = nsa_mixer(xn, positions, nsa_w_in[j], nsa_gate_b[j], nsa_cmp_pe[j],
                          nsa_cmp_w1[j], nsa_cmp_w2[j], nsa_w_out[j])
        else:
            y = mlstm_mixer(xn, mlstm_w_in[j], mlstm_gate_b[j], mlstm_head_norm[j], mlstm_w_out[j])
        x = x + y
        x = x + conv_glu_ffn(rms_norm(x, norm_ffn[i]), ffn_w_up[i], ffn_conv_w[i],
                             ffn_conv_b[i], ffn_w_down[i])
    return rms_norm(x, norm_final)
```

```python
import functools

import numpy as np
import jax
import jax.numpy as jnp
from jax import lax
from jax.experimental import pallas as pl
from jax.experimental.pallas import tpu as pltpu

F32 = jnp.float32
BF16 = jnp.bfloat16

LANES = 128
NORM_EPS = 1e-6
NEG_INF = -1e30

NSA_HEADS = 16
NSA_KV_GROUPS = 2
NSA_HEAD_DIM = 128
NSA_Q_PER_GROUP = NSA_HEADS // NSA_KV_GROUPS
CMP_BLOCK = 32
CMP_STRIDE = 16
SLC_BLOCK = 64
N_SELECT = 8
WINDOW = 512
ROPE_THETA = 500000.0
ROPE_DIM = NSA_HEAD_DIM // 4
ROPE_HALF = ROPE_DIM // 2

MLSTM_HEADS = 8
MLSTM_QK_DIM = 128
MLSTM_V_DIM = 256
MLSTM_CHUNK = 64
GATE_SOFTCAP = 15.0

CONV_WIDTH = 3
CONV_HALO = 8

MIB = 1024 * 1024


def _params(semantics, vmem_mib):
    return pltpu.CompilerParams(dimension_semantics=semantics, vmem_limit_bytes=vmem_mib * MIB)


def _dot(a, b):
    return jnp.dot(a, b, preferred_element_type=F32)


def _dot_nt(a, b):
    return lax.dot_general(a, b, (((1,), (1,)), ((), ())), preferred_element_type=F32)


def _dot_tn(a, b):
    return lax.dot_general(a, b, (((0,), (0,)), ((), ())), preferred_element_type=F32)


def _split3(x):
    hi = x.astype(BF16)
    r1 = x - hi.astype(F32)
    mid = r1.astype(BF16)
    lo = (r1 - mid.astype(F32)).astype(BF16)
    return hi, mid, lo


def _rms_rows(x, g):
    ms = jnp.mean(x * x, axis=-1, keepdims=True)
    return x * lax.rsqrt(ms + NORM_EPS) * g


def _lane_col(x, idx):
    lane = lax.broadcasted_iota(jnp.int32, x.shape, 1)
    return jnp.sum(jnp.where(lane == idx, x, 0.0), axis=-1, keepdims=True)


def _norm_matmul_kernel(x_ref, g_ref, w_ref, o_ref, xn_ref):
    @pl.when(pl.program_id(1) == 0)
    def _():
        xn_ref[...] = _rms_rows(x_ref[...], g_ref[...]).astype(BF16)

    o_ref[...] = _dot(xn_ref[...], w_ref[...])


def _norm_matmul(x2d, g, w_bf, tm=512, tn=512):
    t, d = x2d.shape
    n = w_bf.shape[1]
    return pl.pallas_call(
        _norm_matmul_kernel,
        grid=(t // tm, n // tn),
        in_specs=[pl.BlockSpec((tm, d), lambda i, j: (i, 0)),
                  pl.BlockSpec((1, d), lambda i, j: (0, 0)),
                  pl.BlockSpec((d, tn), lambda i, j: (0, j))],
        out_specs=pl.BlockSpec((tm, tn), lambda i, j: (i, j)),
        out_shape=jax.ShapeDtypeStruct((t, n), F32),
        scratch_shapes=[pltpu.VMEM((tm, d), BF16)],
        compiler_params=_params(("parallel", "arbitrary"), 40),
        name="norm_matmul",
    )(x2d, g.reshape(1, d), w_bf)


def _out_proj_kernel(*refs, n_in):
    a_refs = refs[:n_in]
    w_ref, r_ref, o_ref, a_bf = refs[n_in:]

    @pl.when(pl.program_id(1) == 0)
    def _():
        a = a_refs[0][...]
        for ar in a_refs[1:]:
            a = a + ar[...]
        a_bf[...] = a.astype(BF16)

    o_ref[...] = r_ref[...] + _dot(a_bf[...], w_ref[...])


def _out_proj(a_list, w_bf, resid, tm=512, tn=512):
    t, k = a_list[0].shape
    n = w_bf.shape[1]
    n_in = len(a_list)
    return pl.pallas_call(
        functools.partial(_out_proj_kernel, n_in=n_in),
        grid=(t // tm, n // tn),
        in_specs=[pl.BlockSpec((tm, k), lambda i, j: (i, 0))] * n_in
        + [pl.BlockSpec((k, tn), lambda i, j: (0, j)),
           pl.BlockSpec((tm, tn), lambda i, j: (i, j))],
        out_specs=pl.BlockSpec((tm, tn), lambda i, j: (i, j)),
        out_shape=jax.ShapeDtypeStruct((t, n), F32),
        scratch_shapes=[pltpu.VMEM((tm, k), BF16)],
        compiler_params=_params(("parallel", "arbitrary"), 48),
        name="out_proj",
    )(*a_list, w_bf, resid)


def _ffn_kernel(x_ref, g_ref, wg_ref, wv_ref, cwg_ref, cwv_ref, cbg_ref, cbv_ref, wd_ref, o_ref,
                xn_ref, acc_ref, ug_ref, uv_ref, carry_g, carry_v, *, tm, tiles_per_seq):
    i = pl.program_id(0)
    f = pl.program_id(1)

    @pl.when(f == 0)
    def _():
        xn_ref[...] = _rms_rows(x_ref[...], g_ref[...]).astype(BF16)
        acc_ref[...] = jnp.zeros_like(acc_ref)

    xn = xn_ref[...]
    seq_start = (i % tiles_per_seq) == 0

    def conv_branch(w_ref, cw_ref, cb_ref, u_ref, carry):
        u = _dot(xn, w_ref[...])
        u_ref[pl.ds(CONV_HALO, tm), :] = u

        @pl.when(seq_start)
        def _():
            u_ref[pl.ds(0, CONV_HALO), :] = jnp.zeros((CONV_HALO, u.shape[1]), F32)

        @pl.when(jnp.logical_not(seq_start))
        def _():
            u_ref[pl.ds(0, CONV_HALO), :] = carry[f]

        carry[f] = u[tm - CONV_HALO:, :]
        cw = cw_ref[...]
        c = cb_ref[...] + cw[0:1, :] * u_ref[pl.ds(CONV_HALO - 2, tm), :]
        c = c + cw[1:2, :] * u_ref[pl.ds(CONV_HALO - 1, tm), :]
        return c + cw[2:3, :] * u

    gate = conv_branch(wg_ref, cwg_ref, cbg_ref, ug_ref, carry_g)
    val = conv_branch(wv_ref, cwv_ref, cbv_ref, uv_ref, carry_v)
    h = (jax.nn.silu(gate) * val).astype(BF16)
    acc_ref[...] += _dot(h, wd_ref[...])

    @pl.when(f == pl.num_programs(1) - 1)
    def _():
        o_ref[...] = x_ref[...] + acc_ref[...]


def _ffn(x2d, g, w_up_bf, conv_w, conv_b, w_down_bf, seq, tm=512, tf=512):
    t, d = x2d.shape
    d_ff = w_down_bf.shape[0]
    nf = d_ff // tf
    kern = functools.partial(_ffn_kernel, tm=tm, tiles_per_seq=seq // tm)
    return pl.pallas_call(
        kern,
        grid=(t // tm, nf),
        in_specs=[pl.BlockSpec((tm, d), lambda i, f: (i, 0)),
                  pl.BlockSpec((1, d), lambda i, f: (0, 0)),
                  pl.BlockSpec((d, tf), lambda i, f: (0, f)),
                  pl.BlockSpec((d, tf), lambda i, f: (0, f + nf)),
                  pl.BlockSpec((CONV_WIDTH, tf), lambda i, f: (0, f)),
                  pl.BlockSpec((CONV_WIDTH, tf), lambda i, f: (0, f + nf)),
                  pl.BlockSpec((1, tf), lambda i, f: (0, f)),
                  pl.BlockSpec((1, tf), lambda i, f: (0, f + nf)),
                  pl.BlockSpec((tf, d), lambda i, f: (f, 0))],
        out_specs=pl.BlockSpec((tm, d), lambda i, f: (i, 0)),
        out_shape=jax.ShapeDtypeStruct((t, d), F32),
        scratch_shapes=[pltpu.VMEM((tm, d), BF16),
                        pltpu.VMEM((tm, d), F32),
                        pltpu.VMEM((tm + CONV_HALO, tf), F32),
                        pltpu.VMEM((tm + CONV_HALO, tf), F32),
                        pltpu.VMEM((nf, CONV_HALO, tf), F32),
                        pltpu.VMEM((nf, CONV_HALO, tf), F32)],
        compiler_params=_params(("arbitrary", "arbitrary"), 56),
        name="conv_glu_ffn",
    )(x2d, g.reshape(1, d), w_up_bf, w_up_bf, conv_w, conv_w,
      conv_b.reshape(1, -1), conv_b.reshape(1, -1), w_down_bf)


def _rmsnorm_kernel(x_ref, g_ref, o_ref):
    o_ref[...] = _rms_rows(x_ref[...], g_ref[...])


def _rmsnorm(x2d, g, tm=512):
    t, d = x2d.shape
    return pl.pallas_call(
        _rmsnorm_kernel,
        grid=(t // tm,),
        in_specs=[pl.BlockSpec((tm, d), lambda i: (i, 0)), pl.BlockSpec((1, d), lambda i: (0, 0))],
        out_specs=pl.BlockSpec((tm, d), lambda i: (i, 0)),
        out_shape=jax.ShapeDtypeStruct((t, d), F32),
        compiler_params=_params(("parallel",), 32),
        name="final_rmsnorm",
    )(x2d, g.reshape(1, d))


def _rope_table_kernel(pos_ref, inv_ref, c_ref, sa_ref, sb_ref):
    ang = pos_ref[0].astype(F32) * inv_ref[...]
    lane = lax.broadcasted_iota(jnp.int32, ang.shape, 1)
    cos = jnp.cos(ang)
    sin = jnp.sin(ang)
    c_ref[0] = jnp.where(lane < ROPE_DIM, cos, 1.0)
    sa_ref[0] = jnp.where((lane >= ROPE_HALF) & (lane < ROPE_DIM), sin, 0.0)
    sb_ref[0] = jnp.where(lane < ROPE_HALF, -sin, 0.0)


def _rope_tables(positions):
    b, s = positions.shape
    inv = jnp.power(ROPE_THETA, -jnp.arange(0, ROPE_DIM, 2, dtype=F32) / ROPE_DIM)
    inv_lane = jnp.concatenate([inv, inv, jnp.zeros((LANES - ROPE_DIM,), F32)]).reshape(1, LANES)
    spec = pl.BlockSpec((1, s, LANES), lambda i: (i, 0, 0))
    shape = jax.ShapeDtypeStruct((b, s, LANES), F32)
    return pl.pallas_call(
        _rope_table_kernel,
        grid=(b,),
        in_specs=[pl.BlockSpec((1, s, 1), lambda i: (i, 0, 0)),
                  pl.BlockSpec((1, LANES), lambda i: (0, 0))],
        out_specs=[spec, spec, spec],
        out_shape=[shape, shape, shape],
        compiler_params=_params(("parallel",), 32),
        name="rope_tables",
    )(positions.reshape(b, s, 1), inv_lane)


def _rope(x, c, sa, sb):
    return x * c + pltpu.roll(x, ROPE_HALF, 1) * sa + pltpu.roll(x, LANES - ROPE_HALF, 1) * sb


def _cmp_mlp_kernel(x_ref, pe_ref, w1_ref, w2_ref, o_ref):
    n = o_ref.shape[-2]
    acc_lo = jnp.zeros((n, NSA_HEAD_DIM), F32)
    acc_hi = jnp.zeros((n, NSA_HEAD_DIM), F32)
    for l in range(CMP_STRIDE):
        xs = x_ref[0, pl.ds(l, n, stride=CMP_STRIDE), :]
        acc_lo += _dot((xs + pe_ref[0, l:l + 1, :]).astype(BF16), w1_ref[0, l])
        acc_hi += _dot((xs + pe_ref[0, CMP_STRIDE + l:CMP_STRIDE + l + 1, :]).astype(BF16),
                       w1_ref[0, CMP_STRIDE + l])
    hid = jax.nn.gelu(acc_lo + pltpu.roll(acc_hi, n - 1, 0))
    o_ref[0, 0, 0] = _dot(hid.astype(BF16), w2_ref[0])


def _cmp_mlp(proj, pe, w1_bf, w2_bf, col_block0):
    b, s, _ = proj.shape
    g, hd = NSA_KV_GROUPS, NSA_HEAD_DIM
    n = s // CMP_STRIDE
    return pl.pallas_call(
        _cmp_mlp_kernel,
        grid=(b, 2, g),
        in_specs=[pl.BlockSpec((1, s, hd), lambda bi, c, gi: (bi, 0, col_block0 + c * g + gi)),
                  pl.BlockSpec((1, CMP_BLOCK, hd), lambda bi, c, gi: (c, 0, 0)),
                  pl.BlockSpec((1, CMP_BLOCK, hd, hd), lambda bi, c, gi: (c, 0, 0, 0)),
                  pl.BlockSpec((1, hd, hd), lambda bi, c, gi: (c, 0, 0))],
        out_specs=pl.BlockSpec((1, 1, 1, n, hd), lambda bi, c, gi: (bi, c, gi, 0, 0)),
        out_shape=jax.ShapeDtypeStruct((b, 2, g, n, hd), F32),
        compiler_params=_params(("parallel", "parallel", "parallel"), 32),
        name="nsa_cmp_mlp",
    )(proj, pe, w1_bf, w2_bf)


def _cmp_attn_kernel(q_ref, kc_ref, vc_ref, gl_ref, gb_ref, ov_ref, o_ref, sel_ref, *, ts, n_slc):
    g = pl.program_id(1)
    si = pl.program_id(2)
    hd = NSA_HEAD_DIM
    scale = hd ** -0.5
    lane = lax.broadcasted_iota(jnp.int32, (ts, LANES), 1)
    t = si * ts + lax.broadcasted_iota(jnp.int32, (ts, LANES), 0)
    cmask = (lane * CMP_STRIDE + (CMP_BLOCK - 1)) <= t
    cmask_f = cmask.astype(F32)
    kc = kc_ref[0, 0, 0].astype(BF16)
    vc = vc_ref[0, 0, 0].astype(BF16)
    gates = jax.nn.sigmoid(gl_ref[0] + gb_ref[...])
    psum = jnp.zeros((ts, LANES), F32)
    for r in range(NSA_Q_PER_GROUP):
        q = q_ref[0, :, r * hd:(r + 1) * hd].astype(BF16)
        s = jnp.where(cmask, _dot_nt(q, kc) * scale, NEG_INF)
        e = jnp.exp(s - jnp.max(s, axis=-1, keepdims=True))
        p = e / jnp.sum(e, axis=-1, keepdims=True) * cmask_f
        psum = psum + p
        o = _dot(p.astype(BF16), vc)
        o_ref[0, :, r * hd:(r + 1) * hd] = o * _lane_col(gates, g * NSA_Q_PER_GROUP + r)

    ov = ov_ref[...]
    hi, mid, lo = _split3(psum)
    imp = _dot(hi, ov) + _dot(mid, ov) + _dot(lo, ov)
    q_blk = t // SLC_BLOCK
    forced = (lane == 0) | (lane == q_blk)
    score = jnp.where(forced, jnp.inf, jnp.where(lane <= q_blk, imp, -jnp.inf))
    live = lane < n_slc
    lane_f = lane.astype(F32)
    taken = jnp.zeros((ts, LANES), jnp.int32)
    for _ in range(min(N_SELECT, n_slc)):
        cand_ok = live & (taken == 0)
        best = jnp.max(jnp.where(cand_ok, score, -jnp.inf), axis=-1, keepdims=True)
        first = jnp.min(jnp.where(cand_ok & (score == best), lane_f, float(LANES)), axis=-1, keepdims=True)
        taken = jnp.where(lane_f == first, 1, taken)
    sel_ref[0, 0] = taken.astype(F32)


def _cmp_attn(proj, kv_cmp, gate_b_lane, overlap_bf, gate_col_block, ts=256):
    b, s, _ = proj.shape
    g, hd, r = NSA_KV_GROUPS, NSA_HEAD_DIM, NSA_Q_PER_GROUP
    n = kv_cmp.shape[-2]
    kern = functools.partial(_cmp_attn_kernel, ts=ts, n_slc=s // SLC_BLOCK)
    return pl.pallas_call(
        kern,
        grid=(b, g, s // ts),
        in_specs=[pl.BlockSpec((1, ts, r * hd), lambda bi, gi, si: (bi, si, gi)),
                  pl.BlockSpec((1, 1, 1, n, hd), lambda bi, gi, si: (bi, 0, gi, 0, 0)),
                  pl.BlockSpec((1, 1, 1, n, hd), lambda bi, gi, si: (bi, 1, gi, 0, 0)),
                  pl.BlockSpec((1, ts, LANES), lambda bi, gi, si: (bi, si, gate_col_block)),
                  pl.BlockSpec((1, LANES), lambda bi, gi, si: (0, 0)),
                  pl.BlockSpec((n, LANES), lambda bi, gi, si: (0, 0))],
        out_specs=[pl.BlockSpec((1, ts, r * hd), lambda bi, gi, si: (bi, si, gi)),
                   pl.BlockSpec((1, 1, ts, LANES), lambda bi, gi, si: (bi, gi, si, 0))],
        out_shape=[jax.ShapeDtypeStruct((b, s, g * r * hd), F32),
                   jax.ShapeDtypeStruct((b, g, s, LANES), F32)],
        compiler_params=_params(("parallel", "parallel", "parallel"), 32),
        name="nsa_cmp_attn_topk",
    )(proj, kv_cmp, kv_cmp, proj, gate_b_lane, overlap_bf)


def _flash_heads(q_ref, o_ref, krot, vbf, bias, gates, gate_lane0, ropes, lo, hi, bias_off, tk):
    hd = NSA_HEAD_DIM
    scale = hd ** -0.5
    tq = q_ref.shape[1]
    for r in range(NSA_Q_PER_GROUP):
        q = _rope(q_ref[0, :, r * hd:(r + 1) * hd], *ropes).astype(BF16)

        def body(c, carry):
            m, l, acc = carry
            k0 = pl.multiple_of(c * tk, tk)
            s = _dot_nt(q, krot[pl.ds(k0, tk), :]) * scale + bias[c - bias_off]
            m_new = jnp.maximum(m, jnp.max(s, axis=-1, keepdims=True))
            alpha = jnp.exp(m - m_new)
            p = jnp.exp(s - m_new)
            l = alpha * l + jnp.sum(p, axis=-1, keepdims=True)
            acc = alpha * acc + _dot(p.astype(BF16), vbf[pl.ds(k0, tk), :])
            return m_new, l, acc

        init = (jnp.full((tq, 1), -jnp.inf, F32), jnp.zeros((tq, 1), F32), jnp.zeros((tq, hd), F32))
        _, l, acc = lax.fori_loop(lo, hi, body, init)
        o_ref[0, :, r * hd:(r + 1) * hd] = acc / l * _lane_col(gates, gate_lane0 + r)


def _load_kv(qi, k_ref, v_ref, c_ref, sa_ref, sb_ref, krot, vbf):
    @pl.when(qi == 0)
    def _():
        krot[...] = _rope(k_ref[0], c_ref[0], sa_ref[0], sb_ref[0]).astype(BF16)
        vbf[...] = v_ref[0].astype(BF16)


def _sel_attn_kernel(q_ref, k_ref, v_ref, sm_ref, ex_ref, gl_ref, gb_ref, c_ref, sa_ref, sb_ref, o_ref,
                     krot, vbf, bias, *, tq, tk):
    g = pl.program_id(1)
    qi = pl.program_id(2)
    _load_kv(qi, k_ref, v_ref, c_ref, sa_ref, sb_ref, krot, vbf)
    q0 = pl.multiple_of(qi * tq, tq)
    t = q0 + lax.broadcasted_iota(jnp.int32, (tq, tk), 0)
    sm = sm_ref[0, 0].astype(BF16)
    for c in range(bias.shape[0]):
        member = _dot(sm, ex_ref[:, c * tk:(c + 1) * tk])
        kpos = c * tk + lax.broadcasted_iota(jnp.int32, (tq, tk), 1)
        bias[c] = jnp.where((member > 0.5) & (kpos <= t), 0.0, NEG_INF)
    ropes = (c_ref[0, pl.ds(q0, tq), :], sa_ref[0, pl.ds(q0, tq), :], sb_ref[0, pl.ds(q0, tq), :])
    gates = jax.nn.sigmoid(gl_ref[0] + gb_ref[...])
    n_chunks = (q0 + tq + tk - 1) // tk
    _flash_heads(q_ref, o_ref, krot, vbf, bias, gates, NSA_HEADS + g * NSA_Q_PER_GROUP, ropes,
                 0, n_chunks, 0, tk)


def _win_attn_kernel(q_ref, k_ref, v_ref, gl_ref, gb_ref, c_ref, sa_ref, sb_ref, o_ref,
                     krot, vbf, bias, *, tq):
    g = pl.program_id(1)
    qi = pl.program_id(2)
    _load_kv(qi, k_ref, v_ref, c_ref, sa_ref, sb_ref, krot, vbf)
    q0 = pl.multiple_of(qi * tq, tq)
    back = WINDOW // tq
    row = lax.broadcasted_iota(jnp.int32, (tq, tq), 0)
    col = lax.broadcasted_iota(jnp.int32, (tq, tq), 1)
    for w in range(back + 1):
        dist = (back - w) * tq + row - col
        bias[w] = jnp.where((dist >= 0) & (dist < WINDOW), 0.0, NEG_INF)
    ropes = (c_ref[0, pl.ds(q0, tq), :], sa_ref[0, pl.ds(q0, tq), :], sb_ref[0, pl.ds(q0, tq), :])
    gates = jax.nn.sigmoid(gl_ref[0] + gb_ref[...])
    _flash_heads(q_ref, o_ref, krot, vbf, bias, gates, 2 * NSA_HEADS + g * NSA_Q_PER_GROUP, ropes,
                 jnp.maximum(qi - back, 0), qi + 1, qi - back, tq)


def _kv_attn_specs(s, tq, k_col_block0, gate_col_block):
    g, hd, r = NSA_KV_GROUPS, NSA_HEAD_DIM, NSA_Q_PER_GROUP
    q_spec = pl.BlockSpec((1, tq, r * hd), lambda bi, gi, qi: (bi, qi, gi))
    k_spec = pl.BlockSpec((1, s, hd), lambda bi, gi, qi: (bi, 0, k_col_block0 + gi))
    v_spec = pl.BlockSpec((1, s, hd), lambda bi, gi, qi: (bi, 0, k_col_block0 + g + gi))
    gl_spec = pl.BlockSpec((1, tq, LANES), lambda bi, gi, qi: (bi, qi, gate_col_block))
    gb_spec = pl.BlockSpec((1, LANES), lambda bi, gi, qi: (0, 0))
    tab_spec = pl.BlockSpec((1, s, LANES), lambda bi, gi, qi: (bi, 0, 0))
    return q_spec, k_spec, v_spec, gl_spec, gb_spec, tab_spec


def _sel_attn(proj, sel_mask, expand_bf, gate_b_lane, tables, k_col_block0, gate_col_block, tq=128, tk=512):
    b, s, _ = proj.shape
    g, hd, r = NSA_KV_GROUPS, NSA_HEAD_DIM, NSA_Q_PER_GROUP
    q_spec, k_spec, v_spec, gl_spec, gb_spec, tab_spec = _kv_attn_specs(s, tq, k_col_block0, gate_col_block)
    return pl.pallas_call(
        functools.partial(_sel_attn_kernel, tq=tq, tk=tk),
        grid=(b, g, s // tq),
        in_specs=[q_spec, k_spec, v_spec,
                  pl.BlockSpec((1, 1, tq, LANES), lambda bi, gi, qi: (bi, gi, qi, 0)),
                  pl.BlockSpec((LANES, s), lambda bi, gi, qi: (0, 0)),
                  gl_spec, gb_spec, tab_spec, tab_spec, tab_spec],
        out_specs=q_spec,
        out_shape=jax.ShapeDtypeStruct((b, s, g * r * hd), F32),
        scratch_shapes=[pltpu.VMEM((s, hd), BF16), pltpu.VMEM((s, hd), BF16),
                        pltpu.VMEM((s // tk, tq, tk), F32)],
        compiler_params=_params(("parallel", "parallel", "arbitrary"), 40),
        name="nsa_selected_attn",
    )(proj, proj, proj, sel_mask, expand_bf, proj, gate_b_lane, *tables)


def _win_attn(proj, gate_b_lane, tables, k_col_block0, gate_col_block, tq=128):
    b, s, _ = proj.shape
    g, hd, r = NSA_KV_GROUPS, NSA_HEAD_DIM, NSA_Q_PER_GROUP
    q_spec, k_spec, v_spec, gl_spec, gb_spec, tab_spec = _kv_attn_specs(s, tq, k_col_block0, gate_col_block)
    return pl.pallas_call(
        functools.partial(_win_attn_kernel, tq=tq),
        grid=(b, g, s // tq),
        in_specs=[q_spec, k_spec, v_spec, gl_spec, gb_spec, tab_spec, tab_spec, tab_spec],
        out_specs=q_spec,
        out_shape=jax.ShapeDtypeStruct((b, s, g * r * hd), F32),
        scratch_shapes=[pltpu.VMEM((s, hd), BF16), pltpu.VMEM((s, hd), BF16),
                        pltpu.VMEM((WINDOW // tq + 1, tq, tq), F32)],
        compiler_params=_params(("parallel", "parallel", "arbitrary"), 40),
        name="nsa_window_attn",
    )(proj, proj, proj, proj, gate_b_lane, *tables)


def _pad_cols(w, mult):
    n = w.shape[-1]
    return jnp.pad(w, ((0, 0), (0, (-n) % mult)))


def _lane_row(v):
    v = v.reshape(-1)
    return jnp.pad(v, (0, LANES - v.shape[0])).reshape(1, LANES)


def _nsa_mixer(x2d, b, s, norm_g, tables, w_in, gate_b, cmp_pe, cmp_w1, cmp_w2, w_out):
    g, hd, h = NSA_KV_GROUPS, NSA_HEAD_DIM, NSA_HEADS
    proj = _norm_matmul(x2d, norm_g, _pad_cols(w_in, 512).astype(BF16))
    proj = proj.reshape(b, s, -1)
    q_blocks = h
    cmp0, sel0, win0 = q_blocks, q_blocks + 2 * g, q_blocks + 4 * g
    gate_blk = q_blocks + 6 * g
    gate_b_lane = _lane_row(gate_b)
    kv_cmp = _cmp_mlp(proj, cmp_pe, cmp_w1.astype(BF16), cmp_w2.astype(BF16), cmp0)

    n_cmp_pad = s // CMP_STRIDE
    n_slc = s // SLC_BLOCK
    jj = np.arange(n_cmp_pad)[:, None]
    ss = np.arange(LANES)[None, :]
    lo = np.maximum(jj * CMP_STRIDE, ss * SLC_BLOCK)
    hi = np.minimum(jj * CMP_STRIDE + CMP_BLOCK, ss * SLC_BLOCK + SLC_BLOCK)
    overlap = np.clip(hi - lo, 0, None).astype(np.float32) / CMP_BLOCK
    overlap[(s - CMP_BLOCK) // CMP_STRIDE + 1:, :] = 0.0
    overlap[:, n_slc:] = 0.0
    expand = (np.arange(s)[None, :] // SLC_BLOCK == np.arange(LANES)[:, None]).astype(np.float32)

    o_cmp, sel_mask = _cmp_attn(proj, kv_cmp, gate_b_lane, jnp.asarray(overlap, BF16), gate_blk)
    o_sel = _sel_attn(proj, sel_mask, jnp.asarray(expand, BF16), gate_b_lane, tables, sel0, gate_blk)
    o_win = _win_attn(proj, gate_b_lane, tables, win0, gate_blk)
    t = b * s
    return _out_proj([o_cmp.reshape(t, -1), o_sel.reshape(t, -1), o_win.reshape(t, -1)],
                     w_out.astype(BF16), x2d)


def _mlstm_gate_kernel(gl_ref, gb_ref, tri_ref, li_ref, f_ref):
    pre = gl_ref[0] + gb_ref[...]
    capped = GATE_SOFTCAP * jnp.tanh(pre / GATE_SOFTCAP)
    li_ref[0] = capped
    lf = jax.nn.log_sigmoid(capped)
    rows = tri_ref.shape[0]
    tri = tri_ref[...]
    for sl in range(lf.shape[0] // rows):
        hi, mid, lo = _split3(lf[sl * rows:(sl + 1) * rows, :])
        f_ref[0, pl.ds(sl * rows, rows), :] = _dot(tri, hi) + _dot(tri, mid) + _dot(tri, lo)


def _mlstm_gates(proj, gate_b_lane, gate_col_block, slab=256):
    b, s, _ = proj.shape
    idx = np.arange(slab)
    tri = ((idx[:, None] >= idx[None, :]) &
           (idx[:, None] // MLSTM_CHUNK == idx[None, :] // MLSTM_CHUNK)).astype(np.float32)
    spec = pl.BlockSpec((1, s, LANES), lambda bi: (bi, 0, 0))
    shape = jax.ShapeDtypeStruct((b, s, LANES), F32)
    return pl.pallas_call(
        _mlstm_gate_kernel,
        grid=(b,),
        in_specs=[pl.BlockSpec((1, s, LANES), lambda bi: (bi, 0, gate_col_block)),
                  pl.BlockSpec((1, LANES), lambda bi: (0, 0)),
                  pl.BlockSpec((slab, slab), lambda bi: (0, 0))],
        out_specs=[spec, spec],
        out_shape=[shape, shape],
        compiler_params=_params(("parallel",), 32),
        name="mlstm_gates",
    )(proj, gate_b_lane, jnp.asarray(tri, BF16))


def _mlstm_kernel(q_ref, k_ref, v_ref, og_ref, li_ref, f_ref, hn_ref, o_ref, state_ref):
    h = pl.program_id(1)
    L, dk, dv = MLSTM_CHUNK, MLSTM_QK_DIM, MLSTM_V_DIM
    s = q_ref.shape[1]
    state_ref[...] = jnp.zeros_like(state_ref)
    lane = lax.broadcasted_iota(jnp.int32, (L, LANES), 1)
    causal = lax.broadcasted_iota(jnp.int32, (L, L), 1) <= lax.broadcasted_iota(jnp.int32, (L, L), 0)
    ones_col = (lane < 3).astype(F32)
    ones_row = ((lane >= 3) & (lane < 6)).astype(F32)

    def body(c, m):
        r0 = pl.multiple_of(c * L, L)
        rows = pl.ds(r0, L)
        qb = (q_ref[0, rows, :] * dk ** -0.5).astype(BF16)
        k = k_ref[0, rows, :]
        vb = v_ref[0, rows, :].astype(BF16)
        li = _lane_col(li_ref[0, rows, :], h)
        fc = _lane_col(f_ref[0, rows, :], MLSTM_HEADS + h)
        f_end = fc[L - 1:L, :]
        a = f_end - fc + li
        a_max = jnp.max(a, axis=0, keepdims=True)
        w = jnp.exp(a - a_max)

        fh, fm, fl = _split3(fc)
        gh, gm, gl = _split3(li - fc)
        lhs = jnp.where(lane == 0, fh.astype(F32), jnp.where(lane == 1, fm.astype(F32),
              jnp.where(lane == 2, fl.astype(F32), ones_row)))
        rhs = jnp.where(lane == 3, gh.astype(F32), jnp.where(lane == 4, gm.astype(F32),
              jnp.where(lane == 5, gl.astype(F32), ones_col)))
        d = jnp.where(causal, _dot_nt(lhs.astype(BF16), rhs.astype(BF16)), -jnp.inf)

        inter = fc + m
        m_t = jnp.maximum(inter, jnp.max(d, axis=-1, keepdims=True))
        g_inter = jnp.exp(inter - m_t)
        sqk = _dot_nt(qb, k.astype(BF16)) * jnp.exp(d - m_t)
        state = state_ref[...]
        from_state = _dot(qb, state.astype(BF16))
        num = _dot(sqk.astype(BF16), vb) + g_inter * from_state[:, :dv]
        den = jnp.sum(sqk, axis=-1, keepdims=True) + g_inter * from_state[:, dv:dv + 1]
        hv = num / jnp.maximum(jnp.abs(den), jnp.exp(-m_t))
        hv = hv * lax.rsqrt(jnp.mean(hv * hv, axis=-1, keepdims=True) + NORM_EPS) * hn_ref[0]
        o_ref[0, rows, :] = hv * jax.nn.sigmoid(og_ref[0, rows, :])

        v_ext = jnp.concatenate([vb, jnp.ones((L, LANES), BF16)], axis=1)
        d_state = _dot_tn((w * k).astype(BF16), v_ext)
        m_new = jnp.maximum(f_end + m, a_max)
        state_ref[...] = jnp.exp(f_end + m - m_new) * state + jnp.exp(a_max - m_new) * d_state
        return m_new

    lax.fori_loop(0, s // L, body, jnp.zeros((1, 1), F32))


def _mlstm_core(proj, li, fcum, head_norm):
    b, s, _ = proj.shape
    hh, dk, dv = MLSTM_HEADS, MLSTM_QK_DIM, MLSTM_V_DIM
    v_blk0 = 2 * hh * dk // dv
    og_blk0 = v_blk0 + hh
    gate_spec = pl.BlockSpec((1, s, LANES), lambda bi, hi: (bi, 0, 0))
    return pl.pallas_call(
        _mlstm_kernel,
        grid=(b, hh),
        in_specs=[pl.BlockSpec((1, s, dk), lambda bi, hi: (bi, 0, hi)),
                  pl.BlockSpec((1, s, dk), lambda bi, hi: (bi, 0, hh + hi)),
                  pl.BlockSpec((1, s, dv), lambda bi, hi: (bi, 0, v_blk0 + hi)),
                  pl.BlockSpec((1, s, dv), lambda bi, hi: (bi, 0, og_blk0 + hi)),
                  gate_spec, gate_spec,
                  pl.BlockSpec((1, 1, dv), lambda bi, hi: (hi, 0, 0))],
        out_specs=pl.BlockSpec((1, s, dv), lambda bi, hi: (bi, 0, hi)),
        out_shape=jax.ShapeDtypeStruct((b, s, hh * dv), F32),
        scratch_shapes=[pltpu.VMEM((dk, dv + LANES), F32)],
        compiler_params=_params(("parallel", "parallel"), 40),
        name="mlstm_chunkwise",
    )(proj, proj, proj, proj, li, fcum, head_norm.reshape(hh, 1, dv))


def _mlstm_mixer(x2d, b, s, norm_g, w_in, gate_b, head_norm, w_out):
    hh, dk, dv = MLSTM_HEADS, MLSTM_QK_DIM, MLSTM_V_DIM
    proj = _norm_matmul(x2d, norm_g, _pad_cols(w_in, 512).astype(BF16))
    proj = proj.reshape(b, s, -1)
    gate_blk = (2 * hh * dk + 2 * hh * dv) // LANES
    li, fcum = _mlstm_gates(proj, _lane_row(gate_b), gate_blk)
    hcat = _mlstm_core(proj, li, fcum, head_norm)
    return _out_proj([hcat.reshape(b * s, -1)], w_out.astype(BF16), x2d)


def kernel(x, positions, nsa_w_in, nsa_gate_b, nsa_cmp_pe, nsa_cmp_w1, nsa_cmp_w2, nsa_w_out,
           mlstm_w_in, mlstm_gate_b, mlstm_head_norm, mlstm_w_out,
           norm_mix, norm_ffn, ffn_w_up, ffn_conv_w, ffn_conv_b, ffn_w_down, norm_final):
    b, s, d = x.shape
    depth = norm_mix.shape[0]
    tables = _rope_tables(positions)
    x2d = x.reshape(b * s, d)
    for i in range(depth):
        j = i // 2
        if i % 2 == 0:
            x2d = _nsa_mixer(x2d, b, s, norm_mix[i], tables, nsa_w_in[j], nsa_gate_b[j], nsa_cmp_pe[j],
                             nsa_cmp_w1[j], nsa_cmp_w2[j], nsa_w_out[j])
        else:
            x2d = _mlstm_mixer(x2d, b, s, norm_mix[i], mlstm_w_in[j], mlstm_gate_b[j],
                               mlstm_head_norm[j], mlstm_w_out[j])
        x2d = _ffn(x2d, norm_ffn[i], ffn_w_up[i].astype(BF16), ffn_conv_w[i], ffn_conv_b[i],
                   ffn_w_down[i].astype(BF16), s)
    return _rmsnorm(x2d, norm_final).reshape(b, s, d)
```

```python
import functools

import numpy as np
import jax
import jax.numpy as jnp
from jax import lax
from jax.experimental import pallas as pl
from jax.experimental.pallas import tpu as pltpu

F32 = jnp.float32
BF16 = jnp.bfloat16

LANES = 128
NORM_EPS = 1e-6
NEG_INF = -1e30

NSA_HEADS = 16
NSA_KV_GROUPS = 2
NSA_HEAD_DIM = 128
NSA_Q_PER_GROUP = NSA_HEADS // NSA_KV_GROUPS
CMP_BLOCK = 32
CMP_STRIDE = 16
SLC_BLOCK = 64
N_SELECT = 8
WINDOW = 512
ROPE_THETA = 500000.0
ROPE_DIM = NSA_HEAD_DIM // 4
ROPE_HALF = ROPE_DIM // 2

MLSTM_HEADS = 8
MLSTM_QK_DIM = 128
MLSTM_V_DIM = 256
MLSTM_CHUNK = 64
GATE_SOFTCAP = 15.0

CONV_WIDTH = 3
CONV_HALO = 8

MIB = 1024 * 1024


def _params(semantics, vmem_mib):
    return pltpu.CompilerParams(dimension_semantics=semantics, vmem_limit_bytes=vmem_mib * MIB)


def _dot(a, b):
    return jnp.dot(a, b, preferred_element_type=F32)


def _dot_nt(a, b):
    return lax.dot_general(a, b, (((1,), (1,)), ((), ())), preferred_element_type=F32)


def _dot_tn(a, b):
    return lax.dot_general(a, b, (((0,), (0,)), ((), ())), preferred_element_type=F32)


def _split3(x):
    hi = x.astype(BF16)
    r1 = x - hi.astype(F32)
    mid = r1.astype(BF16)
    lo = (r1 - mid.astype(F32)).astype(BF16)
    return hi, mid, lo


def _rms_rows(x, g):
    ms = jnp.mean(x * x, axis=-1, keepdims=True)
    return x * lax.rsqrt(ms + NORM_EPS) * g


def _lane_col(x, idx):
    lane = lax.broadcasted_iota(jnp.int32, x.shape, 1)
    return jnp.sum(jnp.where(lane == idx, x, 0.0), axis=-1, keepdims=True)


def _norm_matmul_kernel(x_ref, g_ref, w_ref, o_ref, xn_ref):
    @pl.when(pl.program_id(1) == 0)
    def _():
        xn_ref[...] = _rms_rows(x_ref[...], g_ref[...]).astype(BF16)

    o_ref[...] = _dot(xn_ref[...], w_ref[...])


def _norm_matmul(x2d, g, w_bf, tm=512, tn=512):
    t, d = x2d.shape
    n = w_bf.shape[1]
    return pl.pallas_call(
        _norm_matmul_kernel,
        grid=(t // tm, n // tn),
        in_specs=[pl.BlockSpec((tm, d), lambda i, j: (i, 0)),
                  pl.BlockSpec((1, d), lambda i, j: (0, 0)),
                  pl.BlockSpec((d, tn), lambda i, j: (0, j))],
        out_specs=pl.BlockSpec((tm, tn), lambda i, j: (i, j)),
        out_shape=jax.ShapeDtypeStruct((t, n), F32),
        scratch_shapes=[pltpu.VMEM((tm, d), BF16)],
        compiler_params=_params(("parallel", "arbitrary"), 40),
        name="norm_matmul",
    )(x2d, g.reshape(1, d), w_bf)


def _out_proj_kernel(*refs, n_in):
    a_refs = refs[:n_in]
    w_ref, r_ref, o_ref, a_bf = refs[n_in:]

    @pl.when(pl.program_id(1) == 0)
    def _():
        a = a_refs[0][...]
        for ar in a_refs[1:]:
            a = a + ar[...]
        a_bf[...] = a.astype(BF16)

    o_ref[...] = r_ref[...] + _dot(a_bf[...], w_ref[...])


def _out_proj(a_list, w_bf, resid, tm=512, tn=512):
    t, k = a_list[0].shape
    n = w_bf.shape[1]
    n_in = len(a_list)
    return pl.pallas_call(
        functools.partial(_out_proj_kernel, n_in=n_in),
        grid=(t // tm, n // tn),
        in_specs=[pl.BlockSpec((tm, k), lambda i, j: (i, 0))] * n_in
        + [pl.BlockSpec((k, tn), lambda i, j: (0, j)),
           pl.BlockSpec((tm, tn), lambda i, j: (i, j))],
        out_specs=pl.BlockSpec((tm, tn), lambda i, j: (i, j)),
        out_shape=jax.ShapeDtypeStruct((t, n), F32),
        scratch_shapes=[pltpu.VMEM((tm, k), BF16)],
        compiler_params=_params(("parallel", "arbitrary"), 48),
        name="out_proj",
    )(*a_list, w_bf, resid)


def _ffn_kernel(x_ref, g_ref, wg_ref, wv_ref, cwg_ref, cwv_ref, cbg_ref, cbv_ref, wd_ref, xr_ref, o_ref,
                xn_ref, h_ref, ug_ref, uv_ref, carry_g, carry_v, *, tm, nf, tiles_per_seq):
    i = pl.program_id(0)
    st = pl.program_id(1)

    @pl.when(st == 0)
    def _():
        xn_ref[...] = _rms_rows(x_ref[...], g_ref[...]).astype(BF16)

    @pl.when(st < nf)
    def _():
        xn = xn_ref[...]
        seq_start = (i % tiles_per_seq) == 0

        def conv_branch(w_ref, cw_ref, cb_ref, u_ref, carry):
            u = _dot(xn, w_ref[...])
            u_ref[pl.ds(CONV_HALO, tm), :] = u

            @pl.when(seq_start)
            def _():
                u_ref[pl.ds(0, CONV_HALO), :] = jnp.zeros((CONV_HALO, u.shape[1]), F32)

            @pl.when(jnp.logical_not(seq_start))
            def _():
                u_ref[pl.ds(0, CONV_HALO), :] = carry[st]

            carry[st] = u[tm - CONV_HALO:, :]
            cw = cw_ref[...]
            c = cb_ref[...] + cw[0:1, :] * u_ref[pl.ds(CONV_HALO - 2, tm), :]
            c = c + cw[1:2, :] * u_ref[pl.ds(CONV_HALO - 1, tm), :]
            return c + cw[2:3, :] * u

        gate = conv_branch(wg_ref, cwg_ref, cbg_ref, ug_ref, carry_g)
        val = conv_branch(wv_ref, cwv_ref, cbv_ref, uv_ref, carry_v)
        h_ref[st] = (jax.nn.silu(gate) * val).astype(BF16)

    @pl.when(st >= nf)
    def _():
        acc = xr_ref[...]
        for f in range(nf):
            acc = acc + _dot(h_ref[f], wd_ref[f])
        o_ref[...] = acc


def _ffn(x2d, g, w_up_bf, conv_w, conv_b, w_down_bf, seq, tm=512, tf=512, tn=512):
    t, d = x2d.shape
    d_ff = w_down_bf.shape[0]
    nf = d_ff // tf
    nn = d // tn
    kern = functools.partial(_ffn_kernel, tm=tm, nf=nf, tiles_per_seq=seq // tm)

    def up(st):
        return jnp.minimum(st, nf - 1)

    def down(st):
        return jnp.maximum(st - nf, 0)

    return pl.pallas_call(
        kern,
        grid=(t // tm, nf + nn),
        in_specs=[pl.BlockSpec((tm, d), lambda i, st: (i, 0)),
                  pl.BlockSpec((1, d), lambda i, st: (0, 0)),
                  pl.BlockSpec((d, tf), lambda i, st: (0, up(st))),
                  pl.BlockSpec((d, tf), lambda i, st: (0, up(st) + nf)),
                  pl.BlockSpec((CONV_WIDTH, tf), lambda i, st: (0, up(st))),
                  pl.BlockSpec((CONV_WIDTH, tf), lambda i, st: (0, up(st) + nf)),
                  pl.BlockSpec((1, tf), lambda i, st: (0, up(st))),
                  pl.BlockSpec((1, tf), lambda i, st: (0, up(st) + nf)),
                  pl.BlockSpec((nf, tf, tn), lambda i, st: (0, 0, down(st))),
                  pl.BlockSpec((tm, tn), lambda i, st: (i, down(st)))],
        out_specs=pl.BlockSpec((tm, tn), lambda i, st: (i, down(st))),
        out_shape=jax.ShapeDtypeStruct((t, d), F32),
        scratch_shapes=[pltpu.VMEM((tm, d), BF16),
                        pltpu.VMEM((nf, tm, tf), BF16),
                        pltpu.VMEM((tm + CONV_HALO, tf), F32),
                        pltpu.VMEM((tm + CONV_HALO, tf), F32),
                        pltpu.VMEM((nf, CONV_HALO, tf), F32),
                        pltpu.VMEM((nf, CONV_HALO, tf), F32)],
        compiler_params=_params(("arbitrary", "arbitrary"), 56),
        name="conv_glu_ffn",
    )(x2d, g.reshape(1, d), w_up_bf, w_up_bf, conv_w, conv_w,
      conv_b.reshape(1, -1), conv_b.reshape(1, -1), w_down_bf.reshape(nf, tf, d), x2d)


def _rmsnorm_kernel(x_ref, g_ref, o_ref):
    o_ref[...] = _rms_rows(x_ref[...], g_ref[...])


def _rmsnorm(x2d, g, tm=512):
    t, d = x2d.shape
    return pl.pallas_call(
        _rmsnorm_kernel,
        grid=(t // tm,),
        in_specs=[pl.BlockSpec((tm, d), lambda i: (i, 0)), pl.BlockSpec((1, d), lambda i: (0, 0))],
        out_specs=pl.BlockSpec((tm, d), lambda i: (i, 0)),
        out_shape=jax.ShapeDtypeStruct((t, d), F32),
        compiler_params=_params(("parallel",), 32),
        name="final_rmsnorm",
    )(x2d, g.reshape(1, d))


def _rope_table_kernel(pos_ref, inv_ref, c_ref, sa_ref, sb_ref):
    ang = pos_ref[0].astype(F32) * inv_ref[...]
    lane = lax.broadcasted_iota(jnp.int32, ang.shape, 1)
    cos = jnp.cos(ang)
    sin = jnp.sin(ang)
    c_ref[0] = jnp.where(lane < ROPE_DIM, cos, 1.0)
    sa_ref[0] = jnp.where((lane >= ROPE_HALF) & (lane < ROPE_DIM), sin, 0.0)
    sb_ref[0] = jnp.where(lane < ROPE_HALF, -sin, 0.0)


def _rope_tables(positions):
    b, s = positions.shape
    inv = jnp.power(ROPE_THETA, -jnp.arange(0, ROPE_DIM, 2, dtype=F32) / ROPE_DIM)
    inv_lane = jnp.concatenate([inv, inv, jnp.zeros((LANES - ROPE_DIM,), F32)]).reshape(1, LANES)
    spec = pl.BlockSpec((1, s, LANES), lambda i: (i, 0, 0))
    shape = jax.ShapeDtypeStruct((b, s, LANES), F32)
    return pl.pallas_call(
        _rope_table_kernel,
        grid=(b,),
        in_specs=[pl.BlockSpec((1, s, 1), lambda i: (i, 0, 0)),
                  pl.BlockSpec((1, LANES), lambda i: (0, 0))],
        out_specs=[spec, spec, spec],
        out_shape=[shape, shape, shape],
        compiler_params=_params(("parallel",), 32),
        name="rope_tables",
    )(positions.reshape(b, s, 1), inv_lane)


def _rope(x, c, sa, sb):
    return x * c + pltpu.roll(x, ROPE_HALF, 1) * sa + pltpu.roll(x, LANES - ROPE_HALF, 1) * sb


def _cmp_mlp_kernel(x_ref, pe_ref, w1_ref, w2_ref, o_ref):
    n = o_ref.shape[-2]
    acc_lo = jnp.zeros((n, NSA_HEAD_DIM), F32)
    acc_hi = jnp.zeros((n, NSA_HEAD_DIM), F32)
    for l in range(CMP_STRIDE):
        xs = x_ref[0, pl.ds(l, n, stride=CMP_STRIDE), :]
        acc_lo += _dot((xs + pe_ref[0, l:l + 1, :]).astype(BF16), w1_ref[0, l])
        acc_hi += _dot((xs + pe_ref[0, CMP_STRIDE + l:CMP_STRIDE + l + 1, :]).astype(BF16),
                       w1_ref[0, CMP_STRIDE + l])
    hid = jax.nn.gelu(acc_lo + pltpu.roll(acc_hi, n - 1, 0))
    o_ref[0, 0, 0] = _dot(hid.astype(BF16), w2_ref[0])


def _cmp_mlp(proj, pe, w1_bf, w2_bf, col_block0):
    b, s, _ = proj.shape
    g, hd = NSA_KV_GROUPS, NSA_HEAD_DIM
    n = s // CMP_STRIDE
    return pl.pallas_call(
        _cmp_mlp_kernel,
        grid=(b, 2, g),
        in_specs=[pl.BlockSpec((1, s, hd), lambda bi, c, gi: (bi, 0, col_block0 + c * g + gi)),
                  pl.BlockSpec((1, CMP_BLOCK, hd), lambda bi, c, gi: (c, 0, 0)),
                  pl.BlockSpec((1, CMP_BLOCK, hd, hd), lambda bi, c, gi: (c, 0, 0, 0)),
                  pl.BlockSpec((1, hd, hd), lambda bi, c, gi: (c, 0, 0))],
        out_specs=pl.BlockSpec((1, 1, 1, n, hd), lambda bi, c, gi: (bi, c, gi, 0, 0)),
        out_shape=jax.ShapeDtypeStruct((b, 2, g, n, hd), F32),
        compiler_params=_params(("parallel", "parallel", "parallel"), 32),
        name="nsa_cmp_mlp",
    )(proj, pe, w1_bf, w2_bf)


def _cmp_attn_kernel(q_ref, kc_ref, vc_ref, gl_ref, gb_ref, ov_ref, o_ref, sel_ref, *, ts, n_slc):
    g = pl.program_id(1)
    si = pl.program_id(2)
    hd = NSA_HEAD_DIM
    scale = hd ** -0.5
    lane = lax.broadcasted_iota(jnp.int32, (ts, LANES), 1)
    t = si * ts + lax.broadcasted_iota(jnp.int32, (ts, LANES), 0)
    cmask = (lane * CMP_STRIDE + (CMP_BLOCK - 1)) <= t
    cmask_f = cmask.astype(F32)
    kc = kc_ref[0, 0, 0].astype(BF16)
    vc = vc_ref[0, 0, 0].astype(BF16)
    gates = jax.nn.sigmoid(gl_ref[0] + gb_ref[...])
    psum = jnp.zeros((ts, LANES), F32)
    for r in range(NSA_Q_PER_GROUP):
        q = q_ref[0, :, r * hd:(r + 1) * hd].astype(BF16)
        s = jnp.where(cmask, _dot_nt(q, kc) * scale, NEG_INF)
        e = jnp.exp(s - jnp.max(s, axis=-1, keepdims=True))
        p = e / jnp.sum(e, axis=-1, keepdims=True) * cmask_f
        psum = psum + p
        o = _dot(p.astype(BF16), vc)
        o_ref[0, :, r * hd:(r + 1) * hd] = o * _lane_col(gates, g * NSA_Q_PER_GROUP + r)

    ov = ov_ref[...]
    hi, mid, lo = _split3(psum)
    imp = _dot(hi, ov) + _dot(mid, ov) + _dot(lo, ov)
    q_blk = t // SLC_BLOCK
    forced = (lane == 0) | (lane == q_blk)
    score = jnp.where(forced, jnp.inf, jnp.where(lane <= q_blk, imp, -jnp.inf))
    live = lane < n_slc
    lane_f = lane.astype(F32)
    taken = jnp.zeros((ts, LANES), jnp.int32)
    for _ in range(min(N_SELECT, n_slc)):
        cand_ok = live & (taken == 0)
        best = jnp.max(jnp.where(cand_ok, score, -jnp.inf), axis=-1, keepdims=True)
        first = jnp.min(jnp.where(cand_ok & (score == best), lane_f, float(LANES)), axis=-1, keepdims=True)
        taken = jnp.where(lane_f == first, 1, taken)
    sel_ref[0, 0] = taken.astype(F32)


def _cmp_attn(proj, kv_cmp, gate_b_lane, overlap_bf, gate_col_block, ts=256):
    b, s, _ = proj.shape
    g, hd, r = NSA_KV_GROUPS, NSA_HEAD_DIM, NSA_Q_PER_GROUP
    n = kv_cmp.shape[-2]
    kern = functools.partial(_cmp_attn_kernel, ts=ts, n_slc=s // SLC_BLOCK)
    return pl.pallas_call(
        kern,
        grid=(b, g, s // ts),
        in_specs=[pl.BlockSpec((1, ts, r * hd), lambda bi, gi, si: (bi, si, gi)),
                  pl.BlockSpec((1, 1, 1, n, hd), lambda bi, gi, si: (bi, 0, gi, 0, 0)),
                  pl.BlockSpec((1, 1, 1, n, hd), lambda bi, gi, si: (bi, 1, gi, 0, 0)),
                  pl.BlockSpec((1, ts, LANES), lambda bi, gi, si: (bi, si, gate_col_block)),
                  pl.BlockSpec((1, LANES), lambda bi, gi, si: (0, 0)),
                  pl.BlockSpec((n, LANES), lambda bi, gi, si: (0, 0))],
        out_specs=[pl.BlockSpec((1, ts, r * hd), lambda bi, gi, si: (bi, si, gi)),
                   pl.BlockSpec((1, 1, ts, LANES), lambda bi, gi, si: (bi, gi, si, 0))],
        out_shape=[jax.ShapeDtypeStruct((b, s, g * r * hd), F32),
                   jax.ShapeDtypeStruct((b, g, s, LANES), F32)],
        compiler_params=_params(("parallel", "parallel", "parallel"), 32),
        name="nsa_cmp_attn_topk",
    )(proj, kv_cmp, kv_cmp, proj, gate_b_lane, overlap_bf)


def _sel_win_kernel(q_ref, ks_ref, vs_ref, kw_ref, vw_ref, sm_ref, ex_ref, gl_ref, gb_ref,
                    c_ref, sa_ref, sb_ref, o_ref, ks_rot, vs_bf, kw_rot, vw_bf, q_st, *, tq, tk):
    g = pl.program_id(1)
    qi = pl.program_id(2)
    hd, nr = NSA_HEAD_DIM, NSA_Q_PER_GROUP
    s_len = ks_rot.shape[0]
    scale = hd ** -0.5

    @pl.when(qi == 0)
    def _():
        c, sa, sb = c_ref[0], sa_ref[0], sb_ref[0]
        ks_rot[...] = _rope(ks_ref[0], c, sa, sb).astype(BF16)
        kw_rot[...] = _rope(kw_ref[0], c, sa, sb).astype(BF16)
        vs_bf[...] = vs_ref[0].astype(BF16)
        vw_bf[...] = vw_ref[0].astype(BF16)

    q0 = pl.multiple_of(qi * tq, tq)
    ropes = (c_ref[0, pl.ds(q0, tq), :], sa_ref[0, pl.ds(q0, tq), :], sb_ref[0, pl.ds(q0, tq), :])
    for r in range(nr):
        q_st[r * tq:(r + 1) * tq, :] = _rope(q_ref[0, :, r * hd:(r + 1) * hd], *ropes).astype(BF16)
    gates = jax.nn.sigmoid(gl_ref[0] + gb_ref[...])

    def attend(k, v, bias):
        n = k.shape[0]
        s = (_dot_nt(q_st[...], k) * scale).reshape(nr, tq, n) + bias[None]
        p = jnp.exp(s - jnp.max(s, axis=-1, keepdims=True))
        l = jnp.sum(p, axis=-1, keepdims=True)
        o = _dot(p.reshape(nr * tq, n).astype(BF16), v)
        return o.reshape(nr, tq, hd) / l

    wk = WINDOW + tq
    start = pl.multiple_of(jnp.maximum(q0 - WINDOW, 0), tq)
    dist = (q0 - start) + lax.broadcasted_iota(jnp.int32, (tq, wk), 0) - lax.broadcasted_iota(jnp.int32, (tq, wk), 1)
    bias_w = jnp.where((dist >= 0) & (dist < WINDOW), 0.0, NEG_INF)
    o_win = attend(kw_rot[pl.ds(start, wk), :], vw_bf[pl.ds(start, wk), :], bias_w)
    for r in range(nr):
        o_ref[0, :, r * hd:(r + 1) * hd] = o_win[r] * _lane_col(gates, 2 * NSA_HEADS + g * nr + r)

    sm = sm_ref[0, 0].astype(BF16)
    n_chunks = (q0 + tq + tk - 1) // tk
    for n in range(1, s_len // tk + 1):
        @pl.when(n_chunks == n)
        def _():
            nk = n * tk
            member = _dot(sm, ex_ref[:, :nk])
            t = q0 + lax.broadcasted_iota(jnp.int32, (tq, nk), 0)
            kpos = lax.broadcasted_iota(jnp.int32, (tq, nk), 1)
            bias_s = jnp.where((member > 0.5) & (kpos <= t), 0.0, NEG_INF)
            o_sel = attend(ks_rot[:nk, :], vs_bf[:nk, :], bias_s)
            for r in range(nr):
                o_ref[0, :, r * hd:(r + 1) * hd] += o_sel[r] * _lane_col(gates, NSA_HEADS + g * nr + r)


def _sel_win_attn(proj, sel_mask, expand_bf, gate_b_lane, tables, sel_col_block0, win_col_block0,
                  gate_col_block, tq=128, tk=512):
    b, s, _ = proj.shape
    g, hd, r = NSA_KV_GROUPS, NSA_HEAD_DIM, NSA_Q_PER_GROUP
    q_spec = pl.BlockSpec((1, tq, r * hd), lambda bi, gi, qi: (bi, qi, gi))

    def kv_spec(col_block):
        return pl.BlockSpec((1, s, hd), lambda bi, gi, qi: (bi, 0, col_block + gi))

    tab_spec = pl.BlockSpec((1, s, LANES), lambda bi, gi, qi: (bi, 0, 0))
    return pl.pallas_call(
        functools.partial(_sel_win_kernel, tq=tq, tk=tk),
        grid=(b, g, s // tq),
        in_specs=[q_spec, kv_spec(sel_col_block0), kv_spec(sel_col_block0 + g),
                  kv_spec(win_col_block0), kv_spec(win_col_block0 + g),
                  pl.BlockSpec((1, 1, tq, LANES), lambda bi, gi, qi: (bi, gi, qi, 0)),
                  pl.BlockSpec((LANES, s), lambda bi, gi, qi: (0, 0)),
                  pl.BlockSpec((1, tq, LANES), lambda bi, gi, qi: (bi, qi, gate_col_block)),
                  pl.BlockSpec((1, LANES), lambda bi, gi, qi: (0, 0)),
                  tab_spec, tab_spec, tab_spec],
        out_specs=q_spec,
        out_shape=jax.ShapeDtypeStruct((b, s, g * r * hd), F32),
        scratch_shapes=[pltpu.VMEM((s, hd), BF16)] * 4 + [pltpu.VMEM((r * tq, hd), BF16)],
        compiler_params=_params(("parallel", "parallel", "arbitrary"), 56),
        name="nsa_sel_win_attn",
    )(proj, proj, proj, proj, proj, sel_mask, expand_bf, proj, gate_b_lane, *tables)


def _pad_cols(w, mult):
    n = w.shape[-1]
    return jnp.pad(w, ((0, 0), (0, (-n) % mult)))


def _lane_row(v):
    v = v.reshape(-1)
    return jnp.pad(v, (0, LANES - v.shape[0])).reshape(1, LANES)


def _nsa_mixer(x2d, b, s, norm_g, tables, w_in, gate_b, cmp_pe, cmp_w1, cmp_w2, w_out):
    g, hd, h = NSA_KV_GROUPS, NSA_HEAD_DIM, NSA_HEADS
    proj = _norm_matmul(x2d, norm_g, _pad_cols(w_in, 512).astype(BF16))
    proj = proj.reshape(b, s, -1)
    q_blocks = h
    cmp0, sel0, win0 = q_blocks, q_blocks + 2 * g, q_blocks + 4 * g
    gate_blk = q_blocks + 6 * g
    gate_b_lane = _lane_row(gate_b)
    kv_cmp = _cmp_mlp(proj, cmp_pe, cmp_w1.astype(BF16), cmp_w2.astype(BF16), cmp0)

    n_cmp_pad = s // CMP_STRIDE
    n_slc = s // SLC_BLOCK
    jj = np.arange(n_cmp_pad)[:, None]
    ss = np.arange(LANES)[None, :]
    lo = np.maximum(jj * CMP_STRIDE, ss * SLC_BLOCK)
    hi = np.minimum(jj * CMP_STRIDE + CMP_BLOCK, ss * SLC_BLOCK + SLC_BLOCK)
    overlap = np.clip(hi - lo, 0, None).astype(np.float32) / CMP_BLOCK
    overlap[(s - CMP_BLOCK) // CMP_STRIDE + 1:, :] = 0.0
    overlap[:, n_slc:] = 0.0
    expand = (np.arange(s)[None, :] // SLC_BLOCK == np.arange(LANES)[:, None]).astype(np.float32)

    o_cmp, sel_mask = _cmp_attn(proj, kv_cmp, gate_b_lane, jnp.asarray(overlap, BF16), gate_blk)
    o_rot = _sel_win_attn(proj, sel_mask, jnp.asarray(expand, BF16), gate_b_lane, tables, sel0, win0, gate_blk)
    t = b * s
    return _out_proj([o_cmp.reshape(t, -1), o_rot.reshape(t, -1)], w_out.astype(BF16), x2d)


def _mlstm_gate_kernel(gl_ref, gb_ref, tri_ref, li_ref, f_ref):
    pre = gl_ref[0] + gb_ref[...]
    capped = GATE_SOFTCAP * jnp.tanh(pre / GATE_SOFTCAP)
    li_ref[0] = capped
    lf = jax.nn.log_sigmoid(capped)
    rows = tri_ref.shape[0]
    tri = tri_ref[...]
    for sl in range(lf.shape[0] // rows):
        hi, mid, lo = _split3(lf[sl * rows:(sl + 1) * rows, :])
        f_ref[0, pl.ds(sl * rows, rows), :] = _dot(tri, hi) + _dot(tri, mid) + _dot(tri, lo)


def _mlstm_gates(proj, gate_b_lane, gate_col_block, slab=256):
    b, s, _ = proj.shape
    idx = np.arange(slab)
    tri = ((idx[:, None] >= idx[None, :]) &
           (idx[:, None] // MLSTM_CHUNK == idx[None, :] // MLSTM_CHUNK)).astype(np.float32)
    spec = pl.BlockSpec((1, s, LANES), lambda bi: (bi, 0, 0))
    shape = jax.ShapeDtypeStruct((b, s, LANES), F32)
    return pl.pallas_call(
        _mlstm_gate_kernel,
        grid=(b,),
        in_specs=[pl.BlockSpec((1, s, LANES), lambda bi: (bi, 0, gate_col_block)),
                  pl.BlockSpec((1, LANES), lambda bi: (0, 0)),
                  pl.BlockSpec((slab, slab), lambda bi: (0, 0))],
        out_specs=[spec, spec],
        out_shape=[shape, shape],
        compiler_params=_params(("parallel",), 32),
        name="mlstm_gates",
    )(proj, gate_b_lane, jnp.asarray(tri, BF16))


def _mlstm_kernel(q_ref, k_ref, v_ref, og_ref, li_ref, f_ref, hn_ref, o_ref, state_ref):
    h = pl.program_id(1)
    L, dk, dv = MLSTM_CHUNK, MLSTM_QK_DIM, MLSTM_V_DIM
    s = q_ref.shape[1]
    state_ref[...] = jnp.zeros_like(state_ref)
    lane = lax.broadcasted_iota(jnp.int32, (L, LANES), 1)
    causal = lax.broadcasted_iota(jnp.int32, (L, L), 1) <= lax.broadcasted_iota(jnp.int32, (L, L), 0)
    ones_col = (lane < 3).astype(F32)
    ones_row = ((lane >= 3) & (lane < 6)).astype(F32)

    def body(c, m):
        r0 = pl.multiple_of(c * L, L)
        rows = pl.ds(r0, L)
        qb = (q_ref[0, rows, :] * dk ** -0.5).astype(BF16)
        k = k_ref[0, rows, :]
        vb = v_ref[0, rows, :].astype(BF16)
        li = _lane_col(li_ref[0, rows, :], h)
        fc = _lane_col(f_ref[0, rows, :], MLSTM_HEADS + h)
        f_end = fc[L - 1:L, :]
        a = f_end - fc + li
        a_max = jnp.max(a, axis=0, keepdims=True)
        w = jnp.exp(a - a_max)

        fh, fm, fl = _split3(fc)
        gh, gm, gl = _split3(li - fc)
        lhs = jnp.where(lane == 0, fh.astype(F32), jnp.where(lane == 1, fm.astype(F32),
              jnp.where(lane == 2, fl.astype(F32), ones_row)))
        rhs = jnp.where(lane == 3, gh.astype(F32), jnp.where(lane == 4, gm.astype(F32),
              jnp.where(lane == 5, gl.astype(F32), ones_col)))
        d = jnp.where(causal, _dot_nt(lhs.astype(BF16), rhs.astype(BF16)), -jnp.inf)

        inter = fc + m
        m_t = jnp.maximum(inter, jnp.max(d, axis=-1, keepdims=True))
        g_inter = jnp.exp(inter - m_t)
        sqk = _dot_nt(qb, k.astype(BF16)) * jnp.exp(d - m_t)
        state = state_ref[...]
        from_state = _dot(qb, state.astype(BF16))
        num = _dot(sqk.astype(BF16), vb) + g_inter * from_state[:, :dv]
        den = jnp.sum(sqk, axis=-1, keepdims=True) + g_inter * from_state[:, dv:dv + 1]
        hv = num / jnp.maximum(jnp.abs(den), jnp.exp(-m_t))
        hv = hv * lax.rsqrt(jnp.mean(hv * hv, axis=-1, keepdims=True) + NORM_EPS) * hn_ref[0]
        o_ref[0, rows, :] = hv * jax.nn.sigmoid(og_ref[0, rows, :])

        v_ext = jnp.concatenate([vb, jnp.ones((L, LANES), BF16)], axis=1)
        d_state = _dot_tn((w * k).astype(BF16), v_ext)
        m_new = jnp.maximum(f_end + m, a_max)
        state_ref[...] = jnp.exp(f_end + m - m_new) * state + jnp.exp(a_max - m_new) * d_state
        return m_new

    lax.fori_loop(0, s // L, body, jnp.zeros((1, 1), F32), unroll=4)


def _mlstm_core(proj, li, fcum, head_norm):
    b, s, _ = proj.shape
    hh, dk, dv = MLSTM_HEADS, MLSTM_QK_DIM, MLSTM_V_DIM
    v_blk0 = 2 * hh * dk // dv
    og_blk0 = v_blk0 + hh
    gate_spec = pl.BlockSpec((1, s, LANES), lambda bi, hi: (bi, 0, 0))
    return pl.pallas_call(
        _mlstm_kernel,
        grid=(b, hh),
        in_specs=[pl.BlockSpec((1, s, dk), lambda bi, hi: (bi, 0, hi)),
                  pl.BlockSpec((1, s, dk), lambda bi, hi: (bi, 0, hh + hi)),
                  pl.BlockSpec((1, s, dv), lambda bi, hi: (bi, 0, v_blk0 + hi)),
                  pl.BlockSpec((1, s, dv), lambda bi, hi: (bi, 0, og_blk0 + hi)),
                  gate_spec, gate_spec,
                  pl.BlockSpec((1, 1, dv), lambda bi, hi: (hi, 0, 0))],
        out_specs=pl.BlockSpec((1, s, dv), lambda bi, hi: (bi, 0, hi)),
        out_shape=jax.ShapeDtypeStruct((b, s, hh * dv), F32),
        scratch_shapes=[pltpu.VMEM((dk, dv + LANES), F32)],
        compiler_params=_params(("parallel", "parallel"), 40),
        name="mlstm_chunkwise",
    )(proj, proj, proj, proj, li, fcum, head_norm.reshape(hh, 1, dv))


def _mlstm_mixer(x2d, b, s, norm_g, w_in, gate_b, head_norm, w_out):
    hh, dk, dv = MLSTM_HEADS, MLSTM_QK_DIM, MLSTM_V_DIM
    proj = _norm_matmul(x2d, norm_g, _pad_cols(w_in, 512).astype(BF16))
    proj = proj.reshape(b, s, -1)
    gate_blk = (2 * hh * dk + 2 * hh * dv) // LANES
    li, fcum = _mlstm_gates(proj, _lane_row(gate_b), gate_blk)
    hcat = _mlstm_core(proj, li, fcum, head_norm)
    return _out_proj([hcat.reshape(b * s, -1)], w_out.astype(BF16), x2d)


def kernel(x, positions, nsa_w_in, nsa_gate_b, nsa_cmp_pe, nsa_cmp_w1, nsa_cmp_w2, nsa_w_out,
           mlstm_w_in, mlstm_gate_b, mlstm_head_norm, mlstm_w_out,
           norm_mix, norm_ffn, ffn_w_up, ffn_conv_w, ffn_conv_b, ffn_w_down, norm_final):
    b, s, d = x.shape
    depth = norm_mix.shape[0]
    tables = _rope_tables(positions)
    x2d = x.reshape(b * s, d)
    for i in range(depth):
        j = i // 2
        if i % 2 == 0:
            x2d = _nsa_mixer(x2d, b, s, norm_mix[i], tables, nsa_w_in[j], nsa_gate_b[j], nsa_cmp_pe[j],
                             nsa_cmp_w1[j], nsa_cmp_w2[j], nsa_w_out[j])
        else:
            x2d = _mlstm_mixer(x2d, b, s, norm_mix[i], mlstm_w_in[j], mlstm_gate_b[j],
                               mlstm_head_norm[j], mlstm_w_out[j])
        x2d = _ffn(x2d, norm_ffn[i], ffn_w_up[i].astype(BF16), ffn_conv_w[i], ffn_conv_b[i],
                   ffn_w_down[i].astype(BF16), s)
    return _rmsnorm(x2d, norm_final).reshape(b, s, d)
```

```python
import functools

import numpy as np
import jax
import jax.numpy as jnp
from jax import lax
from jax.experimental import pallas as pl
from jax.experimental.pallas import tpu as pltpu

F32 = jnp.float32
BF16 = jnp.bfloat16

LANES = 128
NORM_EPS = 1e-6
NEG_INF = -1e30

NSA_HEADS = 16
NSA_KV_GROUPS = 2
NSA_HEAD_DIM = 128
NSA_Q_PER_GROUP = NSA_HEADS // NSA_KV_GROUPS
CMP_BLOCK = 32
CMP_STRIDE = 16
SLC_BLOCK = 64
N_SELECT = 8
WINDOW = 512
ROPE_THETA = 500000.0
ROPE_DIM = NSA_HEAD_DIM // 4
ROPE_HALF = ROPE_DIM // 2

MLSTM_HEADS = 8
MLSTM_QK_DIM = 128
MLSTM_V_DIM = 256
MLSTM_CHUNK = 64
GATE_SOFTCAP = 15.0

CONV_WIDTH = 3
CONV_HALO = 8
CONV_ROWS = 16

MIB = 1024 * 1024


def _params(semantics, vmem_mib):
    return pltpu.CompilerParams(dimension_semantics=semantics, vmem_limit_bytes=vmem_mib * MIB)


def _dot(a, b):
    return jnp.dot(a, b, preferred_element_type=F32)


def _dot_nt(a, b):
    return lax.dot_general(a, b, (((1,), (1,)), ((), ())), preferred_element_type=F32)


def _dot_tn(a, b):
    return lax.dot_general(a, b, (((0,), (0,)), ((), ())), preferred_element_type=F32)


def _split3(x):
    hi = x.astype(BF16)
    r1 = x - hi.astype(F32)
    mid = r1.astype(BF16)
    lo = (r1 - mid.astype(F32)).astype(BF16)
    return hi, mid, lo


def _rms_rows(x, g):
    ms = jnp.mean(x * x, axis=-1, keepdims=True)
    return x * lax.rsqrt(ms + NORM_EPS) * g


def _lane_col(x, idx):
    lane = lax.broadcasted_iota(jnp.int32, x.shape, 1)
    return jnp.sum(jnp.where(lane == idx, x, 0.0), axis=-1, keepdims=True)


def _norm_matmul_kernel(x_ref, g_ref, w_ref, o_ref, xn_ref):
    @pl.when(pl.program_id(1) == 0)
    def _():
        xn_ref[...] = _rms_rows(x_ref[...], g_ref[...]).astype(BF16)

    o_ref[...] = _dot(xn_ref[...], w_ref[...])


def _norm_matmul(x2d, g, w_bf, tm=1024, tn=512):
    t, d = x2d.shape
    n = w_bf.shape[1]
    return pl.pallas_call(
        _norm_matmul_kernel,
        grid=(t // tm, n // tn),
        in_specs=[pl.BlockSpec((tm, d), lambda i, j: (i, 0)),
                  pl.BlockSpec((1, d), lambda i, j: (0, 0)),
                  pl.BlockSpec((d, tn), lambda i, j: (0, j))],
        out_specs=pl.BlockSpec((tm, tn), lambda i, j: (i, j)),
        out_shape=jax.ShapeDtypeStruct((t, n), F32),
        scratch_shapes=[pltpu.VMEM((tm, d), BF16)],
        compiler_params=_params(("parallel", "arbitrary"), 40),
        name="norm_matmul",
    )(x2d, g.reshape(1, d), w_bf)


def _out_proj_kernel(*refs, n_in):
    a_refs = refs[:n_in]
    w_ref, r_ref, o_ref, a_bf = refs[n_in:]

    @pl.when(pl.program_id(1) == 0)
    def _():
        a = a_refs[0][...]
        for ar in a_refs[1:]:
            a = a + ar[...]
        a_bf[...] = a.astype(BF16)

    o_ref[...] = r_ref[...] + _dot(a_bf[...], w_ref[...])


def _out_proj(a_list, w_bf, resid, tm=1024, tn=512):
    t, k = a_list[0].shape
    n = w_bf.shape[1]
    n_in = len(a_list)
    return pl.pallas_call(
        functools.partial(_out_proj_kernel, n_in=n_in),
        grid=(t // tm, n // tn),
        in_specs=[pl.BlockSpec((tm, k), lambda i, j: (i, 0))] * n_in
        + [pl.BlockSpec((k, tn), lambda i, j: (0, j)),
           pl.BlockSpec((tm, tn), lambda i, j: (i, j))],
        out_specs=pl.BlockSpec((tm, tn), lambda i, j: (i, j)),
        out_shape=jax.ShapeDtypeStruct((t, n), F32),
        scratch_shapes=[pltpu.VMEM((tm, k), BF16)],
        compiler_params=_params(("parallel", "arbitrary"), 56),
        name="out_proj",
    )(*a_list, w_bf, resid)


def _ffn_kernel(x_ref, g_ref, wg_ref, wv_ref, cwg_ref, cwv_ref, cbg_ref, cbv_ref, wd_ref, xr_ref, o_ref,
                xn_ref, h_ref, h_last, ug0, uv0, ug1, uv1, carry_g, carry_v, *, tm, nf, tiles_per_seq):
    i = pl.program_id(0)
    st = pl.program_id(1)
    u_bufs = ((ug0, uv0), (ug1, uv1))

    @pl.when(st == 0)
    def _():
        xn_ref[...] = _rms_rows(x_ref[...], g_ref[...]).astype(BF16)

        @pl.when(i == 0)
        def _():
            for ref in (ug0, uv0, ug1, uv1, carry_g, carry_v):
                ref[...] = jnp.zeros_like(ref)

    def conv_glu(bufs, dst):
        def branch(u_ref, cw_ref, cb_ref, r0):
            cw = cw_ref[...]
            c = cb_ref[...] + cw[0:1, :] * u_ref[pl.ds(r0 + CONV_HALO - 2, CONV_ROWS), :]
            c = c + cw[1:2, :] * u_ref[pl.ds(r0 + CONV_HALO - 1, CONV_ROWS), :]
            return c + cw[2:3, :] * u_ref[pl.ds(r0 + CONV_HALO, CONV_ROWS), :]

        for r0 in range(0, tm, CONV_ROWS):
            gate = branch(bufs[0], cwg_ref, cbg_ref, r0)
            val = branch(bufs[1], cwv_ref, cbv_ref, r0)
            dst[pl.ds(r0, CONV_ROWS), :] = (jax.nn.silu(gate) * val).astype(BF16)

    def up_proj(bufs):
        xn = xn_ref[...]
        seq_start = (i % tiles_per_seq) == 0
        for w_ref, u_ref, carry in ((wg_ref, bufs[0], carry_g), (wv_ref, bufs[1], carry_v)):
            u = _dot(xn, w_ref[...])
            u_ref[pl.ds(CONV_HALO, tm), :] = u
            u_ref[pl.ds(0, CONV_HALO), :] = jnp.where(seq_start, 0.0, carry[st])
            carry[st] = u[tm - CONV_HALO:, :]

    for parity in range(2):
        @pl.when((st < nf) & (st % 2 == parity))
        def _():
            conv_glu(u_bufs[1 - parity], h_ref.at[jnp.maximum(st - 1, 0)])
            up_proj(u_bufs[parity])

    def down_proj(order):
        acc = xr_ref[...]
        for f in order:
            acc = acc + _dot(h_last[...] if f == nf - 1 else h_ref[f], wd_ref[f])
        o_ref[...] = acc

    @pl.when(st == nf)
    def _():
        conv_glu(u_bufs[(nf - 1) % 2], h_last)
        mid = (nf - 1) // 2
        down_proj(list(range(mid)) + [nf - 1] + list(range(mid, nf - 1)))

    @pl.when(st > nf)
    def _():
        down_proj(range(nf))


def _ffn(x2d, g, w_up_bf, conv_w, conv_b, w_down_bf, seq, tm=512, tf=512, tn=512):
    t, d = x2d.shape
    d_ff = w_down_bf.shape[0]
    nf = d_ff // tf
    nn = d // tn
    kern = functools.partial(_ffn_kernel, tm=tm, nf=nf, tiles_per_seq=seq // tm)

    def up(st):
        return jnp.minimum(st, nf - 1)

    def prev(st):
        return jnp.clip(st - 1, 0, nf - 1)

    def down(st):
        return jnp.maximum(st - nf, 0)

    return pl.pallas_call(
        kern,
        grid=(t // tm, nf + nn),
        in_specs=[pl.BlockSpec((tm, d), lambda i, st: (i, 0)),
                  pl.BlockSpec((1, d), lambda i, st: (0, 0)),
                  pl.BlockSpec((d, tf), lambda i, st: (0, up(st))),
                  pl.BlockSpec((d, tf), lambda i, st: (0, up(st) + nf)),
                  pl.BlockSpec((CONV_WIDTH, tf), lambda i, st: (0, prev(st))),
                  pl.BlockSpec((CONV_WIDTH, tf), lambda i, st: (0, prev(st) + nf)),
                  pl.BlockSpec((1, tf), lambda i, st: (0, prev(st))),
                  pl.BlockSpec((1, tf), lambda i, st: (0, prev(st) + nf)),
                  pl.BlockSpec((nf, tf, tn), lambda i, st: (0, 0, down(st))),
                  pl.BlockSpec((tm, tn), lambda i, st: (i, down(st)))],
        out_specs=pl.BlockSpec((tm, tn), lambda i, st: (i, down(st))),
        out_shape=jax.ShapeDtypeStruct((t, d), F32),
        scratch_shapes=[pltpu.VMEM((tm, d), BF16),
                        pltpu.VMEM((nf - 1, tm, tf), BF16),
                        pltpu.VMEM((tm, tf), BF16)]
        + [pltpu.VMEM((tm + CONV_HALO, tf), F32)] * 4
        + [pltpu.VMEM((nf, CONV_HALO, tf), F32)] * 2,
        compiler_params=_params(("arbitrary", "arbitrary"), 56),
        name="conv_glu_ffn",
    )(x2d, g.reshape(1, d), w_up_bf, w_up_bf, conv_w, conv_w,
      conv_b.reshape(1, -1), conv_b.reshape(1, -1), w_down_bf.reshape(nf, tf, d), x2d)


def _rmsnorm_kernel(x_ref, g_ref, o_ref):
    o_ref[...] = _rms_rows(x_ref[...], g_ref[...])


def _rmsnorm(x2d, g, tm=512):
    t, d = x2d.shape
    return pl.pallas_call(
        _rmsnorm_kernel,
        grid=(t // tm,),
        in_specs=[pl.BlockSpec((tm, d), lambda i: (i, 0)), pl.BlockSpec((1, d), lambda i: (0, 0))],
        out_specs=pl.BlockSpec((tm, d), lambda i: (i, 0)),
        out_shape=jax.ShapeDtypeStruct((t, d), F32),
        compiler_params=_params(("parallel",), 32),
        name="final_rmsnorm",
    )(x2d, g.reshape(1, d))


def _rope_table_kernel(pos_ref, inv_ref, c_ref, sa_ref, sb_ref):
    ang = pos_ref[0].astype(F32) * inv_ref[...]
    lane = lax.broadcasted_iota(jnp.int32, ang.shape, 1)
    cos = jnp.cos(ang)
    sin = jnp.sin(ang)
    c_ref[0] = jnp.where(lane < ROPE_DIM, cos, 1.0)
    sa_ref[0] = jnp.where((lane >= ROPE_HALF) & (lane < ROPE_DIM), sin, 0.0)
    sb_ref[0] = jnp.where(lane < ROPE_HALF, -sin, 0.0)


def _rope_tables(positions):
    b, s = positions.shape
    inv = jnp.power(ROPE_THETA, -jnp.arange(0, ROPE_DIM, 2, dtype=F32) / ROPE_DIM)
    inv_lane = jnp.concatenate([inv, inv, jnp.zeros((LANES - ROPE_DIM,), F32)]).reshape(1, LANES)
    spec = pl.BlockSpec((1, s, LANES), lambda i: (i, 0, 0))
    shape = jax.ShapeDtypeStruct((b, s, LANES), F32)
    return pl.pallas_call(
        _rope_table_kernel,
        grid=(b,),
        in_specs=[pl.BlockSpec((1, s, 1), lambda i: (i, 0, 0)),
                  pl.BlockSpec((1, LANES), lambda i: (0, 0))],
        out_specs=[spec, spec, spec],
        out_shape=[shape, shape, shape],
        compiler_params=_params(("parallel",), 32),
        name="rope_tables",
    )(positions.reshape(b, s, 1), inv_lane)


def _rope(x, c, sa, sb):
    return x * c + pltpu.roll(x, ROPE_HALF, 1) * sa + pltpu.roll(x, LANES - ROPE_HALF, 1) * sb


def _cmp_mlp_kernel(x_ref, pe_ref, w1_ref, w2_ref, o_ref):
    n = o_ref.shape[-2]
    acc_lo = jnp.zeros((n, NSA_HEAD_DIM), F32)
    acc_hi = jnp.zeros((n, NSA_HEAD_DIM), F32)
    for l in range(CMP_STRIDE):
        xs = x_ref[0, pl.ds(l, n, stride=CMP_STRIDE), :]
        acc_lo += _dot((xs + pe_ref[0, l:l + 1, :]).astype(BF16), w1_ref[0, l])
        acc_hi += _dot((xs + pe_ref[0, CMP_STRIDE + l:CMP_STRIDE + l + 1, :]).astype(BF16),
                       w1_ref[0, CMP_STRIDE + l])
    hid = jax.nn.gelu(acc_lo + pltpu.roll(acc_hi, n - 1, 0))
    o_ref[0, 0, 0] = _dot(hid.astype(BF16), w2_ref[0])


def _cmp_mlp(proj, pe, w1_bf, w2_bf, col_block0):
    b, s, _ = proj.shape
    g, hd = NSA_KV_GROUPS, NSA_HEAD_DIM
    n = s // CMP_STRIDE
    return pl.pallas_call(
        _cmp_mlp_kernel,
        grid=(b, 2, g),
        in_specs=[pl.BlockSpec((1, s, hd), lambda bi, c, gi: (bi, 0, col_block0 + c * g + gi)),
                  pl.BlockSpec((1, CMP_BLOCK, hd), lambda bi, c, gi: (c, 0, 0)),
                  pl.BlockSpec((1, CMP_BLOCK, hd, hd), lambda bi, c, gi: (c, 0, 0, 0)),
                  pl.BlockSpec((1, hd, hd), lambda bi, c, gi: (c, 0, 0))],
        out_specs=pl.BlockSpec((1, 1, 1, n, hd), lambda bi, c, gi: (bi, c, gi, 0, 0)),
        out_shape=jax.ShapeDtypeStruct((b, 2, g, n, hd), F32),
        compiler_params=_params(("parallel", "parallel", "parallel"), 32),
        name="nsa_cmp_mlp",
    )(proj, pe, w1_bf, w2_bf)


def _cmp_attn_kernel(q_ref, kc_ref, vc_ref, gl_ref, gb_ref, ov_ref, o_ref, sel_ref, *, ts, n_slc):
    g = pl.program_id(1)
    si = pl.program_id(2)
    hd = NSA_HEAD_DIM
    scale = hd ** -0.5
    lane = lax.broadcasted_iota(jnp.int32, (ts, LANES), 1)
    t = si * ts + lax.broadcasted_iota(jnp.int32, (ts, LANES), 0)
    cmask = (lane * CMP_STRIDE + (CMP_BLOCK - 1)) <= t
    cmask_f = cmask.astype(F32)
    kc = kc_ref[0, 0, 0].astype(BF16)
    vc = vc_ref[0, 0, 0].astype(BF16)
    gates = jax.nn.sigmoid(gl_ref[0] + gb_ref[...])
    psum = jnp.zeros((ts, LANES), F32)
    for r in range(NSA_Q_PER_GROUP):
        q = q_ref[0, :, r * hd:(r + 1) * hd].astype(BF16)
        s = jnp.where(cmask, _dot_nt(q, kc) * scale, NEG_INF)
        e = jnp.exp(s - jnp.max(s, axis=-1, keepdims=True))
        p = e / jnp.sum(e, axis=-1, keepdims=True) * cmask_f
        psum = psum + p
        o = _dot(p.astype(BF16), vc)
        o_ref[0, :, r * hd:(r + 1) * hd] = o * _lane_col(gates, g * NSA_Q_PER_GROUP + r)

    ov = ov_ref[...]
    hi, mid, lo = _split3(psum)
    imp = _dot(hi, ov) + _dot(mid, ov) + _dot(lo, ov)
    q_blk = t // SLC_BLOCK
    forced = (lane == 0) | (lane == q_blk)
    score = jnp.where(forced, jnp.inf, jnp.where(lane <= q_blk, imp, -jnp.inf))
    live = lane < n_slc
    lane_f = lane.astype(F32)
    taken = jnp.zeros((ts, LANES), jnp.int32)
    for _ in range(min(N_SELECT, n_slc)):
        cand_ok = live & (taken == 0)
        best = jnp.max(jnp.where(cand_ok, score, -jnp.inf), axis=-1, keepdims=True)
        first = jnp.min(jnp.where(cand_ok & (score == best), lane_f, float(LANES)), axis=-1, keepdims=True)
        taken = jnp.where(lane_f == first, 1, taken)
    sel_ref[0, 0] = taken.astype(F32)


def _cmp_attn(proj, kv_cmp, gate_b_lane, overlap_bf, gate_col_block, ts=256):
    b, s, _ = proj.shape
    g, hd, r = NSA_KV_GROUPS, NSA_HEAD_DIM, NSA_Q_PER_GROUP
    n = kv_cmp.shape[-2]
    kern = functools.partial(_cmp_attn_kernel, ts=ts, n_slc=s // SLC_BLOCK)
    return pl.pallas_call(
        kern,
        grid=(b, g, s // ts),
        in_specs=[pl.BlockSpec((1, ts, r * hd), lambda bi, gi, si: (bi, si, gi)),
                  pl.BlockSpec((1, 1, 1, n, hd), lambda bi, gi, si: (bi, 0, gi, 0, 0)),
                  pl.BlockSpec((1, 1, 1, n, hd), lambda bi, gi, si: (bi, 1, gi, 0, 0)),
                  pl.BlockSpec((1, ts, LANES), lambda bi, gi, si: (bi, si, gate_col_block)),
                  pl.BlockSpec((1, LANES), lambda bi, gi, si: (0, 0)),
                  pl.BlockSpec((n, LANES), lambda bi, gi, si: (0, 0))],
        out_specs=[pl.BlockSpec((1, ts, r * hd), lambda bi, gi, si: (bi, si, gi)),
                   pl.BlockSpec((1, 1, ts, LANES), lambda bi, gi, si: (bi, gi, si, 0))],
        out_shape=[jax.ShapeDtypeStruct((b, s, g * r * hd), F32),
                   jax.ShapeDtypeStruct((b, g, s, LANES), F32)],
        compiler_params=_params(("parallel", "parallel", "parallel"), 32),
        name="nsa_cmp_attn_topk",
    )(proj, kv_cmp, kv_cmp, proj, gate_b_lane, overlap_bf)


def _sel_win_kernel(q_ref, ks_ref, vs_ref, kw_ref, vw_ref, sm_ref, ex_ref, gl_ref, gb_ref,
                    c_ref, sa_ref, sb_ref, o_ref, ks_rot, vs_ext, kw_rot, vw_ext, q_st, *, tq, tk):
    g = pl.program_id(1)
    qi = pl.program_id(2)
    hd, nr = NSA_HEAD_DIM, NSA_Q_PER_GROUP
    s_len = ks_rot.shape[0]
    scale = hd ** -0.5

    @pl.when(qi == 0)
    def _():
        c, sa, sb = c_ref[0], sa_ref[0], sb_ref[0]
        ks_rot[...] = _rope(ks_ref[0], c, sa, sb).astype(BF16)
        kw_rot[...] = _rope(kw_ref[0], c, sa, sb).astype(BF16)
        ones = jnp.ones((s_len, hd), BF16)
        vs_ext[...] = jnp.concatenate([vs_ref[0].astype(BF16), ones], axis=1)
        vw_ext[...] = jnp.concatenate([vw_ref[0].astype(BF16), ones], axis=1)

    q0 = pl.multiple_of(qi * tq, tq)
    ropes = (c_ref[0, pl.ds(q0, tq), :], sa_ref[0, pl.ds(q0, tq), :], sb_ref[0, pl.ds(q0, tq), :])
    for r in range(nr):
        q = _rope(q_ref[0, :, r * hd:(r + 1) * hd], *ropes) * scale
        q_st[r * tq:(r + 1) * tq, :] = q.astype(BF16)
    gates = jax.nn.sigmoid(gl_ref[0] + gb_ref[...])

    def attend(k, v_ext, bias):
        n = k.shape[0]
        s = _dot_nt(q_st[...], k).reshape(nr, tq, n) + bias[None]
        p = jnp.exp(s - jnp.max(s, axis=-1, keepdims=True))
        o_ext = _dot(p.reshape(nr * tq, n).astype(BF16), v_ext)
        return o_ext[:, :hd] / o_ext[:, hd:hd + 1]

    wk = WINDOW + tq
    start = pl.multiple_of(jnp.maximum(q0 - WINDOW, 0), tq)
    dist = (q0 - start) + lax.broadcasted_iota(jnp.int32, (tq, wk), 0) - lax.broadcasted_iota(jnp.int32, (tq, wk), 1)
    sm = sm_ref[0, 0].astype(BF16)
    n_chunks = (q0 + tq + tk - 1) // tk
    for n in range(1, s_len // tk + 1):
        @pl.when(n_chunks == n)
        def _():
            nk = n * tk
            bias_w = jnp.where((dist >= 0) & (dist < WINDOW), 0.0, NEG_INF)
            o_win = attend(kw_rot[pl.ds(start, wk), :], vw_ext[pl.ds(start, wk), :], bias_w)
            member = _dot(sm, ex_ref[:, :nk])
            t = q0 + lax.broadcasted_iota(jnp.int32, (tq, nk), 0)
            kpos = lax.broadcasted_iota(jnp.int32, (tq, nk), 1)
            bias_s = jnp.where((member > 0.5) & (kpos <= t), 0.0, NEG_INF)
            o_sel = attend(ks_rot[:nk, :], vs_ext[:nk, :], bias_s)
            for r in range(nr):
                rows = slice(r * tq, (r + 1) * tq)
                o_ref[0, :, r * hd:(r + 1) * hd] = (
                    o_sel[rows, :] * _lane_col(gates, NSA_HEADS + g * nr + r)
                    + o_win[rows, :] * _lane_col(gates, 2 * NSA_HEADS + g * nr + r))


def _sel_win_attn(proj, sel_mask, expand_bf, gate_b_lane, tables, sel_col_block0, win_col_block0,
                  gate_col_block, tq=128, tk=512):
    b, s, _ = proj.shape
    g, hd, r = NSA_KV_GROUPS, NSA_HEAD_DIM, NSA_Q_PER_GROUP
    q_spec = pl.BlockSpec((1, tq, r * hd), lambda bi, gi, qi: (bi, qi, gi))

    def kv_spec(col_block):
        return pl.BlockSpec((1, s, hd), lambda bi, gi, qi: (bi, 0, col_block + gi))

    tab_spec = pl.BlockSpec((1, s, LANES), lambda bi, gi, qi: (bi, 0, 0))
    return pl.pallas_call(
        functools.partial(_sel_win_kernel, tq=tq, tk=tk),
        grid=(b, g, s // tq),
        in_specs=[q_spec, kv_spec(sel_col_block0), kv_spec(sel_col_block0 + g),
                  kv_spec(win_col_block0), kv_spec(win_col_block0 + g),
                  pl.BlockSpec((1, 1, tq, LANES), lambda bi, gi, qi: (bi, gi, qi, 0)),
                  pl.BlockSpec((LANES, s), lambda bi, gi, qi: (0, 0)),
                  pl.BlockSpec((1, tq, LANES), lambda bi, gi, qi: (bi, qi, gate_col_block)),
                  pl.BlockSpec((1, LANES), lambda bi, gi, qi: (0, 0)),
                  tab_spec, tab_spec, tab_spec],
        out_specs=q_spec,
        out_shape=jax.ShapeDtypeStruct((b, s, g * r * hd), F32),
        scratch_shapes=[pltpu.VMEM((s, hd), BF16), pltpu.VMEM((s, 2 * hd), BF16)] * 2
        + [pltpu.VMEM((r * tq, hd), BF16)],
        compiler_params=_params(("parallel", "parallel", "arbitrary"), 56),
        name="nsa_sel_win_attn",
    )(proj, proj, proj, proj, proj, sel_mask, expand_bf, proj, gate_b_lane, *tables)


def _pad_cols(w, mult):
    n = w.shape[-1]
    return jnp.pad(w, ((0, 0), (0, (-n) % mult)))


def _lane_row(v):
    v = v.reshape(-1)
    return jnp.pad(v, (0, LANES - v.shape[0])).reshape(1, LANES)


def _nsa_mixer(x2d, b, s, norm_g, tables, w_in, gate_b, cmp_pe, cmp_w1, cmp_w2, w_out):
    g, hd, h = NSA_KV_GROUPS, NSA_HEAD_DIM, NSA_HEADS
    proj = _norm_matmul(x2d, norm_g, _pad_cols(w_in, 512).astype(BF16))
    proj = proj.reshape(b, s, -1)
    q_blocks = h
    cmp0, sel0, win0 = q_blocks, q_blocks + 2 * g, q_blocks + 4 * g
    gate_blk = q_blocks + 6 * g
    gate_b_lane = _lane_row(gate_b)
    kv_cmp = _cmp_mlp(proj, cmp_pe, cmp_w1.astype(BF16), cmp_w2.astype(BF16), cmp0)

    n_cmp_pad = s // CMP_STRIDE
    n_slc = s // SLC_BLOCK
    jj = np.arange(n_cmp_pad)[:, None]
    ss = np.arange(LANES)[None, :]
    lo = np.maximum(jj * CMP_STRIDE, ss * SLC_BLOCK)
    hi = np.minimum(jj * CMP_STRIDE + CMP_BLOCK, ss * SLC_BLOCK + SLC_BLOCK)
    overlap = np.clip(hi - lo, 0, None).astype(np.float32) / CMP_BLOCK
    overlap[(s - CMP_BLOCK) // CMP_STRIDE + 1:, :] = 0.0
    overlap[:, n_slc:] = 0.0
    expand = (np.arange(s)[None, :] // SLC_BLOCK == np.arange(LANES)[:, None]).astype(np.float32)

    o_cmp, sel_mask = _cmp_attn(proj, kv_cmp, gate_b_lane, jnp.asarray(overlap, BF16), gate_blk)
    o_rot = _sel_win_attn(proj, sel_mask, jnp.asarray(expand, BF16), gate_b_lane, tables, sel0, win0, gate_blk)
    t = b * s
    return _out_proj([o_cmp.reshape(t, -1), o_rot.reshape(t, -1)], w_out.astype(BF16), x2d)


def _mlstm_gate_kernel(gl_ref, gb_ref, tri_ref, li_ref, f_ref):
    pre = gl_ref[0] + gb_ref[...]
    capped = GATE_SOFTCAP * jnp.tanh(pre / GATE_SOFTCAP)
    li_ref[0] = capped
    lf = jax.nn.log_sigmoid(capped)
    rows = tri_ref.shape[0]
    tri = tri_ref[...]
    for sl in range(lf.shape[0] // rows):
        hi, mid, lo = _split3(lf[sl * rows:(sl + 1) * rows, :])
        f_ref[0, pl.ds(sl * rows, rows), :] = _dot(tri, hi) + _dot(tri, mid) + _dot(tri, lo)


def _mlstm_gates(proj, gate_b_lane, gate_col_block, slab=256):
    b, s, _ = proj.shape
    idx = np.arange(slab)
    tri = ((idx[:, None] >= idx[None, :]) &
           (idx[:, None] // MLSTM_CHUNK == idx[None, :] // MLSTM_CHUNK)).astype(np.float32)
    spec = pl.BlockSpec((1, s, LANES), lambda bi: (bi, 0, 0))
    shape = jax.ShapeDtypeStruct((b, s, LANES), F32)
    return pl.pallas_call(
        _mlstm_gate_kernel,
        grid=(b,),
        in_specs=[pl.BlockSpec((1, s, LANES), lambda bi: (bi, 0, gate_col_block)),
                  pl.BlockSpec((1, LANES), lambda bi: (0, 0)),
                  pl.BlockSpec((slab, slab), lambda bi: (0, 0))],
        out_specs=[spec, spec],
        out_shape=[shape, shape],
        compiler_params=_params(("parallel",), 32),
        name="mlstm_gates",
    )(proj, gate_b_lane, jnp.asarray(tri, BF16))


def _mlstm_kernel(q_ref, k_ref, v_ref, og_ref, li_ref, f_ref, hn_ref, o_ref, state_ref):
    h = pl.program_id(1)
    L, dk, dv = MLSTM_CHUNK, MLSTM_QK_DIM, MLSTM_V_DIM
    s = q_ref.shape[1]
    state_ref[...] = jnp.zeros_like(state_ref)
    lane = lax.broadcasted_iota(jnp.int32, (L, LANES), 1)
    causal = lax.broadcasted_iota(jnp.int32, (L, L), 1) <= lax.broadcasted_iota(jnp.int32, (L, L), 0)
    ones_col = (lane < 3).astype(F32)
    ones_row = ((lane >= 3) & (lane < 6)).astype(F32)

    def body(c, m):
        r0 = pl.multiple_of(c * L, L)
        rows = pl.ds(r0, L)
        qb = (q_ref[0, rows, :] * dk ** -0.5).astype(BF16)
        k = k_ref[0, rows, :]
        vb = v_ref[0, rows, :].astype(BF16)
        li = _lane_col(li_ref[0, rows, :], h)
        fc = _lane_col(f_ref[0, rows, :], MLSTM_HEADS + h)
        f_end = fc[L - 1:L, :]
        a = f_end - fc + li
        a_max = jnp.max(a, axis=0, keepdims=True)
        w = jnp.exp(a - a_max)

        fh, fm, fl = _split3(fc)
        gh, gm, gl = _split3(li - fc)
        lhs = jnp.where(lane == 0, fh.astype(F32), jnp.where(lane == 1, fm.astype(F32),
              jnp.where(lane == 2, fl.astype(F32), ones_row)))
        rhs = jnp.where(lane == 3, gh.astype(F32), jnp.where(lane == 4, gm.astype(F32),
              jnp.where(lane == 5, gl.astype(F32), ones_col)))
        d = jnp.where(causal, _dot_nt(lhs.astype(BF16), rhs.astype(BF16)), -jnp.inf)

        inter = fc + m
        m_t = jnp.maximum(inter, jnp.max(d, axis=-1, keepdims=True))
        g_inter = jnp.exp(inter - m_t)
        sqk = _dot_nt(qb, k.astype(BF16)) * jnp.exp(d - m_t)
        state = state_ref[...]
        from_state = _dot(qb, state.astype(BF16))
        num = _dot(sqk.astype(BF16), vb) + g_inter * from_state[:, :dv]
        den = jnp.sum(sqk, axis=-1, keepdims=True) + g_inter * from_state[:, dv:dv + 1]
        hv = num / jnp.maximum(jnp.abs(den), jnp.exp(-m_t))
        hv = hv * lax.rsqrt(jnp.mean(hv * hv, axis=-1, keepdims=True) + NORM_EPS) * hn_ref[0]
        o_ref[0, rows, :] = hv * jax.nn.sigmoid(og_ref[0, rows, :])

        v_ext = jnp.concatenate([vb, jnp.ones((L, LANES), BF16)], axis=1)
        d_state = _dot_tn((w * k).astype(BF16), v_ext)
        m_new = jnp.maximum(f_end + m, a_max)
        state_ref[...] = jnp.exp(f_end + m - m_new) * state + jnp.exp(a_max - m_new) * d_state
        return m_new

    lax.fori_loop(0, s // L, body, jnp.zeros((1, 1), F32), unroll=4)


def _mlstm_core(proj, li, fcum, head_norm):
    b, s, _ = proj.shape
    hh, dk, dv = MLSTM_HEADS, MLSTM_QK_DIM, MLSTM_V_DIM
    v_blk0 = 2 * hh * dk // dv
    og_blk0 = v_blk0 + hh
    gate_spec = pl.BlockSpec((1, s, LANES), lambda bi, hi: (bi, 0, 0))
    return pl.pallas_call(
        _mlstm_kernel,
        grid=(b, hh),
        in_specs=[pl.BlockSpec((1, s, dk), lambda bi, hi: (bi, 0, hi)),
                  pl.BlockSpec((1, s, dk), lambda bi, hi: (bi, 0, hh + hi)),
                  pl.BlockSpec((1, s, dv), lambda bi, hi: (bi, 0, v_blk0 + hi)),
                  pl.BlockSpec((1, s, dv), lambda bi, hi: (bi, 0, og_blk0 + hi)),
                  gate_spec, gate_spec,
                  pl.BlockSpec((1, 1, dv), lambda bi, hi: (hi, 0, 0))],
        out_specs=pl.BlockSpec((1, s, dv), lambda bi, hi: (bi, 0, hi)),
        out_shape=jax.ShapeDtypeStruct((b, s, hh * dv), F32),
        scratch_shapes=[pltpu.VMEM((dk, dv + LANES), F32)],
        compiler_params=_params(("parallel", "parallel"), 40),
        name="mlstm_chunkwise",
    )(proj, proj, proj, proj, li, fcum, head_norm.reshape(hh, 1, dv))


def _mlstm_mixer(x2d, b, s, norm_g, w_in, gate_b, head_norm, w_out):
    hh, dk, dv = MLSTM_HEADS, MLSTM_QK_DIM, MLSTM_V_DIM
    proj = _norm_matmul(x2d, norm_g, _pad_cols(w_in, 512).astype(BF16))
    proj = proj.reshape(b, s, -1)
    gate_blk = (2 * hh * dk + 2 * hh * dv) // LANES
    li, fcum = _mlstm_gates(proj, _lane_row(gate_b), gate_blk)
    hcat = _mlstm_core(proj, li, fcum, head_norm)
    return _out_proj([hcat.reshape(b * s, -1)], w_out.astype(BF16), x2d)


def kernel(x, positions, nsa_w_in, nsa_gate_b, nsa_cmp_pe, nsa_cmp_w1, nsa_cmp_w2, nsa_w_out,
           mlstm_w_in, mlstm_gate_b, mlstm_head_norm, mlstm_w_out,
           norm_mix, norm_ffn, ffn_w_up, ffn_conv_w, ffn_conv_b, ffn_w_down, norm_final):
    b, s, d = x.shape
    depth = norm_mix.shape[0]
    tables = _rope_tables(positions)
    x2d = x.reshape(b * s, d)
    for i in range(depth):
        j = i // 2
        if i % 2 == 0:
            x2d = _nsa_mixer(x2d, b, s, norm_mix[i], tables, nsa_w_in[j], nsa_gate_b[j], nsa_cmp_pe[j],
                             nsa_cmp_w1[j], nsa_cmp_w2[j], nsa_w_out[j])
        else:
            x2d = _mlstm_mixer(x2d, b, s, norm_mix[i], mlstm_w_in[j], mlstm_gate_b[j],
                               mlstm_head_norm[j], mlstm_w_out[j])
        x2d = _ffn(x2d, norm_ffn[i], ffn_w_up[i].astype(BF16), ffn_conv_w[i], ffn_conv_b[i],
                   ffn_w_down[i].astype(BF16), s)
    return _rmsnorm(x2d, norm_final).reshape(b, s, d)
```

```python
import functools

import numpy as np
import jax
import jax.numpy as jnp
from jax import lax
from jax.experimental import pallas as pl
from jax.experimental.pallas import tpu as pltpu

F32 = jnp.float32
BF16 = jnp.bfloat16

LANES = 128
NORM_EPS = 1e-6
NEG_INF = -1e30

NSA_HEADS = 16
NSA_KV_GROUPS = 2
NSA_HEAD_DIM = 128
NSA_Q_PER_GROUP = NSA_HEADS // NSA_KV_GROUPS
CMP_BLOCK = 32
CMP_STRIDE = 16
SLC_BLOCK = 64
N_SELECT = 8
WINDOW = 512
ROPE_THETA = 500000.0
ROPE_DIM = NSA_HEAD_DIM // 4
ROPE_HALF = ROPE_DIM // 2

MLSTM_HEADS = 8
MLSTM_QK_DIM = 128
MLSTM_V_DIM = 256
MLSTM_CHUNK = 64
GATE_SOFTCAP = 15.0

CONV_WIDTH = 3
CONV_HALO = 8
CONV_ROWS = 16

MIB = 1024 * 1024


def _params(semantics, vmem_mib):
    return pltpu.CompilerParams(dimension_semantics=semantics, vmem_limit_bytes=vmem_mib * MIB)


def _dot(a, b):
    return jnp.dot(a, b, preferred_element_type=F32)


def _dot_nt(a, b):
    return lax.dot_general(a, b, (((1,), (1,)), ((), ())), preferred_element_type=F32)


def _dot_tn(a, b):
    return lax.dot_general(a, b, (((0,), (0,)), ((), ())), preferred_element_type=F32)


def _split3(x):
    hi = x.astype(BF16)
    r1 = x - hi.astype(F32)
    mid = r1.astype(BF16)
    lo = (r1 - mid.astype(F32)).astype(BF16)
    return hi, mid, lo


def _rms_rows(x, g):
    ms = jnp.mean(x * x, axis=-1, keepdims=True)
    return x * lax.rsqrt(ms + NORM_EPS) * g


def _lane_col(x, idx):
    lane = lax.broadcasted_iota(jnp.int32, x.shape, 1)
    return jnp.sum(jnp.where(lane == idx, x, 0.0), axis=-1, keepdims=True)


def _norm_matmul_kernel(x_ref, g_ref, w_ref, o_ref, xn_ref):
    @pl.when(pl.program_id(1) == 0)
    def _():
        xn_ref[...] = _rms_rows(x_ref[...], g_ref[...]).astype(BF16)

    o_ref[...] = _dot(xn_ref[...], w_ref[...])


def _norm_matmul(x2d, g, w_bf, tm=1024, tn=512):
    t, d = x2d.shape
    n = w_bf.shape[1]
    return pl.pallas_call(
        _norm_matmul_kernel,
        grid=(t // tm, n // tn),
        in_specs=[pl.BlockSpec((tm, d), lambda i, j: (i, 0)),
                  pl.BlockSpec((1, d), lambda i, j: (0, 0)),
                  pl.BlockSpec((d, tn), lambda i, j: (0, j))],
        out_specs=pl.BlockSpec((tm, tn), lambda i, j: (i, j)),
        out_shape=jax.ShapeDtypeStruct((t, n), F32),
        scratch_shapes=[pltpu.VMEM((tm, d), BF16)],
        compiler_params=_params(("parallel", "arbitrary"), 40),
        name="norm_matmul",
    )(x2d, g.reshape(1, d), w_bf)


def _out_proj_kernel(*refs, n_in):
    a_refs = refs[:n_in]
    w_ref, r_ref, o_ref, a_bf = refs[n_in:]

    @pl.when(pl.program_id(1) == 0)
    def _():
        a = a_refs[0][...]
        for ar in a_refs[1:]:
            a = a + ar[...]
        a_bf[...] = a.astype(BF16)

    o_ref[...] = r_ref[...] + _dot(a_bf[...], w_ref[...])


def _out_proj(a_list, w_bf, resid, tm=1024, tn=512):
    t, k = a_list[0].shape
    n = w_bf.shape[1]
    n_in = len(a_list)
    return pl.pallas_call(
        functools.partial(_out_proj_kernel, n_in=n_in),
        grid=(t // tm, n // tn),
        in_specs=[pl.BlockSpec((tm, k), lambda i, j: (i, 0))] * n_in
        + [pl.BlockSpec((k, tn), lambda i, j: (0, j)),
           pl.BlockSpec((tm, tn), lambda i, j: (i, j))],
        out_specs=pl.BlockSpec((tm, tn), lambda i, j: (i, j)),
        out_shape=jax.ShapeDtypeStruct((t, n), F32),
        scratch_shapes=[pltpu.VMEM((tm, k), BF16)],
        compiler_params=_params(("parallel", "arbitrary"), 56),
        name="out_proj",
    )(*a_list, w_bf, resid)


def _ffn_kernel(x_ref, g_ref, wg_ref, wv_ref, cwg_ref, cwv_ref, cbg_ref, cbv_ref, wd_ref, xr_ref, o_ref,
                xn_ref, h_ref, h_last, ug0, uv0, ug1, uv1, carry_g, carry_v, *, tm, nf, tiles_per_seq):
    i = pl.program_id(0)
    st = pl.program_id(1)
    u_bufs = ((ug0, uv0), (ug1, uv1))

    @pl.when(st == 0)
    def _():
        xn_ref[...] = _rms_rows(x_ref[...], g_ref[...]).astype(BF16)

        @pl.when(i == 0)
        def _():
            for ref in (ug0, uv0, ug1, uv1, carry_g, carry_v):
                ref[...] = jnp.zeros_like(ref)

    def conv_glu(bufs, dst):
        def branch(u_ref, cw_ref, cb_ref, r0):
            cw = cw_ref[...]
            c = cb_ref[...] + cw[0:1, :] * u_ref[pl.ds(r0 + CONV_HALO - 2, CONV_ROWS), :]
            c = c + cw[1:2, :] * u_ref[pl.ds(r0 + CONV_HALO - 1, CONV_ROWS), :]
            return c + cw[2:3, :] * u_ref[pl.ds(r0 + CONV_HALO, CONV_ROWS), :]

        for r0 in range(0, tm, CONV_ROWS):
            gate = branch(bufs[0], cwg_ref, cbg_ref, r0)
            val = branch(bufs[1], cwv_ref, cbv_ref, r0)
            dst[pl.ds(r0, CONV_ROWS), :] = (jax.nn.silu(gate) * val).astype(BF16)

    def up_proj(bufs):
        xn = xn_ref[...]
        seq_start = (i % tiles_per_seq) == 0
        for w_ref, u_ref, carry in ((wg_ref, bufs[0], carry_g), (wv_ref, bufs[1], carry_v)):
            u = _dot(xn, w_ref[...])
            u_ref[pl.ds(CONV_HALO, tm), :] = u
            u_ref[pl.ds(0, CONV_HALO), :] = jnp.where(seq_start, 0.0, carry[st])
            carry[st] = u[tm - CONV_HALO:, :]

    for parity in range(2):
        @pl.when((st < nf) & (st % 2 == parity))
        def _():
            conv_glu(u_bufs[1 - parity], h_ref.at[jnp.maximum(st - 1, 0)])
            up_proj(u_bufs[parity])

    def down_proj(order):
        acc = xr_ref[...]
        for f in order:
            acc = acc + _dot(h_last[...] if f == nf - 1 else h_ref[f], wd_ref[f])
        o_ref[...] = acc

    @pl.when(st == nf)
    def _():
        conv_glu(u_bufs[(nf - 1) % 2], h_last)
        mid = (nf - 1) // 2
        down_proj(list(range(mid)) + [nf - 1] + list(range(mid, nf - 1)))

    @pl.when(st > nf)
    def _():
        down_proj(range(nf))


def _ffn(x2d, g, w_up_bf, conv_w, conv_b, w_down_bf, seq, tm=1024, tf=512, tn=256):
    t, d = x2d.shape
    d_ff = w_down_bf.shape[0]
    nf = d_ff // tf
    nn = d // tn
    kern = functools.partial(_ffn_kernel, tm=tm, nf=nf, tiles_per_seq=seq // tm)

    def up(st):
        return jnp.minimum(st, nf - 1)

    def prev(st):
        return jnp.clip(st - 1, 0, nf - 1)

    def down(st):
        return jnp.maximum(st - nf, 0)

    return pl.pallas_call(
        kern,
        grid=(t // tm, nf + nn),
        in_specs=[pl.BlockSpec((tm, d), lambda i, st: (i, 0), pipeline_mode=pl.Buffered(1)),
                  pl.BlockSpec((1, d), lambda i, st: (0, 0)),
                  pl.BlockSpec((d, tf), lambda i, st: (0, up(st))),
                  pl.BlockSpec((d, tf), lambda i, st: (0, up(st) + nf)),
                  pl.BlockSpec((CONV_WIDTH, tf), lambda i, st: (0, prev(st))),
                  pl.BlockSpec((CONV_WIDTH, tf), lambda i, st: (0, prev(st) + nf)),
                  pl.BlockSpec((1, tf), lambda i, st: (0, prev(st))),
                  pl.BlockSpec((1, tf), lambda i, st: (0, prev(st) + nf)),
                  pl.BlockSpec((nf, tf, tn), lambda i, st: (0, 0, down(st))),
                  pl.BlockSpec((tm, tn), lambda i, st: (i, down(st)))],
        out_specs=pl.BlockSpec((tm, tn), lambda i, st: (i, down(st))),
        out_shape=jax.ShapeDtypeStruct((t, d), F32),
        scratch_shapes=[pltpu.VMEM((tm, d), BF16),
                        pltpu.VMEM((nf - 1, tm, tf), BF16),
                        pltpu.VMEM((tm, tf), BF16)]
        + [pltpu.VMEM((tm + CONV_HALO, tf), F32)] * 4
        + [pltpu.VMEM((nf, CONV_HALO, tf), F32)] * 2,
        compiler_params=_params(("arbitrary", "arbitrary"), 56),
        name="conv_glu_ffn",
    )(x2d, g.reshape(1, d), w_up_bf, w_up_bf, conv_w, conv_w,
      conv_b.reshape(1, -1), conv_b.reshape(1, -1), w_down_bf.reshape(nf, tf, d), x2d)


def _rmsnorm_kernel(x_ref, g_ref, o_ref):
    o_ref[...] = _rms_rows(x_ref[...], g_ref[...])


def _rmsnorm(x2d, g, tm=512):
    t, d = x2d.shape
    return pl.pallas_call(
        _rmsnorm_kernel,
        grid=(t // tm,),
        in_specs=[pl.BlockSpec((tm, d), lambda i: (i, 0)), pl.BlockSpec((1, d), lambda i: (0, 0))],
        out_specs=pl.BlockSpec((tm, d), lambda i: (i, 0)),
        out_shape=jax.ShapeDtypeStruct((t, d), F32),
        compiler_params=_params(("parallel",), 32),
        name="final_rmsnorm",
    )(x2d, g.reshape(1, d))


def _rope_table_kernel(pos_ref, inv_ref, c_ref, sa_ref, sb_ref):
    ang = pos_ref[0].astype(F32) * inv_ref[...]
    lane = lax.broadcasted_iota(jnp.int32, ang.shape, 1)
    cos = jnp.cos(ang)
    sin = jnp.sin(ang)
    c_ref[0] = jnp.where(lane < ROPE_DIM, cos, 1.0)
    sa_ref[0] = jnp.where((lane >= ROPE_HALF) & (lane < ROPE_DIM), sin, 0.0)
    sb_ref[0] = jnp.where(lane < ROPE_HALF, -sin, 0.0)


def _rope_tables(positions):
    b, s = positions.shape
    inv = jnp.power(ROPE_THETA, -jnp.arange(0, ROPE_DIM, 2, dtype=F32) / ROPE_DIM)
    inv_lane = jnp.concatenate([inv, inv, jnp.zeros((LANES - ROPE_DIM,), F32)]).reshape(1, LANES)
    spec = pl.BlockSpec((1, s, LANES), lambda i: (i, 0, 0))
    shape = jax.ShapeDtypeStruct((b, s, LANES), F32)
    return pl.pallas_call(
        _rope_table_kernel,
        grid=(b,),
        in_specs=[pl.BlockSpec((1, s, 1), lambda i: (i, 0, 0)),
                  pl.BlockSpec((1, LANES), lambda i: (0, 0))],
        out_specs=[spec, spec, spec],
        out_shape=[shape, shape, shape],
        compiler_params=_params(("parallel",), 32),
        name="rope_tables",
    )(positions.reshape(b, s, 1), inv_lane)


def _rope(x, c, sa, sb):
    return x * c + pltpu.roll(x, ROPE_HALF, 1) * sa + pltpu.roll(x, LANES - ROPE_HALF, 1) * sb


def _cmp_mlp_kernel(x_ref, pe_ref, w1_ref, w2_ref, o_ref):
    n = o_ref.shape[-2]
    acc_lo = jnp.zeros((n, NSA_HEAD_DIM), F32)
    acc_hi = jnp.zeros((n, NSA_HEAD_DIM), F32)
    for l in range(CMP_STRIDE):
        xs = x_ref[0, pl.ds(l, n, stride=CMP_STRIDE), :]
        acc_lo += _dot((xs + pe_ref[0, l:l + 1, :]).astype(BF16), w1_ref[0, l])
        acc_hi += _dot((xs + pe_ref[0, CMP_STRIDE + l:CMP_STRIDE + l + 1, :]).astype(BF16),
                       w1_ref[0, CMP_STRIDE + l])
    hid = jax.nn.gelu(acc_lo + pltpu.roll(acc_hi, n - 1, 0))
    o_ref[0, 0, 0] = _dot(hid.astype(BF16), w2_ref[0])


def _cmp_mlp(proj, pe, w1_bf, w2_bf, col_block0):
    b, s, _ = proj.shape
    g, hd = NSA_KV_GROUPS, NSA_HEAD_DIM
    n = s // CMP_STRIDE
    return pl.pallas_call(
        _cmp_mlp_kernel,
        grid=(b, 2, g),
        in_specs=[pl.BlockSpec((1, s, hd), lambda bi, c, gi: (bi, 0, col_block0 + c * g + gi)),
                  pl.BlockSpec((1, CMP_BLOCK, hd), lambda bi, c, gi: (c, 0, 0)),
                  pl.BlockSpec((1, CMP_BLOCK, hd, hd), lambda bi, c, gi: (c, 0, 0, 0)),
                  pl.BlockSpec((1, hd, hd), lambda bi, c, gi: (c, 0, 0))],
        out_specs=pl.BlockSpec((1, 1, 1, n, hd), lambda bi, c, gi: (bi, c, gi, 0, 0)),
        out_shape=jax.ShapeDtypeStruct((b, 2, g, n, hd), F32),
        compiler_params=_params(("parallel", "parallel", "parallel"), 32),
        name="nsa_cmp_mlp",
    )(proj, pe, w1_bf, w2_bf)


def _cmp_attn_kernel(q_ref, kc_ref, vc_ref, gl_ref, gb_ref, ov_ref, o_ref, sel_ref, *, ts, n_slc):
    g = pl.program_id(1)
    si = pl.program_id(2)
    hd = NSA_HEAD_DIM
    scale = hd ** -0.5
    lane = lax.broadcasted_iota(jnp.int32, (ts, LANES), 1)
    t = si * ts + lax.broadcasted_iota(jnp.int32, (ts, LANES), 0)
    cmask = (lane * CMP_STRIDE + (CMP_BLOCK - 1)) <= t
    cmask_f = cmask.astype(F32)
    kc = kc_ref[0, 0, 0].astype(BF16)
    vc = vc_ref[0, 0, 0].astype(BF16)
    gates = jax.nn.sigmoid(gl_ref[0] + gb_ref[...])
    psum = jnp.zeros((ts, LANES), F32)
    for r in range(NSA_Q_PER_GROUP):
        q = q_ref[0, :, r * hd:(r + 1) * hd].astype(BF16)
        s = jnp.where(cmask, _dot_nt(q, kc) * scale, NEG_INF)
        e = jnp.exp(s - jnp.max(s, axis=-1, keepdims=True))
        p = e / jnp.sum(e, axis=-1, keepdims=True) * cmask_f
        psum = psum + p
        o = _dot(p.astype(BF16), vc)
        o_ref[0, :, r * hd:(r + 1) * hd] = o * _lane_col(gates, g * NSA_Q_PER_GROUP + r)

    ov = ov_ref[...]
    hi, mid, lo = _split3(psum)
    imp = _dot(hi, ov) + _dot(mid, ov) + _dot(lo, ov)
    q_blk = t // SLC_BLOCK
    forced = (lane == 0) | (lane == q_blk)
    score = jnp.where(forced, jnp.inf, jnp.where(lane <= q_blk, imp, -jnp.inf))
    live = lane < n_slc
    lane_f = lane.astype(F32)
    taken = jnp.zeros((ts, LANES), jnp.int32)
    for _ in range(min(N_SELECT, n_slc)):
        cand_ok = live & (taken == 0)
        best = jnp.max(jnp.where(cand_ok, score, -jnp.inf), axis=-1, keepdims=True)
        first = jnp.min(jnp.where(cand_ok & (score == best), lane_f, float(LANES)), axis=-1, keepdims=True)
        taken = jnp.where(lane_f == first, 1, taken)
    sel_ref[0, 0] = taken.astype(F32)


def _cmp_attn(proj, kv_cmp, gate_b_lane, overlap_bf, gate_col_block, ts=256):
    b, s, _ = proj.shape
    g, hd, r = NSA_KV_GROUPS, NSA_HEAD_DIM, NSA_Q_PER_GROUP
    n = kv_cmp.shape[-2]
    kern = functools.partial(_cmp_attn_kernel, ts=ts, n_slc=s // SLC_BLOCK)
    return pl.pallas_call(
        kern,
        grid=(b, g, s // ts),
        in_specs=[pl.BlockSpec((1, ts, r * hd), lambda bi, gi, si: (bi, si, gi)),
                  pl.BlockSpec((1, 1, 1, n, hd), lambda bi, gi, si: (bi, 0, gi, 0, 0)),
                  pl.BlockSpec((1, 1, 1, n, hd), lambda bi, gi, si: (bi, 1, gi, 0, 0)),
                  pl.BlockSpec((1, ts, LANES), lambda bi, gi, si: (bi, si, gate_col_block)),
                  pl.BlockSpec((1, LANES), lambda bi, gi, si: (0, 0)),
                  pl.BlockSpec((n, LANES), lambda bi, gi, si: (0, 0))],
        out_specs=[pl.BlockSpec((1, ts, r * hd), lambda bi, gi, si: (bi, si, gi)),
                   pl.BlockSpec((1, 1, ts, LANES), lambda bi, gi, si: (bi, gi, si, 0))],
        out_shape=[jax.ShapeDtypeStruct((b, s, g * r * hd), F32),
                   jax.ShapeDtypeStruct((b, g, s, LANES), F32)],
        compiler_params=_params(("parallel", "parallel", "parallel"), 32),
        name="nsa_cmp_attn_topk",
    )(proj, kv_cmp, kv_cmp, proj, gate_b_lane, overlap_bf)


def _sel_win_kernel(q_ref, ks_ref, vs_ref, kw_ref, vw_ref, sm_ref, ex_ref, gl_ref, gb_ref,
                    c_ref, sa_ref, sb_ref, o_ref, ks_rot, vs_ext, kw_rot, vw_ext, q_st, *, tq, tk):
    g = pl.program_id(1)
    qi = pl.program_id(2)
    hd, nr = NSA_HEAD_DIM, NSA_Q_PER_GROUP
    s_len = ks_rot.shape[0]
    scale = hd ** -0.5

    @pl.when(qi == 0)
    def _():
        c, sa, sb = c_ref[0], sa_ref[0], sb_ref[0]
        ks_rot[...] = _rope(ks_ref[0], c, sa, sb).astype(BF16)
        kw_rot[...] = _rope(kw_ref[0], c, sa, sb).astype(BF16)
        ones = jnp.ones((s_len, hd), BF16)
        vs_ext[...] = jnp.concatenate([vs_ref[0].astype(BF16), ones], axis=1)
        vw_ext[...] = jnp.concatenate([vw_ref[0].astype(BF16), ones], axis=1)

    q0 = pl.multiple_of(qi * tq, tq)
    ropes = (c_ref[0, pl.ds(q0, tq), :], sa_ref[0, pl.ds(q0, tq), :], sb_ref[0, pl.ds(q0, tq), :])
    for r in range(nr):
        q = _rope(q_ref[0, :, r * hd:(r + 1) * hd], *ropes) * scale
        q_st[r * tq:(r + 1) * tq, :] = q.astype(BF16)
    gates = jax.nn.sigmoid(gl_ref[0] + gb_ref[...])

    def attend(k, v_ext, bias):
        n = k.shape[0]
        s = _dot_nt(q_st[...], k).reshape(nr, tq, n) + bias[None]
        p = jnp.exp(s - jnp.max(s, axis=-1, keepdims=True))
        o_ext = _dot(p.reshape(nr * tq, n).astype(BF16), v_ext)
        return o_ext[:, :hd] / o_ext[:, hd:hd + 1]

    wk = WINDOW + tq
    start = pl.multiple_of(jnp.maximum(q0 - WINDOW, 0), tq)
    dist = (q0 - start) + lax.broadcasted_iota(jnp.int32, (tq, wk), 0) - lax.broadcasted_iota(jnp.int32, (tq, wk), 1)
    sm = sm_ref[0, 0].astype(BF16)
    n_chunks = (q0 + tq + tk - 1) // tk
    for n in range(1, s_len // tk + 1):
        @pl.when(n_chunks == n)
        def _():
            nk = n * tk
            bias_w = jnp.where((dist >= 0) & (dist < WINDOW), 0.0, NEG_INF)
            o_win = attend(kw_rot[pl.ds(start, wk), :], vw_ext[pl.ds(start, wk), :], bias_w)
            member = _dot(sm, ex_ref[:, :nk])
            t = q0 + lax.broadcasted_iota(jnp.int32, (tq, nk), 0)
            kpos = lax.broadcasted_iota(jnp.int32, (tq, nk), 1)
            bias_s = jnp.where((member > 0.5) & (kpos <= t), 0.0, NEG_INF)
            o_sel = attend(ks_rot[:nk, :], vs_ext[:nk, :], bias_s)
            for r in range(nr):
                rows = slice(r * tq, (r + 1) * tq)
                o_ref[0, :, r * hd:(r + 1) * hd] = (
                    o_sel[rows, :] * _lane_col(gates, NSA_HEADS + g * nr + r)
                    + o_win[rows, :] * _lane_col(gates, 2 * NSA_HEADS + g * nr + r))


def _sel_win_attn(proj, sel_mask, expand_bf, gate_b_lane, tables, sel_col_block0, win_col_block0,
                  gate_col_block, tq=128, tk=512):
    b, s, _ = proj.shape
    g, hd, r = NSA_KV_GROUPS, NSA_HEAD_DIM, NSA_Q_PER_GROUP
    q_spec = pl.BlockSpec((1, tq, r * hd), lambda bi, gi, qi: (bi, qi, gi))

    def kv_spec(col_block):
        return pl.BlockSpec((1, s, hd), lambda bi, gi, qi: (bi, 0, col_block + gi))

    tab_spec = pl.BlockSpec((1, s, LANES), lambda bi, gi, qi: (bi, 0, 0))
    return pl.pallas_call(
        functools.partial(_sel_win_kernel, tq=tq, tk=tk),
        grid=(b, g, s // tq),
        in_specs=[q_spec, kv_spec(sel_col_block0), kv_spec(sel_col_block0 + g),
                  kv_spec(win_col_block0), kv_spec(win_col_block0 + g),
                  pl.BlockSpec((1, 1, tq, LANES), lambda bi, gi, qi: (bi, gi, qi, 0)),
                  pl.BlockSpec((LANES, s), lambda bi, gi, qi: (0, 0)),
                  pl.BlockSpec((1, tq, LANES), lambda bi, gi, qi: (bi, qi, gate_col_block)),
                  pl.BlockSpec((1, LANES), lambda bi, gi, qi: (0, 0)),
                  tab_spec, tab_spec, tab_spec],
        out_specs=q_spec,
        out_shape=jax.ShapeDtypeStruct((b, s, g * r * hd), F32),
        scratch_shapes=[pltpu.VMEM((s, hd), BF16), pltpu.VMEM((s, 2 * hd), BF16)] * 2
        + [pltpu.VMEM((r * tq, hd), BF16)],
        compiler_params=_params(("parallel", "parallel", "arbitrary"), 56),
        name="nsa_sel_win_attn",
    )(proj, proj, proj, proj, proj, sel_mask, expand_bf, proj, gate_b_lane, *tables)


def _pad_cols(w, mult):
    n = w.shape[-1]
    return jnp.pad(w, ((0, 0), (0, (-n) % mult)))


def _lane_row(v):
    v = v.reshape(-1)
    return jnp.pad(v, (0, LANES - v.shape[0])).reshape(1, LANES)


def _nsa_mixer(x2d, b, s, norm_g, tables, w_in, gate_b, cmp_pe, cmp_w1, cmp_w2, w_out):
    g, hd, h = NSA_KV_GROUPS, NSA_HEAD_DIM, NSA_HEADS
    proj = _norm_matmul(x2d, norm_g, _pad_cols(w_in, 512).astype(BF16))
    proj = proj.reshape(b, s, -1)
    q_blocks = h
    cmp0, sel0, win0 = q_blocks, q_blocks + 2 * g, q_blocks + 4 * g
    gate_blk = q_blocks + 6 * g
    gate_b_lane = _lane_row(gate_b)
    kv_cmp = _cmp_mlp(proj, cmp_pe, cmp_w1.astype(BF16), cmp_w2.astype(BF16), cmp0)

    n_cmp_pad = s // CMP_STRIDE
    n_slc = s // SLC_BLOCK
    jj = np.arange(n_cmp_pad)[:, None]
    ss = np.arange(LANES)[None, :]
    lo = np.maximum(jj * CMP_STRIDE, ss * SLC_BLOCK)
    hi = np.minimum(jj * CMP_STRIDE + CMP_BLOCK, ss * SLC_BLOCK + SLC_BLOCK)
    overlap = np.clip(hi - lo, 0, None).astype(np.float32) / CMP_BLOCK
    overlap[(s - CMP_BLOCK) // CMP_STRIDE + 1:, :] = 0.0
    overlap[:, n_slc:] = 0.0
    expand = (np.arange(s)[None, :] // SLC_BLOCK == np.arange(LANES)[:, None]).astype(np.float32)

    o_cmp, sel_mask = _cmp_attn(proj, kv_cmp, gate_b_lane, jnp.asarray(overlap, BF16), gate_blk)
    o_rot = _sel_win_attn(proj, sel_mask, jnp.asarray(expand, BF16), gate_b_lane, tables, sel0, win0, gate_blk)
    t = b * s
    return _out_proj([o_cmp.reshape(t, -1), o_rot.reshape(t, -1)], w_out.astype(BF16), x2d)


def _mlstm_gate_kernel(gl_ref, gb_ref, tri_ref, li_ref, f_ref):
    pre = gl_ref[0] + gb_ref[...]
    capped = GATE_SOFTCAP * jnp.tanh(pre / GATE_SOFTCAP)
    li_ref[0] = capped
    lf = jax.nn.log_sigmoid(capped)
    rows = tri_ref.shape[0]
    tri = tri_ref[...]
    for sl in range(lf.shape[0] // rows):
        hi, mid, lo = _split3(lf[sl * rows:(sl + 1) * rows, :])
        f_ref[0, pl.ds(sl * rows, rows), :] = _dot(tri, hi) + _dot(tri, mid) + _dot(tri, lo)


def _mlstm_gates(proj, gate_b_lane, gate_col_block, slab=256):
    b, s, _ = proj.shape
    idx = np.arange(slab)
    tri = ((idx[:, None] >= idx[None, :]) &
           (idx[:, None] // MLSTM_CHUNK == idx[None, :] // MLSTM_CHUNK)).astype(np.float32)
    spec = pl.BlockSpec((1, s, LANES), lambda bi: (bi, 0, 0))
    shape = jax.ShapeDtypeStruct((b, s, LANES), F32)
    return pl.pallas_call(
        _mlstm_gate_kernel,
        grid=(b,),
        in_specs=[pl.BlockSpec((1, s, LANES), lambda bi: (bi, 0, gate_col_block)),
                  pl.BlockSpec((1, LANES), lambda bi: (0, 0)),
                  pl.BlockSpec((slab, slab), lambda bi: (0, 0))],
        out_specs=[spec, spec],
        out_shape=[shape, shape],
        compiler_params=_params(("parallel",), 32),
        name="mlstm_gates",
    )(proj, gate_b_lane, jnp.asarray(tri, BF16))


def _mlstm_kernel(q_ref, k_ref, v_ref, og_ref, li_ref, f_ref, hn_ref, o_ref, state_ref):
    h = pl.program_id(1)
    L, dk, dv = MLSTM_CHUNK, MLSTM_QK_DIM, MLSTM_V_DIM
    s = q_ref.shape[1]
    state_ref[...] = jnp.zeros_like(state_ref)
    lane = lax.broadcasted_iota(jnp.int32, (L, LANES), 1)
    causal = lax.broadcasted_iota(jnp.int32, (L, L), 1) <= lax.broadcasted_iota(jnp.int32, (L, L), 0)
    ones_col = (lane < 3).astype(F32)
    ones_row = ((lane >= 3) & (lane < 6)).astype(F32)

    def body(c, m):
        r0 = pl.multiple_of(c * L, L)
        rows = pl.ds(r0, L)
        qb = (q_ref[0, rows, :] * dk ** -0.5).astype(BF16)
        k = k_ref[0, rows, :]
        vb = v_ref[0, rows, :].astype(BF16)
        li = _lane_col(li_ref[0, rows, :], h)
        fc = _lane_col(f_ref[0, rows, :], MLSTM_HEADS + h)
        f_end = fc[L - 1:L, :]
        a = f_end - fc + li
        a_max = jnp.max(a, axis=0, keepdims=True)
        w = jnp.exp(a - a_max)

        fh, fm, fl = _split3(fc)
        gh, gm, gl = _split3(li - fc)
        lhs = jnp.where(lane == 0, fh.astype(F32), jnp.where(lane == 1, fm.astype(F32),
              jnp.where(lane == 2, fl.astype(F32), ones_row)))
        rhs = jnp.where(lane == 3, gh.astype(F32), jnp.where(lane == 4, gm.astype(F32),
              jnp.where(lane == 5, gl.astype(F32), ones_col)))
        d = jnp.where(causal, _dot_nt(lhs.astype(BF16), rhs.astype(BF16)), -jnp.inf)

        inter = fc + m
        m_t = jnp.maximum(inter, jnp.max(d, axis=-1, keepdims=True))
        g_inter = jnp.exp(inter - m_t)
        sqk = _dot_nt(qb, k.astype(BF16)) * jnp.exp(d - m_t)
        state = state_ref[...]
        from_state = _dot(qb, state.astype(BF16))
        num = _dot(sqk.astype(BF16), vb) + g_inter * from_state[:, :dv]
        den = jnp.sum(sqk, axis=-1, keepdims=True) + g_inter * from_state[:, dv:dv + 1]
        hv = num / jnp.maximum(jnp.abs(den), jnp.exp(-m_t))
        hv = hv * lax.rsqrt(jnp.mean(hv * hv, axis=-1, keepdims=True) + NORM_EPS) * hn_ref[0]
        o_ref[0, rows, :] = hv * jax.nn.sigmoid(og_ref[0, rows, :])

        v_ext = jnp.concatenate([vb, jnp.ones((L, LANES), BF16)], axis=1)
        d_state = _dot_tn((w * k).astype(BF16), v_ext)
        m_new = jnp.maximum(f_end + m, a_max)
        state_ref[...] = jnp.exp(f_end + m - m_new) * state + jnp.exp(a_max - m_new) * d_state
        return m_new

    lax.fori_loop(0, s // L, body, jnp.zeros((1, 1), F32), unroll=4)


def _mlstm_core(proj, li, fcum, head_norm):
    b, s, _ = proj.shape
    hh, dk, dv = MLSTM_HEADS, MLSTM_QK_DIM, MLSTM_V_DIM
    v_blk0 = 2 * hh * dk // dv
    og_blk0 = v_blk0 + hh
    gate_spec = pl.BlockSpec((1, s, LANES), lambda bi, hi: (bi, 0, 0))
    return pl.pallas_call(
        _mlstm_kernel,
        grid=(b, hh),
        in_specs=[pl.BlockSpec((1, s, dk), lambda bi, hi: (bi, 0, hi)),
                  pl.BlockSpec((1, s, dk), lambda bi, hi: (bi, 0, hh + hi)),
                  pl.BlockSpec((1, s, dv), lambda bi, hi: (bi, 0, v_blk0 + hi)),
                  pl.BlockSpec((1, s, dv), lambda bi, hi: (bi, 0, og_blk0 + hi)),
                  gate_spec, gate_spec,
                  pl.BlockSpec((1, 1, dv), lambda bi, hi: (hi, 0, 0))],
        out_specs=pl.BlockSpec((1, s, dv), lambda bi, hi: (bi, 0, hi)),
        out_shape=jax.ShapeDtypeStruct((b, s, hh * dv), F32),
        scratch_shapes=[pltpu.VMEM((dk, dv + LANES), F32)],
        compiler_params=_params(("parallel", "parallel"), 40),
        name="mlstm_chunkwise",
    )(proj, proj, proj, proj, li, fcum, head_norm.reshape(hh, 1, dv))


def _mlstm_mixer(x2d, b, s, norm_g, w_in, gate_b, head_norm, w_out):
    hh, dk, dv = MLSTM_HEADS, MLSTM_QK_DIM, MLSTM_V_DIM
    proj = _norm_matmul(x2d, norm_g, _pad_cols(w_in, 512).astype(BF16))
    proj = proj.reshape(b, s, -1)
    gate_blk = (2 * hh * dk + 2 * hh * dv) // LANES
    li, fcum = _mlstm_gates(proj, _lane_row(gate_b), gate_blk)
    hcat = _mlstm_core(proj, li, fcum, head_norm)
    return _out_proj([hcat.reshape(b * s, -1)], w_out.astype(BF16), x2d)


def kernel(x, positions, nsa_w_in, nsa_gate_b, nsa_cmp_pe, nsa_cmp_w1, nsa_cmp_w2, nsa_w_out,
           mlstm_w_in, mlstm_gate_b, mlstm_head_norm, mlstm_w_out,
           norm_mix, norm_ffn, ffn_w_up, ffn_conv_w, ffn_conv_b, ffn_w_down, norm_final):
    b, s, d = x.shape
    depth = norm_mix.shape[0]
    tables = _rope_tables(positions)
    x2d = x.reshape(b * s, d)
    for i in range(depth):
        j = i // 2
        if i % 2 == 0:
            x2d = _nsa_mixer(x2d, b, s, norm_mix[i], tables, nsa_w_in[j], nsa_gate_b[j], nsa_cmp_pe[j],
                             nsa_cmp_w1[j], nsa_cmp_w2[j], nsa_w_out[j])
        else:
            x2d = _mlstm_mixer(x2d, b, s, norm_mix[i], mlstm_w_in[j], mlstm_gate_b[j],
                               mlstm_head_norm[j], mlstm_w_out[j])
        x2d = _ffn(x2d, norm_ffn[i], ffn_w_up[i].astype(BF16), ffn_conv_w[i], ffn_conv_b[i],
                   ffn_w_down[i].astype(BF16), s)
    return _rmsnorm(x2d, norm_final).reshape(b, s, d)
```

```python
import functools

import numpy as np
import jax
import jax.numpy as jnp
from jax import lax
from jax.experimental import pallas as pl
from jax.experimental.pallas import tpu as pltpu

F32 = jnp.float32
BF16 = jnp.bfloat16

LANES = 128
NORM_EPS = 1e-6
NEG_INF = -1e30

NSA_HEADS = 16
NSA_KV_GROUPS = 2
NSA_HEAD_DIM = 128
NSA_Q_PER_GROUP = NSA_HEADS // NSA_KV_GROUPS
CMP_BLOCK = 32
CMP_STRIDE = 16
SLC_BLOCK = 64
N_SELECT = 8
WINDOW = 512
ROPE_THETA = 500000.0
ROPE_DIM = NSA_HEAD_DIM // 4
ROPE_HALF = ROPE_DIM // 2

MLSTM_HEADS = 8
MLSTM_QK_DIM = 128
MLSTM_V_DIM = 256
MLSTM_CHUNK = 64
MLSTM_HEADS_PER_STEP = 1
GATE_SOFTCAP = 15.0

CONV_WIDTH = 3
CONV_HALO = 8
CONV_ROWS = 16

MIB = 1024 * 1024


def _params(semantics, vmem_mib):
    return pltpu.CompilerParams(dimension_semantics=semantics, vmem_limit_bytes=vmem_mib * MIB)


def _dot(a, b):
    return jnp.dot(a, b, preferred_element_type=F32)


def _dot_nt(a, b):
    return lax.dot_general(a, b, (((1,), (1,)), ((), ())), preferred_element_type=F32)


def _dot_tn(a, b):
    return lax.dot_general(a, b, (((0,), (0,)), ((), ())), preferred_element_type=F32)


def _split3(x):
    hi = x.astype(BF16)
    r1 = x - hi.astype(F32)
    mid = r1.astype(BF16)
    lo = (r1 - mid.astype(F32)).astype(BF16)
    return hi, mid, lo


def _rms_rows(x, g):
    ms = jnp.mean(x * x, axis=-1, keepdims=True)
    return x * lax.rsqrt(ms + NORM_EPS) * g


def _lane_col(x, idx):
    lane = lax.broadcasted_iota(jnp.int32, x.shape, 1)
    return jnp.sum(jnp.where(lane == idx, x, 0.0), axis=-1, keepdims=True)


def _norm_matmul_kernel(x_ref, g_ref, w_ref, o_ref, xn_ref):
    @pl.when(pl.program_id(1) == 0)
    def _():
        xn_ref[...] = _rms_rows(x_ref[...], g_ref[...]).astype(BF16)

    o_ref[...] = _dot(xn_ref[...], w_ref[...])


def _norm_matmul(x2d, g, w_bf, tm=1024, tn=512):
    t, d = x2d.shape
    n = w_bf.shape[1]
    return pl.pallas_call(
        _norm_matmul_kernel,
        grid=(t // tm, n // tn),
        in_specs=[pl.BlockSpec((tm, d), lambda i, j: (i, 0)),
                  pl.BlockSpec((1, d), lambda i, j: (0, 0)),
                  pl.BlockSpec((d, tn), lambda i, j: (0, j))],
        out_specs=pl.BlockSpec((tm, tn), lambda i, j: (i, j)),
        out_shape=jax.ShapeDtypeStruct((t, n), F32),
        scratch_shapes=[pltpu.VMEM((tm, d), BF16)],
        compiler_params=_params(("parallel", "arbitrary"), 40),
        name="norm_matmul",
    )(x2d, g.reshape(1, d), w_bf)


def _out_proj_kernel(*refs, n_in):
    a_refs = refs[:n_in]
    w_ref, r_ref, o_ref, a_bf = refs[n_in:]

    @pl.when(pl.program_id(1) == 0)
    def _():
        a = a_refs[0][...]
        for ar in a_refs[1:]:
            a = a + ar[...]
        a_bf[...] = a.astype(BF16)

    o_ref[...] = r_ref[...] + _dot(a_bf[...], w_ref[...])


def _out_proj(a_list, w_bf, resid, tm=1024, tn=512):
    t, k = a_list[0].shape
    n = w_bf.shape[1]
    n_in = len(a_list)
    return pl.pallas_call(
        functools.partial(_out_proj_kernel, n_in=n_in),
        grid=(t // tm, n // tn),
        in_specs=[pl.BlockSpec((tm, k), lambda i, j: (i, 0))] * n_in
        + [pl.BlockSpec((k, tn), lambda i, j: (0, j)),
           pl.BlockSpec((tm, tn), lambda i, j: (i, j))],
        out_specs=pl.BlockSpec((tm, tn), lambda i, j: (i, j)),
        out_shape=jax.ShapeDtypeStruct((t, n), F32),
        scratch_shapes=[pltpu.VMEM((tm, k), BF16)],
        compiler_params=_params(("parallel", "arbitrary"), 56),
        name="out_proj",
    )(*a_list, w_bf, resid)


def _ffn_kernel(x_ref, g_ref, wg_ref, wv_ref, cwg_ref, cwv_ref, cbg_ref, cbv_ref, wd_ref, xr_ref, o_ref,
                xn_ref, h_ref, h_last, ug0, uv0, ug1, uv1, carry_g, carry_v, *, tm, nf, tiles_per_seq):
    i = pl.program_id(0)
    st = pl.program_id(1)
    u_bufs = ((ug0, uv0), (ug1, uv1))

    @pl.when(st == 0)
    def _():
        xn_ref[...] = _rms_rows(x_ref[...], g_ref[...]).astype(BF16)

        @pl.when(i == 0)
        def _():
            for ref in (ug0, uv0, ug1, uv1, carry_g, carry_v):
                ref[...] = jnp.zeros_like(ref)

    def conv_glu(bufs, dst):
        def branch(u_ref, cw_ref, cb_ref, r0):
            def tap(k):
                return cw_ref[k * CONV_ROWS:(k + 1) * CONV_ROWS, :]

            c = cb_ref[...] + tap(0) * u_ref[pl.ds(r0 + CONV_HALO - 2, CONV_ROWS), :]
            c = c + tap(1) * u_ref[pl.ds(r0 + CONV_HALO - 1, CONV_ROWS), :]
            return c + tap(2) * u_ref[pl.ds(r0 + CONV_HALO, CONV_ROWS), :]

        for r0 in range(0, tm, CONV_ROWS):
            gate = branch(bufs[0], cwg_ref, cbg_ref, r0)
            val = branch(bufs[1], cwv_ref, cbv_ref, r0)
            dst[pl.ds(r0, CONV_ROWS), :] = (gate * val * (1.0 / (1.0 + jnp.exp(-gate)))).astype(BF16)

    def up_proj(bufs):
        xn = xn_ref[...]
        seq_start = (i % tiles_per_seq) == 0
        for w_ref, u_ref, carry in ((wg_ref, bufs[0], carry_g), (wv_ref, bufs[1], carry_v)):
            u = _dot(xn, w_ref[...])
            u_ref[pl.ds(CONV_HALO, tm), :] = u
            u_ref[pl.ds(0, CONV_HALO), :] = jnp.where(seq_start, 0.0, carry[st])
            carry[st] = u[tm - CONV_HALO:, :]

    for parity in range(2):
        @pl.when((st < nf) & (st % 2 == parity))
        def _():
            conv_glu(u_bufs[1 - parity], h_ref.at[jnp.maximum(st - 1, 0)])
            up_proj(u_bufs[parity])

    def down_proj(order):
        acc = xr_ref[...]
        for f in order:
            acc = acc + _dot(h_last[...] if f == nf - 1 else h_ref[f], wd_ref[f])
        o_ref[...] = acc

    @pl.when(st == nf)
    def _():
        conv_glu(u_bufs[(nf - 1) % 2], h_last)
        mid = (nf - 1) // 2
        down_proj(list(range(mid)) + [nf - 1] + list(range(mid, nf - 1)))

    @pl.when(st > nf)
    def _():
        down_proj(range(nf))


def _ffn(x2d, g, w_up_bf, conv_w, conv_b, w_down_bf, seq, tm=1024, tf=512, tn=256):
    t, d = x2d.shape
    d_ff = w_down_bf.shape[0]
    nf = d_ff // tf
    nn = d // tn
    kern = functools.partial(_ffn_kernel, tm=tm, nf=nf, tiles_per_seq=seq // tm)
    conv_w_rows = jnp.repeat(conv_w, CONV_ROWS, axis=0)
    conv_b_rows = jnp.broadcast_to(conv_b.reshape(1, -1), (CONV_ROWS, conv_b.shape[-1]))

    def up(st):
        return jnp.minimum(st, nf - 1)

    def prev(st):
        return jnp.clip(st - 1, 0, nf - 1)

    def down(st):
        return jnp.maximum(st - nf, 0)

    return pl.pallas_call(
        kern,
        grid=(t // tm, nf + nn),
        in_specs=[pl.BlockSpec((tm, d), lambda i, st: (i, 0), pipeline_mode=pl.Buffered(1)),
                  pl.BlockSpec((1, d), lambda i, st: (0, 0)),
                  pl.BlockSpec((d, tf), lambda i, st: (0, up(st))),
                  pl.BlockSpec((d, tf), lambda i, st: (0, up(st) + nf)),
                  pl.BlockSpec((CONV_WIDTH * CONV_ROWS, tf), lambda i, st: (0, prev(st))),
                  pl.BlockSpec((CONV_WIDTH * CONV_ROWS, tf), lambda i, st: (0, prev(st) + nf)),
                  pl.BlockSpec((CONV_ROWS, tf), lambda i, st: (0, prev(st))),
                  pl.BlockSpec((CONV_ROWS, tf), lambda i, st: (0, prev(st) + nf)),
                  pl.BlockSpec((nf, tf, tn), lambda i, st: (0, 0, down(st))),
                  pl.BlockSpec((tm, tn), lambda i, st: (i, down(st)))],
        out_specs=pl.BlockSpec((tm, tn), lambda i, st: (i, down(st))),
        out_shape=jax.ShapeDtypeStruct((t, d), F32),
        scratch_shapes=[pltpu.VMEM((tm, d), BF16),
                        pltpu.VMEM((nf - 1, tm, tf), BF16),
                        pltpu.VMEM((tm, tf), BF16)]
        + [pltpu.VMEM((tm + CONV_HALO, tf), F32)] * 4
        + [pltpu.VMEM((nf, CONV_HALO, tf), F32)] * 2,
        compiler_params=_params(("arbitrary", "arbitrary"), 56),
        name="conv_glu_ffn",
    )(x2d, g.reshape(1, d), w_up_bf, w_up_bf, conv_w_rows, conv_w_rows,
      conv_b_rows, conv_b_rows, w_down_bf.reshape(nf, tf, d), x2d)


def _rmsnorm_kernel(x_ref, g_ref, o_ref):
    o_ref[...] = _rms_rows(x_ref[...], g_ref[...])


def _rmsnorm(x2d, g, tm=512):
    t, d = x2d.shape
    return pl.pallas_call(
        _rmsnorm_kernel,
        grid=(t // tm,),
        in_specs=[pl.BlockSpec((tm, d), lambda i: (i, 0)), pl.BlockSpec((1, d), lambda i: (0, 0))],
        out_specs=pl.BlockSpec((tm, d), lambda i: (i, 0)),
        out_shape=jax.ShapeDtypeStruct((t, d), F32),
        compiler_params=_params(("parallel",), 32),
        name="final_rmsnorm",
    )(x2d, g.reshape(1, d))


def _rope_table_kernel(pos_ref, inv_ref, c_ref, sa_ref, sb_ref):
    ang = pos_ref[0].astype(F32) * inv_ref[...]
    lane = lax.broadcasted_iota(jnp.int32, ang.shape, 1)
    cos = jnp.cos(ang)
    sin = jnp.sin(ang)
    c_ref[0] = jnp.where(lane < ROPE_DIM, cos, 1.0)
    sa_ref[0] = jnp.where((lane >= ROPE_HALF) & (lane < ROPE_DIM), sin, 0.0)
    sb_ref[0] = jnp.where(lane < ROPE_HALF, -sin, 0.0)


def _rope_tables(positions):
    b, s = positions.shape
    inv = jnp.power(ROPE_THETA, -jnp.arange(0, ROPE_DIM, 2, dtype=F32) / ROPE_DIM)
    inv_lane = jnp.concatenate([inv, inv, jnp.zeros((LANES - ROPE_DIM,), F32)]).reshape(1, LANES)
    spec = pl.BlockSpec((1, s, LANES), lambda i: (i, 0, 0))
    shape = jax.ShapeDtypeStruct((b, s, LANES), F32)
    return pl.pallas_call(
        _rope_table_kernel,
        grid=(b,),
        in_specs=[pl.BlockSpec((1, s, 1), lambda i: (i, 0, 0)),
                  pl.BlockSpec((1, LANES), lambda i: (0, 0))],
        out_specs=[spec, spec, spec],
        out_shape=[shape, shape, shape],
        compiler_params=_params(("parallel",), 32),
        name="rope_tables",
    )(positions.reshape(b, s, 1), inv_lane)


def _rope(x, c, sa, sb):
    return x * c + pltpu.roll(x, ROPE_HALF, 1) * sa + pltpu.roll(x, LANES - ROPE_HALF, 1) * sb


def _cmp_mlp_kernel(x_ref, pe_ref, w1_ref, w2_ref, o_ref):
    n = o_ref.shape[-2]
    acc_lo = jnp.zeros((n, NSA_HEAD_DIM), F32)
    acc_hi = jnp.zeros((n, NSA_HEAD_DIM), F32)
    for l in range(CMP_STRIDE):
        xs = x_ref[0, pl.ds(l, n, stride=CMP_STRIDE), :]
        acc_lo += _dot((xs + pe_ref[0, l:l + 1, :]).astype(BF16), w1_ref[0, l])
        acc_hi += _dot((xs + pe_ref[0, CMP_STRIDE + l:CMP_STRIDE + l + 1, :]).astype(BF16),
                       w1_ref[0, CMP_STRIDE + l])
    hid = jax.nn.gelu(acc_lo + pltpu.roll(acc_hi, n - 1, 0))
    o_ref[0, 0, 0] = _dot(hid.astype(BF16), w2_ref[0])


def _cmp_mlp(proj, pe, w1_bf, w2_bf, col_block0):
    b, s, _ = proj.shape
    g, hd = NSA_KV_GROUPS, NSA_HEAD_DIM
    n = s // CMP_STRIDE
    return pl.pallas_call(
        _cmp_mlp_kernel,
        grid=(b, 2, g),
        in_specs=[pl.BlockSpec((1, s, hd), lambda bi, c, gi: (bi, 0, col_block0 + c * g + gi)),
                  pl.BlockSpec((1, CMP_BLOCK, hd), lambda bi, c, gi: (c, 0, 0)),
                  pl.BlockSpec((1, CMP_BLOCK, hd, hd), lambda bi, c, gi: (c, 0, 0, 0)),
                  pl.BlockSpec((1, hd, hd), lambda bi, c, gi: (c, 0, 0))],
        out_specs=pl.BlockSpec((1, 1, 1, n, hd), lambda bi, c, gi: (bi, c, gi, 0, 0)),
        out_shape=jax.ShapeDtypeStruct((b, 2, g, n, hd), F32),
        compiler_params=_params(("parallel", "parallel", "parallel"), 32),
        name="nsa_cmp_mlp",
    )(proj, pe, w1_bf, w2_bf)


def _cmp_attn_kernel(q_ref, kc_ref, vc_ref, gl_ref, gb_ref, ov_ref, o_ref, sel_ref, *, ts, n_slc):
    g = pl.program_id(1)
    si = pl.program_id(2)
    hd = NSA_HEAD_DIM
    scale = hd ** -0.5
    lane = lax.broadcasted_iota(jnp.int32, (ts, LANES), 1)
    t = si * ts + lax.broadcasted_iota(jnp.int32, (ts, LANES), 0)
    cmask = (lane * CMP_STRIDE + (CMP_BLOCK - 1)) <= t
    cmask_f = cmask.astype(F32)
    kc = kc_ref[0, 0, 0].astype(BF16)
    vc = vc_ref[0, 0, 0].astype(BF16)
    gates = jax.nn.sigmoid(gl_ref[0] + gb_ref[...])
    psum = jnp.zeros((ts, LANES), F32)
    for r in range(NSA_Q_PER_GROUP):
        q = q_ref[0, :, r * hd:(r + 1) * hd].astype(BF16)
        s = jnp.where(cmask, _dot_nt(q, kc) * scale, NEG_INF)
        e = jnp.exp(s - jnp.max(s, axis=-1, keepdims=True))
        p = e / jnp.sum(e, axis=-1, keepdims=True) * cmask_f
        psum = psum + p
        o = _dot(p.astype(BF16), vc)
        o_ref[0, :, r * hd:(r + 1) * hd] = o * _lane_col(gates, g * NSA_Q_PER_GROUP + r)

    ov = ov_ref[...]
    hi, mid, lo = _split3(psum)
    imp = _dot(hi, ov) + _dot(mid, ov) + _dot(lo, ov)
    q_blk = t // SLC_BLOCK
    forced = (lane == 0) | (lane == q_blk)
    score = jnp.where(forced, jnp.inf, jnp.where(lane <= q_blk, imp, -jnp.inf))
    live = lane < n_slc
    lane_f = lane.astype(F32)
    taken = jnp.zeros((ts, LANES), jnp.int32)
    for _ in range(min(N_SELECT, n_slc)):
        cand_ok = live & (taken == 0)
        best = jnp.max(jnp.where(cand_ok, score, -jnp.inf), axis=-1, keepdims=True)
        first = jnp.min(jnp.where(cand_ok & (score == best), lane_f, float(LANES)), axis=-1, keepdims=True)
        taken = jnp.where(lane_f == first, 1, taken)
    sel_ref[0, 0] = taken.astype(F32)


def _cmp_attn(proj, kv_cmp, gate_b_lane, overlap_bf, gate_col_block, ts=256):
    b, s, _ = proj.shape
    g, hd, r = NSA_KV_GROUPS, NSA_HEAD_DIM, NSA_Q_PER_GROUP
    n = kv_cmp.shape[-2]
    kern = functools.partial(_cmp_attn_kernel, ts=ts, n_slc=s // SLC_BLOCK)
    return pl.pallas_call(
        kern,
        grid=(b, g, s // ts),
        in_specs=[pl.BlockSpec((1, ts, r * hd), lambda bi, gi, si: (bi, si, gi)),
                  pl.BlockSpec((1, 1, 1, n, hd), lambda bi, gi, si: (bi, 0, gi, 0, 0)),
                  pl.BlockSpec((1, 1, 1, n, hd), lambda bi, gi, si: (bi, 1, gi, 0, 0)),
                  pl.BlockSpec((1, ts, LANES), lambda bi, gi, si: (bi, si, gate_col_block)),
                  pl.BlockSpec((1, LANES), lambda bi, gi, si: (0, 0)),
                  pl.BlockSpec((n, LANES), lambda bi, gi, si: (0, 0))],
        out_specs=[pl.BlockSpec((1, ts, r * hd), lambda bi, gi, si: (bi, si, gi)),
                   pl.BlockSpec((1, 1, ts, LANES), lambda bi, gi, si: (bi, gi, si, 0))],
        out_shape=[jax.ShapeDtypeStruct((b, s, g * r * hd), F32),
                   jax.ShapeDtypeStruct((b, g, s, LANES), F32)],
        compiler_params=_params(("parallel", "parallel", "parallel"), 32),
        name="nsa_cmp_attn_topk",
    )(proj, kv_cmp, kv_cmp, proj, gate_b_lane, overlap_bf)


def _sel_win_kernel(q_ref, ks_ref, vs_ref, kw_ref, vw_ref, sm_ref, ex_ref, gl_ref, gb_ref,
                    c_ref, sa_ref, sb_ref, o_ref, ks_rot, vs_ext, kw_rot, vw_ext, q_st, *, tq, tk):
    g = pl.program_id(1)
    qi = pl.program_id(2)
    hd, nr = NSA_HEAD_DIM, NSA_Q_PER_GROUP
    s_len = ks_rot.shape[0]
    scale = hd ** -0.5

    @pl.when(qi == 0)
    def _():
        c, sa, sb = c_ref[0], sa_ref[0], sb_ref[0]
        ks_rot[...] = _rope(ks_ref[0], c, sa, sb).astype(BF16)
        kw_rot[...] = _rope(kw_ref[0], c, sa, sb).astype(BF16)
        ones = jnp.ones((s_len, hd), BF16)
        vs_ext[...] = jnp.concatenate([vs_ref[0].astype(BF16), ones], axis=1)
        vw_ext[...] = jnp.concatenate([vw_ref[0].astype(BF16), ones], axis=1)

    q0 = pl.multiple_of(qi * tq, tq)
    ropes = (c_ref[0, pl.ds(q0, tq), :], sa_ref[0, pl.ds(q0, tq), :], sb_ref[0, pl.ds(q0, tq), :])
    for r in range(nr):
        q = _rope(q_ref[0, :, r * hd:(r + 1) * hd], *ropes) * scale
        q_st[r * tq:(r + 1) * tq, :] = q.astype(BF16)
    gates = jax.nn.sigmoid(gl_ref[0] + gb_ref[...])

    def attend(k, v_ext, bias):
        n = k.shape[0]
        s = _dot_nt(q_st[...], k).reshape(nr, tq, n) + bias[None]
        p = jnp.exp(s - jnp.max(s, axis=-1, keepdims=True))
        o_ext = _dot(p.reshape(nr * tq, n).astype(BF16), v_ext)
        return o_ext[:, :hd] / o_ext[:, hd:hd + 1]

    wk = WINDOW + tq
    start = pl.multiple_of(jnp.maximum(q0 - WINDOW, 0), tq)
    dist = (q0 - start) + lax.broadcasted_iota(jnp.int32, (tq, wk), 0) - lax.broadcasted_iota(jnp.int32, (tq, wk), 1)
    sm = sm_ref[0, 0].astype(BF16)
    n_chunks = (q0 + tq + tk - 1) // tk
    for n in range(1, s_len // tk + 1):
        @pl.when(n_chunks == n)
        def _():
            nk = n * tk
            bias_w = jnp.where((dist >= 0) & (dist < WINDOW), 0.0, NEG_INF)
            o_win = attend(kw_rot[pl.ds(start, wk), :], vw_ext[pl.ds(start, wk), :], bias_w)
            member = _dot(sm, ex_ref[:, :nk])
            t = q0 + lax.broadcasted_iota(jnp.int32, (tq, nk), 0)
            kpos = lax.broadcasted_iota(jnp.int32, (tq, nk), 1)
            bias_s = jnp.where((member > 0.5) & (kpos <= t), 0.0, NEG_INF)
            o_sel = attend(ks_rot[:nk, :], vs_ext[:nk, :], bias_s)
            for r in range(nr):
                rows = slice(r * tq, (r + 1) * tq)
                o_ref[0, :, r * hd:(r + 1) * hd] = (
                    o_sel[rows, :] * _lane_col(gates, NSA_HEADS + g * nr + r)
                    + o_win[rows, :] * _lane_col(gates, 2 * NSA_HEADS + g * nr + r))


def _sel_win_attn(proj, sel_mask, expand_bf, gate_b_lane, tables, sel_col_block0, win_col_block0,
                  gate_col_block, tq=128, tk=512):
    b, s, _ = proj.shape
    g, hd, r = NSA_KV_GROUPS, NSA_HEAD_DIM, NSA_Q_PER_GROUP
    q_spec = pl.BlockSpec((1, tq, r * hd), lambda bi, gi, qi: (bi, qi, gi))

    def kv_spec(col_block):
        return pl.BlockSpec((1, s, hd), lambda bi, gi, qi: (bi, 0, col_block + gi))

    tab_spec = pl.BlockSpec((1, s, LANES), lambda bi, gi, qi: (bi, 0, 0))
    return pl.pallas_call(
        functools.partial(_sel_win_kernel, tq=tq, tk=tk),
        grid=(b, g, s // tq),
        in_specs=[q_spec, kv_spec(sel_col_block0), kv_spec(sel_col_block0 + g),
                  kv_spec(win_col_block0), kv_spec(win_col_block0 + g),
                  pl.BlockSpec((1, 1, tq, LANES), lambda bi, gi, qi: (bi, gi, qi, 0)),
                  pl.BlockSpec((LANES, s), lambda bi, gi, qi: (0, 0)),
                  pl.BlockSpec((1, tq, LANES), lambda bi, gi, qi: (bi, qi, gate_col_block)),
                  pl.BlockSpec((1, LANES), lambda bi, gi, qi: (0, 0)),
                  tab_spec, tab_spec, tab_spec],
        out_specs=q_spec,
        out_shape=jax.ShapeDtypeStruct((b, s, g * r * hd), F32),
        scratch_shapes=[pltpu.VMEM((s, hd), BF16), pltpu.VMEM((s, 2 * hd), BF16)] * 2
        + [pltpu.VMEM((r * tq, hd), BF16)],
        compiler_params=_params(("parallel", "parallel", "arbitrary"), 56),
        name="nsa_sel_win_attn",
    )(proj, proj, proj, proj, proj, sel_mask, expand_bf, proj, gate_b_lane, *tables)


def _pad_cols(w, mult):
    n = w.shape[-1]
    return jnp.pad(w, ((0, 0), (0, (-n) % mult)))


def _lane_row(v):
    v = v.reshape(-1)
    return jnp.pad(v, (0, LANES - v.shape[0])).reshape(1, LANES)


def _nsa_mixer(x2d, b, s, norm_g, tables, w_in, gate_b, cmp_pe, cmp_w1, cmp_w2, w_out):
    g, hd, h = NSA_KV_GROUPS, NSA_HEAD_DIM, NSA_HEADS
    proj = _norm_matmul(x2d, norm_g, _pad_cols(w_in, 512).astype(BF16))
    proj = proj.reshape(b, s, -1)
    q_blocks = h
    cmp0, sel0, win0 = q_blocks, q_blocks + 2 * g, q_blocks + 4 * g
    gate_blk = q_blocks + 6 * g
    gate_b_lane = _lane_row(gate_b)
    kv_cmp = _cmp_mlp(proj, cmp_pe, cmp_w1.astype(BF16), cmp_w2.astype(BF16), cmp0)

    n_cmp_pad = s // CMP_STRIDE
    n_slc = s // SLC_BLOCK
    jj = np.arange(n_cmp_pad)[:, None]
    ss = np.arange(LANES)[None, :]
    lo = np.maximum(jj * CMP_STRIDE, ss * SLC_BLOCK)
    hi = np.minimum(jj * CMP_STRIDE + CMP_BLOCK, ss * SLC_BLOCK + SLC_BLOCK)
    overlap = np.clip(hi - lo, 0, None).astype(np.float32) / CMP_BLOCK
    overlap[(s - CMP_BLOCK) // CMP_STRIDE + 1:, :] = 0.0
    overlap[:, n_slc:] = 0.0
    expand = (np.arange(s)[None, :] // SLC_BLOCK == np.arange(LANES)[:, None]).astype(np.float32)

    o_cmp, sel_mask = _cmp_attn(proj, kv_cmp, gate_b_lane, jnp.asarray(overlap, BF16), gate_blk)
    o_rot = _sel_win_attn(proj, sel_mask, jnp.asarray(expand, BF16), gate_b_lane, tables, sel0, win0, gate_blk)
    t = b * s
    return _out_proj([o_cmp.reshape(t, -1), o_rot.reshape(t, -1)], w_out.astype(BF16), x2d)


def _mlstm_gate_kernel(gl_ref, gb_ref, tri_ref, li_ref, f_ref):
    pre = gl_ref[0] + gb_ref[...]
    capped = GATE_SOFTCAP * jnp.tanh(pre / GATE_SOFTCAP)
    li_ref[0] = capped
    lf = jax.nn.log_sigmoid(capped)
    rows = tri_ref.shape[0]
    tri = tri_ref[...]
    for sl in range(lf.shape[0] // rows):
        hi, mid, lo = _split3(lf[sl * rows:(sl + 1) * rows, :])
        f_ref[0, pl.ds(sl * rows, rows), :] = _dot(tri, hi) + _dot(tri, mid) + _dot(tri, lo)


def _mlstm_gates(proj, gate_b_lane, gate_col_block, slab=256):
    b, s, _ = proj.shape
    idx = np.arange(slab)
    tri = ((idx[:, None] >= idx[None, :]) &
           (idx[:, None] // MLSTM_CHUNK == idx[None, :] // MLSTM_CHUNK)).astype(np.float32)
    spec = pl.BlockSpec((1, s, LANES), lambda bi: (bi, 0, 0))
    shape = jax.ShapeDtypeStruct((b, s, LANES), F32)
    return pl.pallas_call(
        _mlstm_gate_kernel,
        grid=(b,),
        in_specs=[pl.BlockSpec((1, s, LANES), lambda bi: (bi, 0, gate_col_block)),
                  pl.BlockSpec((1, LANES), lambda bi: (0, 0)),
                  pl.BlockSpec((slab, slab), lambda bi: (0, 0))],
        out_specs=[spec, spec],
        out_shape=[shape, shape],
        compiler_params=_params(("parallel",), 32),
        name="mlstm_gates",
    )(proj, gate_b_lane, jnp.asarray(tri, BF16))


def _mlstm_kernel(q_ref, k_ref, v_ref, og_ref, li_ref, f_ref, hn_ref, o_ref, state_ref):
    hp = pl.program_id(1)
    L, dk, dv = MLSTM_CHUNK, MLSTM_QK_DIM, MLSTM_V_DIM
    s = q_ref.shape[1]
    state_ref[...] = jnp.zeros_like(state_ref)
    lane = lax.broadcasted_iota(jnp.int32, (L, LANES), 1)
    causal = lax.broadcasted_iota(jnp.int32, (L, L), 1) <= lax.broadcasted_iota(jnp.int32, (L, L), 0)
    ones_col = (lane < 3).astype(F32)
    ones_row = ((lane >= 3) & (lane < 6)).astype(F32)

    def chunk(j, rows, m):
        h = hp * MLSTM_HEADS_PER_STEP + j
        qb = (q_ref[0, rows, j * dk:(j + 1) * dk] * dk ** -0.5).astype(BF16)
        k = k_ref[0, rows, j * dk:(j + 1) * dk]
        vb = v_ref[0, rows, j * dv:(j + 1) * dv].astype(BF16)
        li = _lane_col(li_ref[0, rows, :], h)
        fc = _lane_col(f_ref[0, rows, :], MLSTM_HEADS + h)
        f_end = fc[L - 1:L, :]
        a = f_end - fc + li
        a_max = jnp.max(a, axis=0, keepdims=True)
        w = jnp.exp(a - a_max)

        fh, fm, fl = _split3(fc)
        gh, gm, gl = _split3(li - fc)
        lhs = jnp.where(lane == 0, fh.astype(F32), jnp.where(lane == 1, fm.astype(F32),
              jnp.where(lane == 2, fl.astype(F32), ones_row)))
        rhs = jnp.where(lane == 3, gh.astype(F32), jnp.where(lane == 4, gm.astype(F32),
              jnp.where(lane == 5, gl.astype(F32), ones_col)))
        d = jnp.where(causal, _dot_nt(lhs.astype(BF16), rhs.astype(BF16)), -jnp.inf)

        inter = fc + m
        m_t = jnp.maximum(inter, jnp.max(d, axis=-1, keepdims=True))
        g_inter = jnp.exp(inter - m_t)
        sqk = _dot_nt(qb, k.astype(BF16)) * jnp.exp(d - m_t)
        state = state_ref[j]
        from_state = _dot(qb, state.astype(BF16))
        num = _dot(sqk.astype(BF16), vb) + g_inter * from_state[:, :dv]
        den = jnp.sum(sqk, axis=-1, keepdims=True) + g_inter * from_state[:, dv:dv + 1]
        hv = num / jnp.maximum(jnp.abs(den), jnp.exp(-m_t))
        hv = hv * lax.rsqrt(jnp.mean(hv * hv, axis=-1, keepdims=True) + NORM_EPS) * hn_ref[j]
        o_ref[0, rows, j * dv:(j + 1) * dv] = hv * jax.nn.sigmoid(og_ref[0, rows, j * dv:(j + 1) * dv])

        v_ext = jnp.concatenate([vb, jnp.ones((L, LANES), BF16)], axis=1)
        d_state = _dot_tn((w * k).astype(BF16), v_ext)
        m_new = jnp.maximum(f_end + m, a_max)
        state_ref[j] = jnp.exp(f_end + m - m_new) * state + jnp.exp(a_max - m_new) * d_state
        return m_new

    def body(c, ms):
        rows = pl.ds(pl.multiple_of(c * L, L), L)
        return tuple(chunk(j, rows, m) for j, m in enumerate(ms))

    lax.fori_loop(0, s // L, body, (jnp.zeros((1, 1), F32),) * MLSTM_HEADS_PER_STEP, unroll=2)


def _mlstm_core(proj, li, fcum, head_norm):
    b, s, _ = proj.shape
    hh, dk, dv, hps = MLSTM_HEADS, MLSTM_QK_DIM, MLSTM_V_DIM, MLSTM_HEADS_PER_STEP
    k_blk0 = hh // hps
    v_blk0 = 2 * hh * dk // (hps * dv)
    og_blk0 = v_blk0 + hh // hps
    gate_spec = pl.BlockSpec((1, s, LANES), lambda bi, hi: (bi, 0, 0))
    return pl.pallas_call(
        _mlstm_kernel,
        grid=(b, hh // hps),
        in_specs=[pl.BlockSpec((1, s, hps * dk), lambda bi, hi: (bi, 0, hi)),
                  pl.BlockSpec((1, s, hps * dk), lambda bi, hi: (bi, 0, k_blk0 + hi)),
                  pl.BlockSpec((1, s, hps * dv), lambda bi, hi: (bi, 0, v_blk0 + hi)),
                  pl.BlockSpec((1, s, hps * dv), lambda bi, hi: (bi, 0, og_blk0 + hi)),
                  gate_spec, gate_spec,
                  pl.BlockSpec((hps, 1, dv), lambda bi, hi: (hi, 0, 0))],
        out_specs=pl.BlockSpec((1, s, hps * dv), lambda bi, hi: (bi, 0, hi)),
        out_shape=jax.ShapeDtypeStruct((b, s, hh * dv), F32),
        scratch_shapes=[pltpu.VMEM((hps, dk, dv + LANES), F32)],
        compiler_params=_params(("parallel", "parallel"), 48),
        name="mlstm_chunkwise",
    )(proj, proj, proj, proj, li, fcum, head_norm.reshape(hh, 1, dv))


def _mlstm_mixer(x2d, b, s, norm_g, w_in, gate_b, head_norm, w_out):
    hh, dk, dv = MLSTM_HEADS, MLSTM_QK_DIM, MLSTM_V_DIM
    proj = _norm_matmul(x2d, norm_g, _pad_cols(w_in, 512).astype(BF16))
    proj = proj.reshape(b, s, -1)
    gate_blk = (2 * hh * dk + 2 * hh * dv) // LANES
    li, fcum = _mlstm_gates(proj, _lane_row(gate_b), gate_blk)
    hcat = _mlstm_core(proj, li, fcum, head_norm)
    return _out_proj([hcat.reshape(b * s, -1)], w_out.astype(BF16), x2d)


def kernel(x, positions, nsa_w_in, nsa_gate_b, nsa_cmp_pe, nsa_cmp_w1, nsa_cmp_w2, nsa_w_out,
           mlstm_w_in, mlstm_gate_b, mlstm_head_norm, mlstm_w_out,
           norm_mix, norm_ffn, ffn_w_up, ffn_conv_w, ffn_conv_b, ffn_w_down, norm_final):
    b, s, d = x.shape
    depth = norm_mix.shape[0]
    tables = _rope_tables(positions)
    x2d = x.reshape(b * s, d)
    for i in range(depth):
        j = i // 2
        if i % 2 == 0:
            x2d = _nsa_mixer(x2d, b, s, norm_mix[i], tables, nsa_w_in[j], nsa_gate_b[j], nsa_cmp_pe[j],
                             nsa_cmp_w1[j], nsa_cmp_w2[j], nsa_w_out[j])
        else:
            x2d = _mlstm_mixer(x2d, b, s, norm_mix[i], mlstm_w_in[j], mlstm_gate_b[j],
                               mlstm_head_norm[j], mlstm_w_out[j])
        x2d = _ffn(x2d, norm_ffn[i], ffn_w_up[i].astype(BF16), ffn_conv_w[i], ffn_conv_b[i],
                   ffn_w_down[i].astype(BF16), s)
    return _rmsnorm(x2d, norm_final).reshape(b, s, d)
```

```python
import functools

import numpy as np
import jax
import jax.numpy as jnp
from jax import lax
from jax.experimental import pallas as pl
from jax.experimental.pallas import tpu as pltpu

F32 = jnp.float32
BF16 = jnp.bfloat16

LANES = 128
NORM_EPS = 1e-6
NEG_INF = -1e30

NSA_HEADS = 16
NSA_KV_GROUPS = 2
NSA_HEAD_DIM = 128
NSA_Q_PER_GROUP = NSA_HEADS // NSA_KV_GROUPS
CMP_BLOCK = 32
CMP_STRIDE = 16
SLC_BLOCK = 64
N_SELECT = 8
WINDOW = 512
ROPE_THETA = 500000.0
ROPE_DIM = NSA_HEAD_DIM // 4
ROPE_HALF = ROPE_DIM // 2

MLSTM_HEADS = 8
MLSTM_QK_DIM = 128
MLSTM_V_DIM = 256
MLSTM_CHUNK = 64
MLSTM_HEADS_PER_STEP = 1
GATE_SOFTCAP = 15.0

CONV_WIDTH = 3
CONV_HALO = 8
CONV_ROWS = 16

MIB = 1024 * 1024


def _params(semantics, vmem_mib):
    return pltpu.CompilerParams(dimension_semantics=semantics, vmem_limit_bytes=vmem_mib * MIB)


def _dot(a, b):
    return jnp.dot(a, b, preferred_element_type=F32)


def _dot_nt(a, b):
    return lax.dot_general(a, b, (((1,), (1,)), ((), ())), preferred_element_type=F32)


def _dot_tn(a, b):
    return lax.dot_general(a, b, (((0,), (0,)), ((), ())), preferred_element_type=F32)


def _split3(x):
    hi = x.astype(BF16)
    r1 = x - hi.astype(F32)
    mid = r1.astype(BF16)
    lo = (r1 - mid.astype(F32)).astype(BF16)
    return hi, mid, lo


def _rms_rows(x, g):
    ms = jnp.mean(x * x, axis=-1, keepdims=True)
    return x * lax.rsqrt(ms + NORM_EPS) * g


def _lane_col(x, idx):
    lane = lax.broadcasted_iota(jnp.int32, x.shape, 1)
    return jnp.sum(jnp.where(lane == idx, x, 0.0), axis=-1, keepdims=True)


def _norm_matmul_kernel(x_ref, g_ref, w_ref, o_ref, xn_ref):
    @pl.when(pl.program_id(1) == 0)
    def _():
        xn_ref[...] = _rms_rows(x_ref[...], g_ref[...]).astype(BF16)

    o_ref[...] = _dot(xn_ref[...], w_ref[...])


def _norm_matmul(x2d, g, w_bf, tm=1024, tn=512):
    t, d = x2d.shape
    n = w_bf.shape[1]
    return pl.pallas_call(
        _norm_matmul_kernel,
        grid=(t // tm, n // tn),
        in_specs=[pl.BlockSpec((tm, d), lambda i, j: (i, 0)),
                  pl.BlockSpec((1, d), lambda i, j: (0, 0)),
                  pl.BlockSpec((d, tn), lambda i, j: (0, j))],
        out_specs=pl.BlockSpec((tm, tn), lambda i, j: (i, j)),
        out_shape=jax.ShapeDtypeStruct((t, n), F32),
        scratch_shapes=[pltpu.VMEM((tm, d), BF16)],
        compiler_params=_params(("parallel", "arbitrary"), 40),
        name="norm_matmul",
    )(x2d, g.reshape(1, d), w_bf)


def _out_proj_kernel(*refs, n_in):
    a_refs = refs[:n_in]
    w_ref, r_ref, o_ref, a_bf = refs[n_in:]

    @pl.when(pl.program_id(1) == 0)
    def _():
        a = a_refs[0][...]
        for ar in a_refs[1:]:
            a = a + ar[...]
        a_bf[...] = a.astype(BF16)

    o_ref[...] = r_ref[...] + _dot(a_bf[...], w_ref[...])


def _out_proj(a_list, w_bf, resid, tm=1024, tn=512):
    t, k = a_list[0].shape
    n = w_bf.shape[1]
    n_in = len(a_list)
    return pl.pallas_call(
        functools.partial(_out_proj_kernel, n_in=n_in),
        grid=(t // tm, n // tn),
        in_specs=[pl.BlockSpec((tm, k), lambda i, j: (i, 0))] * n_in
        + [pl.BlockSpec((k, tn), lambda i, j: (0, j)),
           pl.BlockSpec((tm, tn), lambda i, j: (i, j))],
        out_specs=pl.BlockSpec((tm, tn), lambda i, j: (i, j)),
        out_shape=jax.ShapeDtypeStruct((t, n), F32),
        scratch_shapes=[pltpu.VMEM((tm, k), BF16)],
        compiler_params=_params(("parallel", "arbitrary"), 56),
        name="out_proj",
    )(*a_list, w_bf, resid)


def _ffn_kernel(x_ref, g_ref, wg_ref, wv_ref, cwg_ref, cwv_ref, cbg_ref, cbv_ref, wd_ref, xr_ref, o_ref,
                xn_ref, h_ref, h_last, ug0, uv0, ug1, uv1, carry_g, carry_v, unperm, *, tm, nf, tiles_per_seq):
    i = pl.program_id(0)
    st = pl.program_id(1)
    u_bufs = ((ug0, uv0), (ug1, uv1))
    n_slab = ug0.shape[0]
    half = CONV_ROWS // 2

    @pl.when(st == 0)
    def _():
        xn_ref[...] = _rms_rows(x_ref[...], g_ref[...]).astype(BF16)

        @pl.when(i == 0)
        def _():
            for ref in (ug0, uv0, ug1, uv1, carry_g, carry_v):
                ref[...] = jnp.zeros_like(ref)

    def conv_glu(bufs, dst):
        for sl in range(n_slab):
            lanes = slice(sl * LANES, (sl + 1) * LANES)

            def weights(cw_ref, cb_ref):
                return [cw_ref[k * half:(k + 1) * half, lanes] for k in range(CONV_WIDTH)] + [cb_ref[:, lanes]]

            wts = (weights(cwg_ref, cbg_ref), weights(cwv_ref, cbv_ref))
            for r0 in range(0, tm, CONV_ROWS):
                base = r0 + CONV_HALO

                def branch(u_ref, w):
                    cur_e = u_ref[sl, pl.ds(base, half, stride=2), :]
                    cur_o = u_ref[sl, pl.ds(base + 1, half, stride=2), :]
                    prev_e = u_ref[sl, pl.ds(base - 1, half, stride=2), :]
                    prev2_e = u_ref[sl, pl.ds(base - 2, half, stride=2), :]
                    even = ((w[3] + w[0] * prev2_e) + w[1] * prev_e) + w[2] * cur_e
                    odd = ((w[3] + w[0] * prev_e) + w[1] * cur_e) + w[2] * cur_o
                    return even, odd

                gates = branch(bufs[0], wts[0])
                vals = branch(bufs[1], wts[1])
                hs = [g * v * (1.0 / (1.0 + jnp.exp(-g))) for g, v in zip(gates, vals)]
                dst[pl.ds(r0, CONV_ROWS), lanes] = jnp.concatenate(hs, axis=0).astype(BF16)

    def up_proj(bufs):
        xn = xn_ref[...]
        seq_start = (i % tiles_per_seq) == 0
        for w_ref, u_ref, carry in ((wg_ref, bufs[0], carry_g), (wv_ref, bufs[1], carry_v)):
            u = _dot(xn, w_ref[...])
            halo = jnp.where(seq_start, 0.0, carry[st])
            for sl in range(n_slab):
                lanes = slice(sl * LANES, (sl + 1) * LANES)
                u_ref[sl, pl.ds(CONV_HALO, tm), :] = u[:, lanes]
                u_ref[sl, pl.ds(0, CONV_HALO), :] = halo[:, lanes]
            carry[st] = u[tm - CONV_HALO:, :]

    for parity in range(2):
        @pl.when((st < nf) & (st % 2 == parity))
        def _():
            conv_glu(u_bufs[1 - parity], h_ref.at[jnp.maximum(st - 1, 0)])
            up_proj(u_bufs[parity])

    def down_proj(order):
        acc = None
        for f in order:
            d = _dot(h_last[...] if f == nf - 1 else h_ref[f], wd_ref[f])
            acc = d if acc is None else acc + d
        for sl in range(acc.shape[1] // LANES):
            lanes = slice(sl * LANES, (sl + 1) * LANES)
            for r0 in range(0, tm, CONV_ROWS):
                unperm[sl, pl.ds(r0, half, stride=2), :] = acc[r0:r0 + half, lanes]
                unperm[sl, pl.ds(r0 + 1, half, stride=2), :] = acc[r0 + half:r0 + CONV_ROWS, lanes]
            o_ref[:, lanes] = xr_ref[:, lanes] + unperm[sl]

    @pl.when(st == nf)
    def _():
        conv_glu(u_bufs[(nf - 1) % 2], h_last)
        mid = (nf - 1) // 2
        down_proj(list(range(mid)) + [nf - 1] + list(range(mid, nf - 1)))

    @pl.when(st > nf)
    def _():
        down_proj(range(nf))


def _ffn(x2d, g, layer, w_up_bf, conv_w, conv_b, w_down_bf, seq, tm=1024, tf=512, tn=256):
    t, d = x2d.shape
    d_ff = w_down_bf.shape[1]
    nf = d_ff // tf
    nn = d // tn
    half = CONV_ROWS // 2
    kern = functools.partial(_ffn_kernel, tm=tm, nf=nf, tiles_per_seq=seq // tm)
    conv_w_rows = jnp.repeat(conv_w, half, axis=0)
    conv_b_rows = jnp.broadcast_to(conv_b.reshape(1, -1), (half, conv_b.shape[-1]))

    def up(st):
        return jnp.minimum(st, nf - 1)

    def prev(st):
        return jnp.clip(st - 1, 0, nf - 1)

    def down(st):
        return jnp.maximum(st - nf, 0)

    u_buf = pltpu.VMEM((tf // LANES, tm + CONV_HALO, LANES), F32)
    return pl.pallas_call(
        kern,
        grid=(t // tm, nf + nn),
        in_specs=[pl.BlockSpec((tm, d), lambda i, st: (i, 0), pipeline_mode=pl.Buffered(1)),
                  pl.BlockSpec((1, d), lambda i, st: (0, 0)),
                  pl.BlockSpec((None, d, tf), lambda i, st: (layer, 0, up(st))),
                  pl.BlockSpec((None, d, tf), lambda i, st: (layer, 0, up(st) + nf)),
                  pl.BlockSpec((CONV_WIDTH * half, tf), lambda i, st: (0, prev(st))),
                  pl.BlockSpec((CONV_WIDTH * half, tf), lambda i, st: (0, prev(st) + nf)),
                  pl.BlockSpec((half, tf), lambda i, st: (0, prev(st))),
                  pl.BlockSpec((half, tf), lambda i, st: (0, prev(st) + nf)),
                  pl.BlockSpec((None, nf, tf, tn), lambda i, st: (layer, 0, 0, down(st))),
                  pl.BlockSpec((tm, tn), lambda i, st: (i, down(st)))],
        out_specs=pl.BlockSpec((tm, tn), lambda i, st: (i, down(st))),
        out_shape=jax.ShapeDtypeStruct((t, d), F32),
        scratch_shapes=[pltpu.VMEM((tm, d), BF16),
                        pltpu.VMEM((nf - 1, tm, tf), BF16),
                        pltpu.VMEM((tm, tf), BF16)]
        + [u_buf] * 4
        + [pltpu.VMEM((nf, CONV_HALO, tf), F32)] * 2
        + [pltpu.VMEM((tn // LANES, tm, LANES), F32)],
        compiler_params=_params(("arbitrary", "arbitrary"), 56),
        name="conv_glu_ffn",
    )(x2d, g.reshape(1, d), w_up_bf, w_up_bf, conv_w_rows, conv_w_rows,
      conv_b_rows, conv_b_rows, w_down_bf.reshape(w_down_bf.shape[0], nf, tf, d), x2d)


def _rmsnorm_kernel(x_ref, g_ref, o_ref):
    o_ref[...] = _rms_rows(x_ref[...], g_ref[...])


def _rmsnorm(x2d, g, tm=512):
    t, d = x2d.shape
    return pl.pallas_call(
        _rmsnorm_kernel,
        grid=(t // tm,),
        in_specs=[pl.BlockSpec((tm, d), lambda i: (i, 0)), pl.BlockSpec((1, d), lambda i: (0, 0))],
        out_specs=pl.BlockSpec((tm, d), lambda i: (i, 0)),
        out_shape=jax.ShapeDtypeStruct((t, d), F32),
        compiler_params=_params(("parallel",), 32),
        name="final_rmsnorm",
    )(x2d, g.reshape(1, d))


def _rope_table_kernel(pos_ref, inv_ref, c_ref, sa_ref, sb_ref):
    ang = pos_ref[0].astype(F32) * inv_ref[...]
    lane = lax.broadcasted_iota(jnp.int32, ang.shape, 1)
    cos = jnp.cos(ang)
    sin = jnp.sin(ang)
    c_ref[0] = jnp.where(lane < ROPE_DIM, cos, 1.0)
    sa_ref[0] = jnp.where((lane >= ROPE_HALF) & (lane < ROPE_DIM), sin, 0.0)
    sb_ref[0] = jnp.where(lane < ROPE_HALF, -sin, 0.0)


def _rope_tables(positions):
    b, s = positions.shape
    inv = jnp.power(ROPE_THETA, -jnp.arange(0, ROPE_DIM, 2, dtype=F32) / ROPE_DIM)
    inv_lane = jnp.concatenate([inv, inv, jnp.zeros((LANES - ROPE_DIM,), F32)]).reshape(1, LANES)
    spec = pl.BlockSpec((1, s, LANES), lambda i: (i, 0, 0))
    shape = jax.ShapeDtypeStruct((b, s, LANES), F32)
    return pl.pallas_call(
        _rope_table_kernel,
        grid=(b,),
        in_specs=[pl.BlockSpec((1, s, 1), lambda i: (i, 0, 0)),
                  pl.BlockSpec((1, LANES), lambda i: (0, 0))],
        out_specs=[spec, spec, spec],
        out_shape=[shape, shape, shape],
        compiler_params=_params(("parallel",), 32),
        name="rope_tables",
    )(positions.reshape(b, s, 1), inv_lane)


def _rope(x, c, sa, sb):
    return x * c + pltpu.roll(x, ROPE_HALF, 1) * sa + pltpu.roll(x, LANES - ROPE_HALF, 1) * sb


def _cmp_mlp_kernel(x_ref, pe_ref, w1_ref, w2_ref, o_ref):
    n = o_ref.shape[-2]
    acc_lo = jnp.zeros((n, NSA_HEAD_DIM), F32)
    acc_hi = jnp.zeros((n, NSA_HEAD_DIM), F32)
    for l in range(CMP_STRIDE):
        xs = x_ref[0, pl.ds(l, n, stride=CMP_STRIDE), :]
        acc_lo += _dot((xs + pe_ref[0, l:l + 1, :]).astype(BF16), w1_ref[0, l])
        acc_hi += _dot((xs + pe_ref[0, CMP_STRIDE + l:CMP_STRIDE + l + 1, :]).astype(BF16),
                       w1_ref[0, CMP_STRIDE + l])
    hid = jax.nn.gelu(acc_lo + pltpu.roll(acc_hi, n - 1, 0))
    o_ref[0, 0, 0] = _dot(hid.astype(BF16), w2_ref[0])


def _cmp_mlp(proj, pe, w1_bf, w2_bf, col_block0):
    b, s, _ = proj.shape
    g, hd = NSA_KV_GROUPS, NSA_HEAD_DIM
    n = s // CMP_STRIDE
    return pl.pallas_call(
        _cmp_mlp_kernel,
        grid=(b, 2, g),
        in_specs=[pl.BlockSpec((1, s, hd), lambda bi, c, gi: (bi, 0, col_block0 + c * g + gi)),
                  pl.BlockSpec((1, CMP_BLOCK, hd), lambda bi, c, gi: (c, 0, 0)),
                  pl.BlockSpec((1, CMP_BLOCK, hd, hd), lambda bi, c, gi: (c, 0, 0, 0)),
                  pl.BlockSpec((1, hd, hd), lambda bi, c, gi: (c, 0, 0))],
        out_specs=pl.BlockSpec((1, 1, 1, n, hd), lambda bi, c, gi: (bi, c, gi, 0, 0)),
        out_shape=jax.ShapeDtypeStruct((b, 2, g, n, hd), F32),
        compiler_params=_params(("parallel", "parallel", "parallel"), 32),
        name="nsa_cmp_mlp",
    )(proj, pe, w1_bf, w2_bf)


def _cmp_attn_kernel(q_ref, kc_ref, vc_ref, gl_ref, gb_ref, ov_ref, o_ref, sel_ref, *, ts, n_slc):
    g = pl.program_id(1)
    si = pl.program_id(2)
    hd = NSA_HEAD_DIM
    scale = hd ** -0.5
    lane = lax.broadcasted_iota(jnp.int32, (ts, LANES), 1)
    t = si * ts + lax.broadcasted_iota(jnp.int32, (ts, LANES), 0)
    cmask = (lane * CMP_STRIDE + (CMP_BLOCK - 1)) <= t
    cmask_f = cmask.astype(F32)
    kc = kc_ref[0, 0, 0].astype(BF16)
    vc = vc_ref[0, 0, 0].astype(BF16)
    gates = jax.nn.sigmoid(gl_ref[0] + gb_ref[...])
    psum = jnp.zeros((ts, LANES), F32)
    for r in range(NSA_Q_PER_GROUP):
        q = q_ref[0, :, r * hd:(r + 1) * hd].astype(BF16)
        s = jnp.where(cmask, _dot_nt(q, kc) * scale, NEG_INF)
        e = jnp.exp(s - jnp.max(s, axis=-1, keepdims=True))
        p = e / jnp.sum(e, axis=-1, keepdims=True) * cmask_f
        psum = psum + p
        o = _dot(p.astype(BF16), vc)
        o_ref[0, :, r * hd:(r + 1) * hd] = o * _lane_col(gates, g * NSA_Q_PER_GROUP + r)

    ov = ov_ref[...]
    hi, mid, lo = _split3(psum)
    imp = _dot(hi, ov) + _dot(mid, ov) + _dot(lo, ov)
    q_blk = t // SLC_BLOCK
    forced = (lane == 0) | (lane == q_blk)
    score = jnp.where(forced, jnp.inf, jnp.where(lane <= q_blk, imp, -jnp.inf))
    live = lane < n_slc
    lane_f = lane.astype(F32)
    taken = jnp.zeros((ts, LANES), jnp.int32)
    for _ in range(min(N_SELECT, n_slc)):
        cand_ok = live & (taken == 0)
        best = jnp.max(jnp.where(cand_ok, score, -jnp.inf), axis=-1, keepdims=True)
        first = jnp.min(jnp.where(cand_ok & (score == best), lane_f, float(LANES)), axis=-1, keepdims=True)
        taken = jnp.where(lane_f == first, 1, taken)
    sel_ref[0, 0] = taken.astype(F32)


def _cmp_attn(proj, kv_cmp, gate_b_lane, overlap_bf, gate_col_block, ts=256):
    b, s, _ = proj.shape
    g, hd, r = NSA_KV_GROUPS, NSA_HEAD_DIM, NSA_Q_PER_GROUP
    n = kv_cmp.shape[-2]
    kern = functools.partial(_cmp_attn_kernel, ts=ts, n_slc=s // SLC_BLOCK)
    return pl.pallas_call(
        kern,
        grid=(b, g, s // ts),
        in_specs=[pl.BlockSpec((1, ts, r * hd), lambda bi, gi, si: (bi, si, gi)),
                  pl.BlockSpec((1, 1, 1, n, hd), lambda bi, gi, si: (bi, 0, gi, 0, 0)),
                  pl.BlockSpec((1, 1, 1, n, hd), lambda bi, gi, si: (bi, 1, gi, 0, 0)),
                  pl.BlockSpec((1, ts, LANES), lambda bi, gi, si: (bi, si, gate_col_block)),
                  pl.BlockSpec((1, LANES), lambda bi, gi, si: (0, 0)),
                  pl.BlockSpec((n, LANES), lambda bi, gi, si: (0, 0))],
        out_specs=[pl.BlockSpec((1, ts, r * hd), lambda bi, gi, si: (bi, si, gi)),
                   pl.BlockSpec((1, 1, ts, LANES), lambda bi, gi, si: (bi, gi, si, 0))],
        out_shape=[jax.ShapeDtypeStruct((b, s, g * r * hd), F32),
                   jax.ShapeDtypeStruct((b, g, s, LANES), F32)],
        compiler_params=_params(("parallel", "parallel", "parallel"), 32),
        name="nsa_cmp_attn_topk",
    )(proj, kv_cmp, kv_cmp, proj, gate_b_lane, overlap_bf)


def _sel_win_kernel(q_ref, ks_ref, vs_ref, kw_ref, vw_ref, sm_ref, ex_ref, gl_ref, gb_ref,
                    c_ref, sa_ref, sb_ref, o_ref, ks_rot, vs_ext, kw_rot, vw_ext, q_st, *, tq, tk):
    g = pl.program_id(1)
    qi = pl.program_id(2)
    hd, nr = NSA_HEAD_DIM, NSA_Q_PER_GROUP
    s_len = ks_rot.shape[0]
    scale = hd ** -0.5

    @pl.when(qi == 0)
    def _():
        c, sa, sb = c_ref[0], sa_ref[0], sb_ref[0]
        ks_rot[...] = _rope(ks_ref[0], c, sa, sb).astype(BF16)
        kw_rot[...] = _rope(kw_ref[0], c, sa, sb).astype(BF16)
        ones = jnp.ones((s_len, hd), BF16)
        vs_ext[...] = jnp.concatenate([vs_ref[0].astype(BF16), ones], axis=1)
        vw_ext[...] = jnp.concatenate([vw_ref[0].astype(BF16), ones], axis=1)

    q0 = pl.multiple_of(qi * tq, tq)
    ropes = (c_ref[0, pl.ds(q0, tq), :], sa_ref[0, pl.ds(q0, tq), :], sb_ref[0, pl.ds(q0, tq), :])
    for r in range(nr):
        q = _rope(q_ref[0, :, r * hd:(r + 1) * hd], *ropes) * scale
        q_st[r * tq:(r + 1) * tq, :] = q.astype(BF16)
    gates = jax.nn.sigmoid(gl_ref[0] + gb_ref[...])

    def attend(k, v_ext, bias):
        n = k.shape[0]
        s = _dot_nt(q_st[...], k).reshape(nr, tq, n) + bias[None]
        p = jnp.exp(s - jnp.max(s, axis=-1, keepdims=True))
        o_ext = _dot(p.reshape(nr * tq, n).astype(BF16), v_ext)
        return o_ext[:, :hd] / o_ext[:, hd:hd + 1]

    wk = WINDOW + tq
    start = pl.multiple_of(jnp.maximum(q0 - WINDOW, 0), tq)
    dist = (q0 - start) + lax.broadcasted_iota(jnp.int32, (tq, wk), 0) - lax.broadcasted_iota(jnp.int32, (tq, wk), 1)
    sm = sm_ref[0, 0].astype(BF16)
    n_chunks = (q0 + tq + tk - 1) // tk
    for n in range(1, s_len // tk + 1):
        @pl.when(n_chunks == n)
        def _():
            nk = n * tk
            bias_w = jnp.where((dist >= 0) & (dist < WINDOW), 0.0, NEG_INF)
            o_win = attend(kw_rot[pl.ds(start, wk), :], vw_ext[pl.ds(start, wk), :], bias_w)
            member = _dot(sm, ex_ref[:, :nk])
            t = q0 + lax.broadcasted_iota(jnp.int32, (tq, nk), 0)
            kpos = lax.broadcasted_iota(jnp.int32, (tq, nk), 1)
            bias_s = jnp.where((member > 0.5) & (kpos <= t), 0.0, NEG_INF)
            o_sel = attend(ks_rot[:nk, :], vs_ext[:nk, :], bias_s)
            for r in range(nr):
                rows = slice(r * tq, (r + 1) * tq)
                o_ref[0, :, r * hd:(r + 1) * hd] = (
                    o_sel[rows, :] * _lane_col(gates, NSA_HEADS + g * nr + r)
                    + o_win[rows, :] * _lane_col(gates, 2 * NSA_HEADS + g * nr + r))


def _sel_win_attn(proj, sel_mask, expand_bf, gate_b_lane, tables, sel_col_block0, win_col_block0,
                  gate_col_block, tq=128, tk=512):
    b, s, _ = proj.shape
    g, hd, r = NSA_KV_GROUPS, NSA_HEAD_DIM, NSA_Q_PER_GROUP
    q_spec = pl.BlockSpec((1, tq, r * hd), lambda bi, gi, qi: (bi, qi, gi))

    def kv_spec(col_block):
        return pl.BlockSpec((1, s, hd), lambda bi, gi, qi: (bi, 0, col_block + gi))

    tab_spec = pl.BlockSpec((1, s, LANES), lambda bi, gi, qi: (bi, 0, 0))
    return pl.pallas_call(
        functools.partial(_sel_win_kernel, tq=tq, tk=tk),
        grid=(b, g, s // tq),
        in_specs=[q_spec, kv_spec(sel_col_block0), kv_spec(sel_col_block0 + g),
                  kv_spec(win_col_block0), kv_spec(win_col_block0 + g),
                  pl.BlockSpec((1, 1, tq, LANES), lambda bi, gi, qi: (bi, gi, qi, 0)),
                  pl.BlockSpec((LANES, s), lambda bi, gi, qi: (0, 0)),
                  pl.BlockSpec((1, tq, LANES), lambda bi, gi, qi: (bi, qi, gate_col_block)),
                  pl.BlockSpec((1, LANES), lambda bi, gi, qi: (0, 0)),
                  tab_spec, tab_spec, tab_spec],
        out_specs=q_spec,
        out_shape=jax.ShapeDtypeStruct((b, s, g * r * hd), F32),
        scratch_shapes=[pltpu.VMEM((s, hd), BF16), pltpu.VMEM((s, 2 * hd), BF16)] * 2
        + [pltpu.VMEM((r * tq, hd), BF16)],
        compiler_params=_params(("parallel", "parallel", "arbitrary"), 56),
        name="nsa_sel_win_attn",
    )(proj, proj, proj, proj, proj, sel_mask, expand_bf, proj, gate_b_lane, *tables)


def _pad_cols(w, mult):
    n = w.shape[-1]
    return jnp.pad(w, ((0, 0), (0, (-n) % mult)))


def _lane_row(v):
    v = v.reshape(-1)
    return jnp.pad(v, (0, LANES - v.shape[0])).reshape(1, LANES)


def _nsa_mixer(x2d, b, s, norm_g, tables, w_in, gate_b, cmp_pe, cmp_w1, cmp_w2, w_out):
    g, hd, h = NSA_KV_GROUPS, NSA_HEAD_DIM, NSA_HEADS
    proj = _norm_matmul(x2d, norm_g, _pad_cols(w_in, 512).astype(BF16))
    proj = proj.reshape(b, s, -1)
    q_blocks = h
    cmp0, sel0, win0 = q_blocks, q_blocks + 2 * g, q_blocks + 4 * g
    gate_blk = q_blocks + 6 * g
    gate_b_lane = _lane_row(gate_b)
    kv_cmp = _cmp_mlp(proj, cmp_pe, cmp_w1.astype(BF16), cmp_w2.astype(BF16), cmp0)

    n_cmp_pad = s // CMP_STRIDE
    n_slc = s // SLC_BLOCK
    jj = np.arange(n_cmp_pad)[:, None]
    ss = np.arange(LANES)[None, :]
    lo = np.maximum(jj * CMP_STRIDE, ss * SLC_BLOCK)
    hi = np.minimum(jj * CMP_STRIDE + CMP_BLOCK, ss * SLC_BLOCK + SLC_BLOCK)
    overlap = np.clip(hi - lo, 0, None).astype(np.float32) / CMP_BLOCK
    overlap[(s - CMP_BLOCK) // CMP_STRIDE + 1:, :] = 0.0
    overlap[:, n_slc:] = 0.0
    expand = (np.arange(s)[None, :] // SLC_BLOCK == np.arange(LANES)[:, None]).astype(np.float32)

    o_cmp, sel_mask = _cmp_attn(proj, kv_cmp, gate_b_lane, jnp.asarray(overlap, BF16), gate_blk)
    o_rot = _sel_win_attn(proj, sel_mask, jnp.asarray(expand, BF16), gate_b_lane, tables, sel0, win0, gate_blk)
    t = b * s
    return _out_proj([o_cmp.reshape(t, -1), o_rot.reshape(t, -1)], w_out.astype(BF16), x2d)


def _mlstm_gate_kernel(gl_ref, gb_ref, tri_ref, li_ref, f_ref):
    pre = gl_ref[0] + gb_ref[...]
    capped = GATE_SOFTCAP * jnp.tanh(pre / GATE_SOFTCAP)
    li_ref[0] = capped
    lf = jax.nn.log_sigmoid(capped)
    rows = tri_ref.shape[0]
    tri = tri_ref[...]
    for sl in range(lf.shape[0] // rows):
        hi, mid, lo = _split3(lf[sl * rows:(sl + 1) * rows, :])
        f_ref[0, pl.ds(sl * rows, rows), :] = _dot(tri, hi) + _dot(tri, mid) + _dot(tri, lo)


def _mlstm_gates(proj, gate_b_lane, gate_col_block, slab=256):
    b, s, _ = proj.shape
    idx = np.arange(slab)
    tri = ((idx[:, None] >= idx[None, :]) &
           (idx[:, None] // MLSTM_CHUNK == idx[None, :] // MLSTM_CHUNK)).astype(np.float32)
    spec = pl.BlockSpec((1, s, LANES), lambda bi: (bi, 0, 0))
    shape = jax.ShapeDtypeStruct((b, s, LANES), F32)
    return pl.pallas_call(
        _mlstm_gate_kernel,
        grid=(b,),
        in_specs=[pl.BlockSpec((1, s, LANES), lambda bi: (bi, 0, gate_col_block)),
                  pl.BlockSpec((1, LANES), lambda bi: (0, 0)),
                  pl.BlockSpec((slab, slab), lambda bi: (0, 0))],
        out_specs=[spec, spec],
        out_shape=[shape, shape],
        compiler_params=_params(("parallel",), 32),
        name="mlstm_gates",
    )(proj, gate_b_lane, jnp.asarray(tri, BF16))


def _mlstm_kernel(q_ref, k_ref, v_ref, og_ref, li_ref, f_ref, hn_ref, o_ref, state_ref):
    hp = pl.program_id(1)
    L, dk, dv = MLSTM_CHUNK, MLSTM_QK_DIM, MLSTM_V_DIM
    s = q_ref.shape[1]
    state_ref[...] = jnp.zeros_like(state_ref)
    lane = lax.broadcasted_iota(jnp.int32, (L, LANES), 1)
    causal = lax.broadcasted_iota(jnp.int32, (L, L), 1) <= lax.broadcasted_iota(jnp.int32, (L, L), 0)
    ones_col = (lane < 3).astype(F32)
    ones_row = ((lane >= 3) & (lane < 6)).astype(F32)

    def chunk(j, rows, m):
        h = hp * MLSTM_HEADS_PER_STEP + j
        qb = (q_ref[0, rows, j * dk:(j + 1) * dk] * dk ** -0.5).astype(BF16)
        k = k_ref[0, rows, j * dk:(j + 1) * dk]
        vb = v_ref[0, rows, j * dv:(j + 1) * dv].astype(BF16)
        li = _lane_col(li_ref[0, rows, :], h)
        fc = _lane_col(f_ref[0, rows, :], MLSTM_HEADS + h)
        f_end = fc[L - 1:L, :]
        a = f_end - fc + li
        a_max = jnp.max(a, axis=0, keepdims=True)
        w = jnp.exp(a - a_max)

        fh, fm, fl = _split3(fc)
        gh, gm, gl = _split3(li - fc)
        lhs = jnp.where(lane == 0, fh.astype(F32), jnp.where(lane == 1, fm.astype(F32),
              jnp.where(lane == 2, fl.astype(F32), ones_row)))
        rhs = jnp.where(lane == 3, gh.astype(F32), jnp.where(lane == 4, gm.astype(F32),
              jnp.where(lane == 5, gl.astype(F32), ones_col)))
        d = jnp.where(causal, _dot_nt(lhs.astype(BF16), rhs.astype(BF16)), -jnp.inf)

        inter = fc + m
        m_t = jnp.maximum(inter, jnp.max(d, axis=-1, keepdims=True))
        g_inter = jnp.exp(inter - m_t)
        sqk = _dot_nt(qb, k.astype(BF16)) * jnp.exp(d - m_t)
        state = state_ref[j]
        from_state = _dot(qb, state.astype(BF16))
        num = _dot(sqk.astype(BF16), vb) + g_inter * from_state[:, :dv]
        den = jnp.sum(sqk, axis=-1, keepdims=True) + g_inter * from_state[:, dv:dv + 1]
        hv = num / jnp.maximum(jnp.abs(den), jnp.exp(-m_t))
        hv = hv * lax.rsqrt(jnp.mean(hv * hv, axis=-1, keepdims=True) + NORM_EPS) * hn_ref[j]
        o_ref[0, rows, j * dv:(j + 1) * dv] = hv * jax.nn.sigmoid(og_ref[0, rows, j * dv:(j + 1) * dv])

        v_ext = jnp.concatenate([vb, jnp.ones((L, LANES), BF16)], axis=1)
        d_state = _dot_tn((w * k).astype(BF16), v_ext)
        m_new = jnp.maximum(f_end + m, a_max)
        state_ref[j] = jnp.exp(f_end + m - m_new) * state + jnp.exp(a_max - m_new) * d_state
        return m_new

    def body(c, ms):
        rows = pl.ds(pl.multiple_of(c * L, L), L)
        return tuple(chunk(j, rows, m) for j, m in enumerate(ms))

    lax.fori_loop(0, s // L, body, (jnp.zeros((1, 1), F32),) * MLSTM_HEADS_PER_STEP,
                  unroll=4 // MLSTM_HEADS_PER_STEP)


def _mlstm_core(proj, li, fcum, head_norm):
    b, s, _ = proj.shape
    hh, dk, dv, hps = MLSTM_HEADS, MLSTM_QK_DIM, MLSTM_V_DIM, MLSTM_HEADS_PER_STEP
    k_blk0 = hh // hps
    v_blk0 = 2 * hh * dk // (hps * dv)
    og_blk0 = v_blk0 + hh // hps
    gate_spec = pl.BlockSpec((1, s, LANES), lambda bi, hi: (bi, 0, 0))
    return pl.pallas_call(
        _mlstm_kernel,
        grid=(b, hh // hps),
        in_specs=[pl.BlockSpec((1, s, hps * dk), lambda bi, hi: (bi, 0, hi)),
                  pl.BlockSpec((1, s, hps * dk), lambda bi, hi: (bi, 0, k_blk0 + hi)),
                  pl.BlockSpec((1, s, hps * dv), lambda bi, hi: (bi, 0, v_blk0 + hi)),
                  pl.BlockSpec((1, s, hps * dv), lambda bi, hi: (bi, 0, og_blk0 + hi)),
                  gate_spec, gate_spec,
                  pl.BlockSpec((hps, 1, dv), lambda bi, hi: (hi, 0, 0))],
        out_specs=pl.BlockSpec((1, s, hps * dv), lambda bi, hi: (bi, 0, hi)),
        out_shape=jax.ShapeDtypeStruct((b, s, hh * dv), F32),
        scratch_shapes=[pltpu.VMEM((hps, dk, dv + LANES), F32)],
        compiler_params=_params(("parallel", "parallel"), 48),
        name="mlstm_chunkwise",
    )(proj, proj, proj, proj, li, fcum, head_norm.reshape(hh, 1, dv))


def _mlstm_mixer(x2d, b, s, norm_g, w_in, gate_b, head_norm, w_out):
    hh, dk, dv = MLSTM_HEADS, MLSTM_QK_DIM, MLSTM_V_DIM
    proj = _norm_matmul(x2d, norm_g, _pad_cols(w_in, 512).astype(BF16))
    proj = proj.reshape(b, s, -1)
    gate_blk = (2 * hh * dk + 2 * hh * dv) // LANES
    li, fcum = _mlstm_gates(proj, _lane_row(gate_b), gate_blk)
    hcat = _mlstm_core(proj, li, fcum, head_norm)
    return _out_proj([hcat.reshape(b * s, -1)], w_out.astype(BF16), x2d)


def kernel(x, positions, nsa_w_in, nsa_gate_b, nsa_cmp_pe, nsa_cmp_w1, nsa_cmp_w2, nsa_w_out,
           mlstm_w_in, mlstm_gate_b, mlstm_head_norm, mlstm_w_out,
           norm_mix, norm_ffn, ffn_w_up, ffn_conv_w, ffn_conv_b, ffn_w_down, norm_final):
    b, s, d = x.shape
    depth = norm_mix.shape[0]
    tables = _rope_tables(positions)
    ffn_up_bf = ffn_w_up.astype(BF16)
    ffn_down_bf = ffn_w_down.astype(BF16)
    x2d = x.reshape(b * s, d)
    for i in range(depth):
        j = i // 2
        if i % 2 == 0:
            x2d = _nsa_mixer(x2d, b, s, norm_mix[i], tables, nsa_w_in[j], nsa_gate_b[j], nsa_cmp_pe[j],
                             nsa_cmp_w1[j], nsa_cmp_w2[j], nsa_w_out[j])
        else:
            x2d = _mlstm_mixer(x2d, b, s, norm_mix[i], mlstm_w_in[j], mlstm_gate_b[j],
                               mlstm_head_norm[j], mlstm_w_out[j])
        x2d = _ffn(x2d, norm_ffn[i], i, ffn_up_bf, ffn_conv_w[i], ffn_conv_b[i], ffn_down_bf, s)
    return _rmsnorm(x2d, norm_final).reshape(b, s, d)
```

```python
import functools

import numpy as np
import jax
import jax.numpy as jnp
from jax import lax
from jax.experimental import pallas as pl
from jax.experimental.pallas import tpu as pltpu

F32 = jnp.float32
BF16 = jnp.bfloat16

LANES = 128
NORM_EPS = 1e-6
NEG_INF = -1e30

NSA_HEADS = 16
NSA_KV_GROUPS = 2
NSA_HEAD_DIM = 128
NSA_Q_PER_GROUP = NSA_HEADS // NSA_KV_GROUPS
CMP_BLOCK = 32
CMP_STRIDE = 16
SLC_BLOCK = 64
N_SELECT = 8
WINDOW = 512
ROPE_THETA = 500000.0
ROPE_DIM = NSA_HEAD_DIM // 4
ROPE_HALF = ROPE_DIM // 2

MLSTM_HEADS = 8
MLSTM_QK_DIM = 128
MLSTM_V_DIM = 256
MLSTM_CHUNK = 64
MLSTM_HEADS_PER_STEP = 1
GATE_SOFTCAP = 15.0

CONV_WIDTH = 3
CONV_HALO = 8
CONV_ROWS = 16

MIB = 1024 * 1024


def _params(semantics, vmem_mib):
    return pltpu.CompilerParams(dimension_semantics=semantics, vmem_limit_bytes=vmem_mib * MIB)


def _dot(a, b):
    return jnp.dot(a, b, preferred_element_type=F32)


def _dot_nt(a, b):
    return lax.dot_general(a, b, (((1,), (1,)), ((), ())), preferred_element_type=F32)


def _dot_tn(a, b):
    return lax.dot_general(a, b, (((0,), (0,)), ((), ())), preferred_element_type=F32)


def _split3(x):
    hi = x.astype(BF16)
    r1 = x - hi.astype(F32)
    mid = r1.astype(BF16)
    lo = (r1 - mid.astype(F32)).astype(BF16)
    return hi, mid, lo


def _rms_rows(x, g):
    ms = jnp.mean(x * x, axis=-1, keepdims=True)
    return x * lax.rsqrt(ms + NORM_EPS) * g


def _lane_col(x, idx):
    lane = lax.broadcasted_iota(jnp.int32, x.shape, 1)
    return jnp.sum(jnp.where(lane == idx, x, 0.0), axis=-1, keepdims=True)


def _norm_matmul_kernel(x_ref, g_ref, w_ref, o_ref, xn_ref):
    @pl.when(pl.program_id(1) == 0)
    def _():
        xn_ref[...] = _rms_rows(x_ref[...], g_ref[...]).astype(BF16)

    o_ref[...] = _dot(xn_ref[...], w_ref[...])


def _norm_matmul(x2d, g, layer, w_bf, tm=1024, tn=512):
    t, d = x2d.shape
    n = w_bf.shape[2]
    return pl.pallas_call(
        _norm_matmul_kernel,
        grid=(t // tm, n // tn),
        in_specs=[pl.BlockSpec((tm, d), lambda i, j: (i, 0)),
                  pl.BlockSpec((1, d), lambda i, j: (0, 0)),
                  pl.BlockSpec((None, d, tn), lambda i, j: (layer, 0, j))],
        out_specs=pl.BlockSpec((tm, tn), lambda i, j: (i, j)),
        out_shape=jax.ShapeDtypeStruct((t, n), F32),
        scratch_shapes=[pltpu.VMEM((tm, d), BF16)],
        compiler_params=_params(("parallel", "arbitrary"), 40),
        name="norm_matmul",
    )(x2d, g.reshape(1, d), w_bf)


def _out_proj_kernel(*refs, n_in):
    a_refs = refs[:n_in]
    w_ref, r_ref, o_ref, a_bf = refs[n_in:]

    @pl.when(pl.program_id(1) == 0)
    def _():
        a = a_refs[0][...]
        for ar in a_refs[1:]:
            a = a + ar[...]
        a_bf[...] = a.astype(BF16)

    o_ref[...] = r_ref[...] + _dot(a_bf[...], w_ref[...])


def _out_proj(a_list, layer, w_bf, resid, tm=1024, tn=512):
    t, k = a_list[0].shape
    n = w_bf.shape[2]
    n_in = len(a_list)
    return pl.pallas_call(
        functools.partial(_out_proj_kernel, n_in=n_in),
        grid=(t // tm, n // tn),
        in_specs=[pl.BlockSpec((tm, k), lambda i, j: (i, 0))] * n_in
        + [pl.BlockSpec((None, k, tn), lambda i, j: (layer, 0, j)),
           pl.BlockSpec((tm, tn), lambda i, j: (i, j))],
        out_specs=pl.BlockSpec((tm, tn), lambda i, j: (i, j)),
        out_shape=jax.ShapeDtypeStruct((t, n), F32),
        scratch_shapes=[pltpu.VMEM((tm, k), BF16)],
        compiler_params=_params(("parallel", "arbitrary"), 56),
        name="out_proj",
    )(*a_list, w_bf, resid)


def _ffn_kernel(x_ref, g_ref, wg_ref, wv_ref, cwg_ref, cwv_ref, cbg_ref, cbv_ref, wd_ref, xr_ref, o_ref,
                xn_ref, h_ref, h_last, ug0, uv0, ug1, uv1, carry_g, carry_v, unperm, *, tm, nf, tiles_per_seq):
    i = pl.program_id(0)
    st = pl.program_id(1)
    u_bufs = ((ug0, uv0), (ug1, uv1))
    n_slab = ug0.shape[0]
    half = CONV_ROWS // 2

    @pl.when(st == 0)
    def _():
        xn_ref[...] = _rms_rows(x_ref[...], g_ref[...]).astype(BF16)

        @pl.when(i == 0)
        def _():
            for ref in (ug0, uv0, ug1, uv1, carry_g, carry_v):
                ref[...] = jnp.zeros_like(ref)

    def conv_glu(bufs, dst):
        for sl in range(n_slab):
            lanes = slice(sl * LANES, (sl + 1) * LANES)

            def weights(cw_ref, cb_ref):
                return [cw_ref[k * half:(k + 1) * half, lanes] for k in range(CONV_WIDTH)] + [cb_ref[:, lanes]]

            wts = (weights(cwg_ref, cbg_ref), weights(cwv_ref, cbv_ref))
            for r0 in range(0, tm, CONV_ROWS):
                base = r0 + CONV_HALO

                def branch(u_ref, w):
                    cur_e = u_ref[sl, pl.ds(base, half, stride=2), :]
                    cur_o = u_ref[sl, pl.ds(base + 1, half, stride=2), :]
                    prev_e = u_ref[sl, pl.ds(base - 1, half, stride=2), :]
                    prev2_e = u_ref[sl, pl.ds(base - 2, half, stride=2), :]
                    even = ((w[3] + w[0] * prev2_e) + w[1] * prev_e) + w[2] * cur_e
                    odd = ((w[3] + w[0] * prev_e) + w[1] * cur_e) + w[2] * cur_o
                    return even, odd

                gates = branch(bufs[0], wts[0])
                vals = branch(bufs[1], wts[1])
                hs = [g * v * (1.0 / (1.0 + jnp.exp(-g))) for g, v in zip(gates, vals)]
                dst[pl.ds(r0, CONV_ROWS), lanes] = jnp.concatenate(hs, axis=0).astype(BF16)

    def up_proj(bufs):
        xn = xn_ref[...]
        seq_start = (i % tiles_per_seq) == 0
        for w_ref, u_ref, carry in ((wg_ref, bufs[0], carry_g), (wv_ref, bufs[1], carry_v)):
            u = _dot(xn, w_ref[...])
            halo = jnp.where(seq_start, 0.0, carry[st])
            for sl in range(n_slab):
                lanes = slice(sl * LANES, (sl + 1) * LANES)
                u_ref[sl, pl.ds(CONV_HALO, tm), :] = u[:, lanes]
                u_ref[sl, pl.ds(0, CONV_HALO), :] = halo[:, lanes]
            carry[st] = u[tm - CONV_HALO:, :]

    for parity in range(2):
        @pl.when((st < nf) & (st % 2 == parity))
        def _():
            conv_glu(u_bufs[1 - parity], h_ref.at[jnp.maximum(st - 1, 0)])
            up_proj(u_bufs[parity])

    def down_proj(order):
        acc = None
        for f in order:
            d = _dot(h_last[...] if f == nf - 1 else h_ref[f], wd_ref[f])
            acc = d if acc is None else acc + d
        for sl in range(acc.shape[1] // LANES):
            lanes = slice(sl * LANES, (sl + 1) * LANES)
            for r0 in range(0, tm, CONV_ROWS):
                unperm[sl, pl.ds(r0, half, stride=2), :] = acc[r0:r0 + half, lanes]
                unperm[sl, pl.ds(r0 + 1, half, stride=2), :] = acc[r0 + half:r0 + CONV_ROWS, lanes]
            o_ref[:, lanes] = xr_ref[:, lanes] + unperm[sl]

    @pl.when(st == nf)
    def _():
        conv_glu(u_bufs[(nf - 1) % 2], h_last)
        mid = (nf - 1) // 2
        down_proj(list(range(mid)) + [nf - 1] + list(range(mid, nf - 1)))

    @pl.when(st > nf)
    def _():
        down_proj(range(nf))


def _ffn(x2d, g, layer, w_up_bf, conv_w, conv_b, w_down_bf, seq, tm=1024, tf=512, tn=256):
    t, d = x2d.shape
    d_ff = w_down_bf.shape[1]
    nf = d_ff // tf
    nn = d // tn
    half = CONV_ROWS // 2
    kern = functools.partial(_ffn_kernel, tm=tm, nf=nf, tiles_per_seq=seq // tm)
    conv_w_rows = jnp.repeat(conv_w, half, axis=0)
    conv_b_rows = jnp.broadcast_to(conv_b.reshape(1, -1), (half, conv_b.shape[-1]))

    def up(st):
        return jnp.minimum(st, nf - 1)

    def prev(st):
        return jnp.clip(st - 1, 0, nf - 1)

    def down(st):
        return jnp.maximum(st - nf, 0)

    u_buf = pltpu.VMEM((tf // LANES, tm + CONV_HALO, LANES), F32)
    return pl.pallas_call(
        kern,
        grid=(t // tm, nf + nn),
        in_specs=[pl.BlockSpec((tm, d), lambda i, st: (i, 0), pipeline_mode=pl.Buffered(1)),
                  pl.BlockSpec((1, d), lambda i, st: (0, 0)),
                  pl.BlockSpec((None, d, tf), lambda i, st: (layer, 0, up(st))),
                  pl.BlockSpec((None, d, tf), lambda i, st: (layer, 0, up(st) + nf)),
                  pl.BlockSpec((CONV_WIDTH * half, tf), lambda i, st: (0, prev(st))),
                  pl.BlockSpec((CONV_WIDTH * half, tf), lambda i, st: (0, prev(st) + nf)),
                  pl.BlockSpec((half, tf), lambda i, st: (0, prev(st))),
                  pl.BlockSpec((half, tf), lambda i, st: (0, prev(st) + nf)),
                  pl.BlockSpec((None, nf, tf, tn), lambda i, st: (layer, 0, 0, down(st))),
                  pl.BlockSpec((tm, tn), lambda i, st: (i, down(st)))],
        out_specs=pl.BlockSpec((tm, tn), lambda i, st: (i, down(st))),
        out_shape=jax.ShapeDtypeStruct((t, d), F32),
        scratch_shapes=[pltpu.VMEM((tm, d), BF16),
                        pltpu.VMEM((nf - 1, tm, tf), BF16),
                        pltpu.VMEM((tm, tf), BF16)]
        + [u_buf] * 4
        + [pltpu.VMEM((nf, CONV_HALO, tf), F32)] * 2
        + [pltpu.VMEM((tn // LANES, tm, LANES), F32)],
        compiler_params=_params(("arbitrary", "arbitrary"), 56),
        name="conv_glu_ffn",
    )(x2d, g.reshape(1, d), w_up_bf, w_up_bf, conv_w_rows, conv_w_rows,
      conv_b_rows, conv_b_rows, w_down_bf.reshape(w_down_bf.shape[0], nf, tf, d), x2d)


def _rmsnorm_kernel(x_ref, g_ref, o_ref):
    o_ref[...] = _rms_rows(x_ref[...], g_ref[...])


def _rmsnorm(x2d, g, tm=512):
    t, d = x2d.shape
    return pl.pallas_call(
        _rmsnorm_kernel,
        grid=(t // tm,),
        in_specs=[pl.BlockSpec((tm, d), lambda i: (i, 0)), pl.BlockSpec((1, d), lambda i: (0, 0))],
        out_specs=pl.BlockSpec((tm, d), lambda i: (i, 0)),
        out_shape=jax.ShapeDtypeStruct((t, d), F32),
        compiler_params=_params(("parallel",), 32),
        name="final_rmsnorm",
    )(x2d, g.reshape(1, d))


def _rope_table_kernel(pos_ref, inv_ref, c_ref, sa_ref, sb_ref):
    ang = pos_ref[0].astype(F32) * inv_ref[...]
    lane = lax.broadcasted_iota(jnp.int32, ang.shape, 1)
    cos = jnp.cos(ang)
    sin = jnp.sin(ang)
    c_ref[0] = jnp.where(lane < ROPE_DIM, cos, 1.0)
    sa_ref[0] = jnp.where((lane >= ROPE_HALF) & (lane < ROPE_DIM), sin, 0.0)
    sb_ref[0] = jnp.where(lane < ROPE_HALF, -sin, 0.0)


def _rope_tables(positions):
    b, s = positions.shape
    inv = jnp.power(ROPE_THETA, -jnp.arange(0, ROPE_DIM, 2, dtype=F32) / ROPE_DIM)
    inv_lane = jnp.concatenate([inv, inv, jnp.zeros((LANES - ROPE_DIM,), F32)]).reshape(1, LANES)
    spec = pl.BlockSpec((1, s, LANES), lambda i: (i, 0, 0))
    shape = jax.ShapeDtypeStruct((b, s, LANES), F32)
    return pl.pallas_call(
        _rope_table_kernel,
        grid=(b,),
        in_specs=[pl.BlockSpec((1, s, 1), lambda i: (i, 0, 0)),
                  pl.BlockSpec((1, LANES), lambda i: (0, 0))],
        out_specs=[spec, spec, spec],
        out_shape=[shape, shape, shape],
        compiler_params=_params(("parallel",), 32),
        name="rope_tables",
    )(positions.reshape(b, s, 1), inv_lane)


def _rope(x, c, sa, sb):
    return x * c + pltpu.roll(x, ROPE_HALF, 1) * sa + pltpu.roll(x, LANES - ROPE_HALF, 1) * sb


def _cmp_mlp_kernel(x_ref, pe_ref, w1_ref, w2_ref, o_ref):
    n = o_ref.shape[-2]
    acc_lo = jnp.zeros((n, NSA_HEAD_DIM), F32)
    acc_hi = jnp.zeros((n, NSA_HEAD_DIM), F32)
    for l in range(CMP_STRIDE):
        xs = x_ref[0, pl.ds(l, n, stride=CMP_STRIDE), :]
        acc_lo += _dot((xs + pe_ref[0, l:l + 1, :]).astype(BF16), w1_ref[0, l])
        acc_hi += _dot((xs + pe_ref[0, CMP_STRIDE + l:CMP_STRIDE + l + 1, :]).astype(BF16),
                       w1_ref[0, CMP_STRIDE + l])
    hid = jax.nn.gelu(acc_lo + pltpu.roll(acc_hi, n - 1, 0))
    o_ref[0, 0, 0] = _dot(hid.astype(BF16), w2_ref[0])


def _cmp_mlp(proj, pe, w1_bf, w2_bf, col_block0):
    b, s, _ = proj.shape
    g, hd = NSA_KV_GROUPS, NSA_HEAD_DIM
    n = s // CMP_STRIDE
    return pl.pallas_call(
        _cmp_mlp_kernel,
        grid=(b, 2, g),
        in_specs=[pl.BlockSpec((1, s, hd), lambda bi, c, gi: (bi, 0, col_block0 + c * g + gi)),
                  pl.BlockSpec((1, CMP_BLOCK, hd), lambda bi, c, gi: (c, 0, 0)),
                  pl.BlockSpec((1, CMP_BLOCK, hd, hd), lambda bi, c, gi: (c, 0, 0, 0)),
                  pl.BlockSpec((1, hd, hd), lambda bi, c, gi: (c, 0, 0))],
        out_specs=pl.BlockSpec((1, 1, 1, n, hd), lambda bi, c, gi: (bi, c, gi, 0, 0)),
        out_shape=jax.ShapeDtypeStruct((b, 2, g, n, hd), F32),
        compiler_params=_params(("parallel", "parallel", "parallel"), 32),
        name="nsa_cmp_mlp",
    )(proj, pe, w1_bf, w2_bf)


def _cmp_attn_kernel(q_ref, kc_ref, vc_ref, gl_ref, gb_ref, ov_ref, o_ref, sel_ref, *, ts, n_slc):
    g = pl.program_id(1)
    si = pl.program_id(2)
    hd = NSA_HEAD_DIM
    scale = hd ** -0.5
    lane = lax.broadcasted_iota(jnp.int32, (ts, LANES), 1)
    t = si * ts + lax.broadcasted_iota(jnp.int32, (ts, LANES), 0)
    cmask = (lane * CMP_STRIDE + (CMP_BLOCK - 1)) <= t
    cmask_f = cmask.astype(F32)
    kc = kc_ref[0, 0, 0].astype(BF16)
    vc = vc_ref[0, 0, 0].astype(BF16)
    gates = jax.nn.sigmoid(gl_ref[0] + gb_ref[...])
    psum = jnp.zeros((ts, LANES), F32)
    for r in range(NSA_Q_PER_GROUP):
        q = q_ref[0, :, r * hd:(r + 1) * hd].astype(BF16)
        s = jnp.where(cmask, _dot_nt(q, kc) * scale, NEG_INF)
        e = jnp.exp(s - jnp.max(s, axis=-1, keepdims=True))
        p = e / jnp.sum(e, axis=-1, keepdims=True) * cmask_f
        psum = psum + p
        o = _dot(p.astype(BF16), vc)
        o_ref[0, :, r * hd:(r + 1) * hd] = o * _lane_col(gates, g * NSA_Q_PER_GROUP + r)

    ov = ov_ref[...]
    hi, mid, lo = _split3(psum)
    imp = _dot(hi, ov) + _dot(mid, ov) + _dot(lo, ov)
    q_blk = t // SLC_BLOCK
    forced = (lane == 0) | (lane == q_blk)
    score = jnp.where(forced, jnp.inf, jnp.where(lane <= q_blk, imp, -jnp.inf))
    live = lane < n_slc
    lane_f = lane.astype(F32)
    taken = jnp.zeros((ts, LANES), jnp.int32)
    for _ in range(min(N_SELECT, n_slc)):
        cand_ok = live & (taken == 0)
        best = jnp.max(jnp.where(cand_ok, score, -jnp.inf), axis=-1, keepdims=True)
        first = jnp.min(jnp.where(cand_ok & (score == best), lane_f, float(LANES)), axis=-1, keepdims=True)
        taken = jnp.where(lane_f == first, 1, taken)
    sel_ref[0, 0] = taken.astype(F32)


def _cmp_attn(proj, kv_cmp, gate_b_lane, overlap_bf, gate_col_block, ts=256):
    b, s, _ = proj.shape
    g, hd, r = NSA_KV_GROUPS, NSA_HEAD_DIM, NSA_Q_PER_GROUP
    n = kv_cmp.shape[-2]
    kern = functools.partial(_cmp_attn_kernel, ts=ts, n_slc=s // SLC_BLOCK)
    return pl.pallas_call(
        kern,
        grid=(b, g, s // ts),
        in_specs=[pl.BlockSpec((1, ts, r * hd), lambda bi, gi, si: (bi, si, gi)),
                  pl.BlockSpec((1, 1, 1, n, hd), lambda bi, gi, si: (bi, 0, gi, 0, 0)),
                  pl.BlockSpec((1, 1, 1, n, hd), lambda bi, gi, si: (bi, 1, gi, 0, 0)),
                  pl.BlockSpec((1, ts, LANES), lambda bi, gi, si: (bi, si, gate_col_block)),
                  pl.BlockSpec((1, LANES), lambda bi, gi, si: (0, 0)),
                  pl.BlockSpec((n, LANES), lambda bi, gi, si: (0, 0))],
        out_specs=[pl.BlockSpec((1, ts, r * hd), lambda bi, gi, si: (bi, si, gi)),
                   pl.BlockSpec((1, 1, ts, LANES), lambda bi, gi, si: (bi, gi, si, 0))],
        out_shape=[jax.ShapeDtypeStruct((b, s, g * r * hd), F32),
                   jax.ShapeDtypeStruct((b, g, s, LANES), F32)],
        compiler_params=_params(("parallel", "parallel", "parallel"), 32),
        name="nsa_cmp_attn_topk",
    )(proj, kv_cmp, kv_cmp, proj, gate_b_lane, overlap_bf)


def _sel_win_kernel(q_ref, ks_ref, vs_ref, kw_ref, vw_ref, oc_ref, sm_ref, ex_ref, gl_ref, gb_ref,
                    c_ref, sa_ref, sb_ref, o_ref, ks_rot, vs_ext, kw_rot, vw_ext, q_st, *, tq, tk):
    g = pl.program_id(1)
    qi = pl.program_id(2)
    hd, nr = NSA_HEAD_DIM, NSA_Q_PER_GROUP
    s_len = ks_rot.shape[0]
    scale = hd ** -0.5

    @pl.when(qi == 0)
    def _():
        c, sa, sb = c_ref[0], sa_ref[0], sb_ref[0]
        ks_rot[...] = _rope(ks_ref[0], c, sa, sb).astype(BF16)
        kw_rot[...] = _rope(kw_ref[0], c, sa, sb).astype(BF16)
        ones = jnp.ones((s_len, hd), BF16)
        vs_ext[...] = jnp.concatenate([vs_ref[0].astype(BF16), ones], axis=1)
        vw_ext[...] = jnp.concatenate([vw_ref[0].astype(BF16), ones], axis=1)

    q0 = pl.multiple_of(qi * tq, tq)
    ropes = (c_ref[0, pl.ds(q0, tq), :], sa_ref[0, pl.ds(q0, tq), :], sb_ref[0, pl.ds(q0, tq), :])
    for r in range(nr):
        q = _rope(q_ref[0, :, r * hd:(r + 1) * hd], *ropes) * scale
        q_st[r * tq:(r + 1) * tq, :] = q.astype(BF16)
    gates = jax.nn.sigmoid(gl_ref[0] + gb_ref[...])

    def attend(k, v_ext, bias):
        n = k.shape[0]
        s = _dot_nt(q_st[...], k).reshape(nr, tq, n) + bias[None]
        p = jnp.exp(s - jnp.max(s, axis=-1, keepdims=True))
        o_ext = _dot(p.reshape(nr * tq, n).astype(BF16), v_ext)
        return o_ext[:, :hd] / o_ext[:, hd:hd + 1]

    wk = WINDOW + tq
    start = pl.multiple_of(jnp.maximum(q0 - WINDOW, 0), tq)
    dist = (q0 - start) + lax.broadcasted_iota(jnp.int32, (tq, wk), 0) - lax.broadcasted_iota(jnp.int32, (tq, wk), 1)
    sm = sm_ref[0, 0].astype(BF16)
    n_chunks = (q0 + tq + tk - 1) // tk
    for n in range(1, s_len // tk + 1):
        @pl.when(n_chunks == n)
        def _():
            nk = n * tk
            bias_w = jnp.where((dist >= 0) & (dist < WINDOW), 0.0, NEG_INF)
            o_win = attend(kw_rot[pl.ds(start, wk), :], vw_ext[pl.ds(start, wk), :], bias_w)
            member = _dot(sm, ex_ref[:, :nk])
            t = q0 + lax.broadcasted_iota(jnp.int32, (tq, nk), 0)
            kpos = lax.broadcasted_iota(jnp.int32, (tq, nk), 1)
            bias_s = jnp.where((member > 0.5) & (kpos <= t), 0.0, NEG_INF)
            o_sel = attend(ks_rot[:nk, :], vs_ext[:nk, :], bias_s)
            for r in range(nr):
                rows = slice(r * tq, (r + 1) * tq)
                total = (oc_ref[0, :, r * hd:(r + 1) * hd]
                         + o_sel[rows, :] * _lane_col(gates, NSA_HEADS + g * nr + r)
                         + o_win[rows, :] * _lane_col(gates, 2 * NSA_HEADS + g * nr + r))
                o_ref[0, :, r * hd:(r + 1) * hd] = total.astype(o_ref.dtype)


def _sel_win_attn(proj, o_cmp, sel_mask, expand_bf, gate_b_lane, tables, sel_col_block0, win_col_block0,
                  gate_col_block, tq=128, tk=512):
    b, s, _ = proj.shape
    g, hd, r = NSA_KV_GROUPS, NSA_HEAD_DIM, NSA_Q_PER_GROUP
    q_spec = pl.BlockSpec((1, tq, r * hd), lambda bi, gi, qi: (bi, qi, gi))

    def kv_spec(col_block):
        return pl.BlockSpec((1, s, hd), lambda bi, gi, qi: (bi, 0, col_block + gi))

    tab_spec = pl.BlockSpec((1, s, LANES), lambda bi, gi, qi: (bi, 0, 0))
    return pl.pallas_call(
        functools.partial(_sel_win_kernel, tq=tq, tk=tk),
        grid=(b, g, s // tq),
        in_specs=[q_spec, kv_spec(sel_col_block0), kv_spec(sel_col_block0 + g),
                  kv_spec(win_col_block0), kv_spec(win_col_block0 + g), q_spec,
                  pl.BlockSpec((1, 1, tq, LANES), lambda bi, gi, qi: (bi, gi, qi, 0)),
                  pl.BlockSpec((LANES, s), lambda bi, gi, qi: (0, 0)),
                  pl.BlockSpec((1, tq, LANES), lambda bi, gi, qi: (bi, qi, gate_col_block)),
                  pl.BlockSpec((1, LANES), lambda bi, gi, qi: (0, 0)),
                  tab_spec, tab_spec, tab_spec],
        out_specs=q_spec,
        out_shape=jax.ShapeDtypeStruct((b, s, g * r * hd), BF16),
        scratch_shapes=[pltpu.VMEM((s, hd), BF16), pltpu.VMEM((s, 2 * hd), BF16)] * 2
        + [pltpu.VMEM((r * tq, hd), BF16)],
        compiler_params=_params(("parallel", "parallel", "arbitrary"), 56),
        name="nsa_sel_win_attn",
    )(proj, proj, proj, proj, proj, o_cmp, sel_mask, expand_bf, proj, gate_b_lane, *tables)


def _pad_cols(w, mult):
    n = w.shape[-1]
    return jnp.pad(w, ((0, 0),) * (w.ndim - 1) + ((0, (-n) % mult),))


def _lane_row(v):
    v = v.reshape(-1)
    return jnp.pad(v, (0, LANES - v.shape[0])).reshape(1, LANES)


def _nsa_mixer(x2d, b, s, norm_g, tables, layer, w_in_bf, gate_b, cmp_pe, cmp_w1, cmp_w2, w_out_bf):
    g, hd, h = NSA_KV_GROUPS, NSA_HEAD_DIM, NSA_HEADS
    proj = _norm_matmul(x2d, norm_g, layer, w_in_bf)
    proj = proj.reshape(b, s, -1)
    q_blocks = h
    cmp0, sel0, win0 = q_blocks, q_blocks + 2 * g, q_blocks + 4 * g
    gate_blk = q_blocks + 6 * g
    gate_b_lane = _lane_row(gate_b)
    kv_cmp = _cmp_mlp(proj, cmp_pe, cmp_w1.astype(BF16), cmp_w2.astype(BF16), cmp0)

    n_cmp_pad = s // CMP_STRIDE
    n_slc = s // SLC_BLOCK
    jj = np.arange(n_cmp_pad)[:, None]
    ss = np.arange(LANES)[None, :]
    lo = np.maximum(jj * CMP_STRIDE, ss * SLC_BLOCK)
    hi = np.minimum(jj * CMP_STRIDE + CMP_BLOCK, ss * SLC_BLOCK + SLC_BLOCK)
    overlap = np.clip(hi - lo, 0, None).astype(np.float32) / CMP_BLOCK
    overlap[(s - CMP_BLOCK) // CMP_STRIDE + 1:, :] = 0.0
    overlap[:, n_slc:] = 0.0
    expand = (np.arange(s)[None, :] // SLC_BLOCK == np.arange(LANES)[:, None]).astype(np.float32)

    o_cmp, sel_mask = _cmp_attn(proj, kv_cmp, gate_b_lane, jnp.asarray(overlap, BF16), gate_blk)
    o_all = _sel_win_attn(proj, o_cmp, sel_mask, jnp.asarray(expand, BF16), gate_b_lane, tables, sel0, win0,
                          gate_blk)
    return _out_proj([o_all.reshape(b * s, -1)], layer, w_out_bf, x2d)


def _mlstm_gate_kernel(gl_ref, gb_ref, tri_ref, li_ref, f_ref):
    pre = gl_ref[0] + gb_ref[...]
    capped = GATE_SOFTCAP * jnp.tanh(pre / GATE_SOFTCAP)
    li_ref[0] = capped
    lf = jax.nn.log_sigmoid(capped)
    rows = tri_ref.shape[0]
    tri = tri_ref[...]
    for sl in range(lf.shape[0] // rows):
        hi, mid, lo = _split3(lf[sl * rows:(sl + 1) * rows, :])
        f_ref[0, pl.ds(sl * rows, rows), :] = _dot(tri, hi) + _dot(tri, mid) + _dot(tri, lo)


def _mlstm_gates(proj, gate_b_lane, gate_col_block, slab=256):
    b, s, _ = proj.shape
    idx = np.arange(slab)
    tri = ((idx[:, None] >= idx[None, :]) &
           (idx[:, None] // MLSTM_CHUNK == idx[None, :] // MLSTM_CHUNK)).astype(np.float32)
    spec = pl.BlockSpec((1, s, LANES), lambda bi: (bi, 0, 0))
    shape = jax.ShapeDtypeStruct((b, s, LANES), F32)
    return pl.pallas_call(
        _mlstm_gate_kernel,
        grid=(b,),
        in_specs=[pl.BlockSpec((1, s, LANES), lambda bi: (bi, 0, gate_col_block)),
                  pl.BlockSpec((1, LANES), lambda bi: (0, 0)),
                  pl.BlockSpec((slab, slab), lambda bi: (0, 0))],
        out_specs=[spec, spec],
        out_shape=[shape, shape],
        compiler_params=_params(("parallel",), 32),
        name="mlstm_gates",
    )(proj, gate_b_lane, jnp.asarray(tri, BF16))


def _mlstm_kernel(q_ref, k_ref, v_ref, og_ref, li_ref, f_ref, hn_ref, o_ref, state_ref):
    hp = pl.program_id(1)
    L, dk, dv = MLSTM_CHUNK, MLSTM_QK_DIM, MLSTM_V_DIM
    s = q_ref.shape[1]
    state_ref[...] = jnp.zeros_like(state_ref)
    lane = lax.broadcasted_iota(jnp.int32, (L, LANES), 1)
    causal = lax.broadcasted_iota(jnp.int32, (L, L), 1) <= lax.broadcasted_iota(jnp.int32, (L, L), 0)
    ones_col = (lane < 3).astype(F32)
    ones_row = ((lane >= 3) & (lane < 6)).astype(F32)

    def chunk(j, rows, m):
        h = hp * MLSTM_HEADS_PER_STEP + j
        qb = (q_ref[0, rows, j * dk:(j + 1) * dk] * dk ** -0.5).astype(BF16)
        k = k_ref[0, rows, j * dk:(j + 1) * dk]
        vb = v_ref[0, rows, j * dv:(j + 1) * dv].astype(BF16)
        li = _lane_col(li_ref[0, rows, :], h)
        fc = _lane_col(f_ref[0, rows, :], MLSTM_HEADS + h)
        f_end = fc[L - 1:L, :]
        a = f_end - fc + li
        a_max = jnp.max(a, axis=0, keepdims=True)
        w = jnp.exp(a - a_max)

        fh, fm, fl = _split3(fc)
        gh, gm, gl = _split3(li - fc)
        lhs = jnp.where(lane == 0, fh.astype(F32), jnp.where(lane == 1, fm.astype(F32),
              jnp.where(lane == 2, fl.astype(F32), ones_row)))
        rhs = jnp.where(lane == 3, gh.astype(F32), jnp.where(lane == 4, gm.astype(F32),
              jnp.where(lane == 5, gl.astype(F32), ones_col)))
        d = jnp.where(causal, _dot_nt(lhs.astype(BF16), rhs.astype(BF16)), -jnp.inf)

        inter = fc + m
        m_t = jnp.maximum(inter, jnp.max(d, axis=-1, keepdims=True))
        g_inter = jnp.exp(inter - m_t)
        sqk = _dot_nt(qb, k.astype(BF16)) * jnp.exp(d - m_t)
        state = state_ref[j]
        from_state = _dot(qb, state.astype(BF16))
        num = _dot(sqk.astype(BF16), vb) + g_inter * from_state[:, :dv]
        den = jnp.sum(sqk, axis=-1, keepdims=True) + g_inter * from_state[:, dv:dv + 1]
        hv = num / jnp.maximum(jnp.abs(den), jnp.exp(-m_t))
        hv = hv * lax.rsqrt(jnp.mean(hv * hv, axis=-1, keepdims=True) + NORM_EPS) * hn_ref[j]
        gated = hv * jax.nn.sigmoid(og_ref[0, rows, j * dv:(j + 1) * dv])
        o_ref[0, rows, j * dv:(j + 1) * dv] = gated.astype(o_ref.dtype)

        v_ext = jnp.concatenate([vb, jnp.ones((L, LANES), BF16)], axis=1)
        d_state = _dot_tn((w * k).astype(BF16), v_ext)
        m_new = jnp.maximum(f_end + m, a_max)
        state_ref[j] = jnp.exp(f_end + m - m_new) * state + jnp.exp(a_max - m_new) * d_state
        return m_new

    def body(c, ms):
        rows = pl.ds(pl.multiple_of(c * L, L), L)
        return tuple(chunk(j, rows, m) for j, m in enumerate(ms))

    lax.fori_loop(0, s // L, body, (jnp.zeros((1, 1), F32),) * MLSTM_HEADS_PER_STEP,
                  unroll=4 // MLSTM_HEADS_PER_STEP)


def _mlstm_core(proj, li, fcum, head_norm):
    b, s, _ = proj.shape
    hh, dk, dv, hps = MLSTM_HEADS, MLSTM_QK_DIM, MLSTM_V_DIM, MLSTM_HEADS_PER_STEP
    k_blk0 = hh // hps
    v_blk0 = 2 * hh * dk // (hps * dv)
    og_blk0 = v_blk0 + hh // hps
    gate_spec = pl.BlockSpec((1, s, LANES), lambda bi, hi: (bi, 0, 0))
    return pl.pallas_call(
        _mlstm_kernel,
        grid=(b, hh // hps),
        in_specs=[pl.BlockSpec((1, s, hps * dk), lambda bi, hi: (bi, 0, hi)),
                  pl.BlockSpec((1, s, hps * dk), lambda bi, hi: (bi, 0, k_blk0 + hi)),
                  pl.BlockSpec((1, s, hps * dv), lambda bi, hi: (bi, 0, v_blk0 + hi)),
                  pl.BlockSpec((1, s, hps * dv), lambda bi, hi: (bi, 0, og_blk0 + hi)),
                  gate_spec, gate_spec,
                  pl.BlockSpec((hps, 1, dv), lambda bi, hi: (hi, 0, 0))],
        out_specs=pl.BlockSpec((1, s, hps * dv), lambda bi, hi: (bi, 0, hi)),
        out_shape=jax.ShapeDtypeStruct((b, s, hh * dv), BF16),
        scratch_shapes=[pltpu.VMEM((hps, dk, dv + LANES), F32)],
        compiler_params=_params(("parallel", "parallel"), 48),
        name="mlstm_chunkwise",
    )(proj, proj, proj, proj, li, fcum, head_norm.reshape(hh, 1, dv))


def _mlstm_mixer(x2d, b, s, norm_g, layer, w_in_bf, gate_b, head_norm, w_out_bf):
    hh, dk, dv = MLSTM_HEADS, MLSTM_QK_DIM, MLSTM_V_DIM
    proj = _norm_matmul(x2d, norm_g, layer, w_in_bf)
    proj = proj.reshape(b, s, -1)
    gate_blk = (2 * hh * dk + 2 * hh * dv) // LANES
    li, fcum = _mlstm_gates(proj, _lane_row(gate_b), gate_blk)
    hcat = _mlstm_core(proj, li, fcum, head_norm)
    return _out_proj([hcat.reshape(b * s, -1)], layer, w_out_bf, x2d)


def kernel(x, positions, nsa_w_in, nsa_gate_b, nsa_cmp_pe, nsa_cmp_w1, nsa_cmp_w2, nsa_w_out,
           mlstm_w_in, mlstm_gate_b, mlstm_head_norm, mlstm_w_out,
           norm_mix, norm_ffn, ffn_w_up, ffn_conv_w, ffn_conv_b, ffn_w_down, norm_final):
    b, s, d = x.shape
    depth = norm_mix.shape[0]
    tables = _rope_tables(positions)
    ffn_up_bf = ffn_w_up.astype(BF16)
    ffn_down_bf = ffn_w_down.astype(BF16)
    nsa_in_bf = _pad_cols(nsa_w_in, 512).astype(BF16)
    nsa_out_bf = nsa_w_out.astype(BF16)
    mlstm_in_bf = _pad_cols(mlstm_w_in, 512).astype(BF16)
    mlstm_out_bf = mlstm_w_out.astype(BF16)
    x2d = x.reshape(b * s, d)
    for i in range(depth):
        j = i // 2
        if i % 2 == 0:
            x2d = _nsa_mixer(x2d, b, s, norm_mix[i], tables, j, nsa_in_bf, nsa_gate_b[j], nsa_cmp_pe[j],
                             nsa_cmp_w1[j], nsa_cmp_w2[j], nsa_out_bf)
        else:
            x2d = _mlstm_mixer(x2d, b, s, norm_mix[i], j, mlstm_in_bf, mlstm_gate_b[j],
                               mlstm_head_norm[j], mlstm_out_bf)
        x2d = _ffn(x2d, norm_ffn[i], i, ffn_up_bf, ffn_conv_w[i], ffn_conv_b[i], ffn_down_bf, s)
    return _rmsnorm(x2d, norm_final).reshape(b, s, d)
```

```python
import functools

import numpy as np
import jax
import jax.numpy as jnp
from jax import lax
from jax.experimental import pallas as pl
from jax.experimental.pallas import tpu as pltpu

F32 = jnp.float32
BF16 = jnp.bfloat16

LANES = 128
NORM_EPS = 1e-6
NEG_INF = -1e30

NSA_HEADS = 16
NSA_KV_GROUPS = 2
NSA_HEAD_DIM = 128
NSA_Q_PER_GROUP = NSA_HEADS // NSA_KV_GROUPS
CMP_BLOCK = 32
CMP_STRIDE = 16
SLC_BLOCK = 64
N_SELECT = 8
WINDOW = 512
ROPE_THETA = 500000.0
ROPE_DIM = NSA_HEAD_DIM // 4
ROPE_HALF = ROPE_DIM // 2

MLSTM_HEADS = 8
MLSTM_QK_DIM = 128
MLSTM_V_DIM = 256
MLSTM_CHUNK = 64
MLSTM_HEADS_PER_STEP = 1
GATE_SOFTCAP = 15.0

CONV_WIDTH = 3
CONV_HALO = 8
CONV_ROWS = 16

MIB = 1024 * 1024


def _params(semantics, vmem_mib):
    return pltpu.CompilerParams(dimension_semantics=semantics, vmem_limit_bytes=vmem_mib * MIB)


def _dot(a, b):
    return jnp.dot(a, b, preferred_element_type=F32)


def _dot_nt(a, b):
    return lax.dot_general(a, b, (((1,), (1,)), ((), ())), preferred_element_type=F32)


def _dot_tn(a, b):
    return lax.dot_general(a, b, (((0,), (0,)), ((), ())), preferred_element_type=F32)


def _split3(x):
    hi = x.astype(BF16)
    r1 = x - hi.astype(F32)
    mid = r1.astype(BF16)
    lo = (r1 - mid.astype(F32)).astype(BF16)
    return hi, mid, lo


def _rms_rows(x, g):
    ms = jnp.mean(x * x, axis=-1, keepdims=True)
    return x * lax.rsqrt(ms + NORM_EPS) * g


def _lane_col(x, idx):
    lane = lax.broadcasted_iota(jnp.int32, x.shape, 1)
    return jnp.sum(jnp.where(lane == idx, x, 0.0), axis=-1, keepdims=True)


def _norm_matmul_kernel(x_ref, g_ref, w_ref, o_ref, xn_ref):
    @pl.when(pl.program_id(1) == 0)
    def _():
        xn_ref[...] = _rms_rows(x_ref[...], g_ref[...]).astype(BF16)

    o_ref[...] = _dot(xn_ref[...], w_ref[...])


def _norm_matmul(x2d, g, layer, w_bf, tm=1024, tn=512):
    t, d = x2d.shape
    n = w_bf.shape[2]
    return pl.pallas_call(
        _norm_matmul_kernel,
        grid=(t // tm, n // tn),
        in_specs=[pl.BlockSpec((tm, d), lambda i, j: (i, 0)),
                  pl.BlockSpec((1, d), lambda i, j: (0, 0)),
                  pl.BlockSpec((None, d, tn), lambda i, j: (layer, 0, j))],
        out_specs=pl.BlockSpec((tm, tn), lambda i, j: (i, j)),
        out_shape=jax.ShapeDtypeStruct((t, n), F32),
        scratch_shapes=[pltpu.VMEM((tm, d), BF16)],
        compiler_params=_params(("parallel", "arbitrary"), 40),
        name="norm_matmul",
    )(x2d, g.reshape(1, d), w_bf)


def _out_proj_kernel(*refs, n_in):
    a_refs = refs[:n_in]
    w_ref, r_ref, o_ref, a_bf = refs[n_in:]

    @pl.when(pl.program_id(1) == 0)
    def _():
        a = a_refs[0][...]
        for ar in a_refs[1:]:
            a = a + ar[...]
        a_bf[...] = a.astype(BF16)

    o_ref[...] = r_ref[...] + _dot(a_bf[...], w_ref[...])


def _out_proj(a_list, layer, w_bf, resid, tm=1024, tn=512):
    t, k = a_list[0].shape
    n = w_bf.shape[2]
    n_in = len(a_list)
    return pl.pallas_call(
        functools.partial(_out_proj_kernel, n_in=n_in),
        grid=(t // tm, n // tn),
        in_specs=[pl.BlockSpec((tm, k), lambda i, j: (i, 0))] * n_in
        + [pl.BlockSpec((None, k, tn), lambda i, j: (layer, 0, j)),
           pl.BlockSpec((tm, tn), lambda i, j: (i, j))],
        out_specs=pl.BlockSpec((tm, tn), lambda i, j: (i, j)),
        out_shape=jax.ShapeDtypeStruct((t, n), F32),
        scratch_shapes=[pltpu.VMEM((tm, k), BF16)],
        compiler_params=_params(("parallel", "arbitrary"), 56),
        name="out_proj",
    )(*a_list, w_bf, resid)


def _ffn_kernel(x_ref, g_ref, wg_ref, wv_ref, cwg_ref, cwv_ref, cbg_ref, cbv_ref, wd_ref, xr_ref, o_ref,
                xn_ref, h_ref, h_last, ug0, uv0, ug1, uv1, carry_g, carry_v, unperm, *, tm, nf, tiles_per_seq):
    i = pl.program_id(0)
    st = pl.program_id(1)
    u_bufs = ((ug0, uv0), (ug1, uv1))
    n_slab = ug0.shape[0]
    half = CONV_ROWS // 2

    @pl.when(st == 0)
    def _():
        xn_ref[...] = _rms_rows(x_ref[...], g_ref[...]).astype(BF16)

        @pl.when(i == 0)
        def _():
            for ref in (ug0, uv0, ug1, uv1, carry_g, carry_v):
                ref[...] = jnp.zeros_like(ref)

    def conv_glu(bufs, dst):
        for sl in range(n_slab):
            lanes = slice(sl * LANES, (sl + 1) * LANES)

            def weights(cw_ref, cb_ref):
                return [cw_ref[k * half:(k + 1) * half, lanes] for k in range(CONV_WIDTH)] + [cb_ref[:, lanes]]

            wts = (weights(cwg_ref, cbg_ref), weights(cwv_ref, cbv_ref))
            for r0 in range(0, tm, CONV_ROWS):
                base = r0 + CONV_HALO

                def branch(u_ref, w):
                    cur_e = u_ref[sl, pl.ds(base, half, stride=2), :]
                    cur_o = u_ref[sl, pl.ds(base + 1, half, stride=2), :]
                    prev_e = u_ref[sl, pl.ds(base - 1, half, stride=2), :]
                    prev2_e = u_ref[sl, pl.ds(base - 2, half, stride=2), :]
                    even = ((w[3] + w[0] * prev2_e) + w[1] * prev_e) + w[2] * cur_e
                    odd = ((w[3] + w[0] * prev_e) + w[1] * cur_e) + w[2] * cur_o
                    return even, odd

                gates = branch(bufs[0], wts[0])
                vals = branch(bufs[1], wts[1])
                hs = [g * v * (1.0 / (1.0 + jnp.exp(-g))) for g, v in zip(gates, vals)]
                dst[pl.ds(r0, CONV_ROWS), lanes] = jnp.concatenate(hs, axis=0).astype(BF16)

    def up_proj(bufs):
        xn = xn_ref[...]
        seq_start = (i % tiles_per_seq) == 0
        for w_ref, u_ref, carry in ((wg_ref, bufs[0], carry_g), (wv_ref, bufs[1], carry_v)):
            u = _dot(xn, w_ref[...])
            halo = jnp.where(seq_start, 0.0, carry[st])
            for sl in range(n_slab):
                lanes = slice(sl * LANES, (sl + 1) * LANES)
                u_ref[sl, pl.ds(CONV_HALO, tm), :] = u[:, lanes]
                u_ref[sl, pl.ds(0, CONV_HALO), :] = halo[:, lanes]
            carry[st] = u[tm - CONV_HALO:, :]

    for parity in range(2):
        @pl.when((st < nf) & (st % 2 == parity))
        def _():
            conv_glu(u_bufs[1 - parity], h_ref.at[jnp.maximum(st - 1, 0)])
            up_proj(u_bufs[parity])

    def down_proj(order):
        acc = None
        for f in order:
            d = _dot(h_last[...] if f == nf - 1 else h_ref[f], wd_ref[f])
            acc = d if acc is None else acc + d
        for sl in range(acc.shape[1] // LANES):
            lanes = slice(sl * LANES, (sl + 1) * LANES)
            for r0 in range(0, tm, CONV_ROWS):
                unperm[sl, pl.ds(r0, half, stride=2), :] = acc[r0:r0 + half, lanes]
                unperm[sl, pl.ds(r0 + 1, half, stride=2), :] = acc[r0 + half:r0 + CONV_ROWS, lanes]
            o_ref[:, lanes] = xr_ref[:, lanes] + unperm[sl]

    @pl.when(st == nf)
    def _():
        conv_glu(u_bufs[(nf - 1) % 2], h_last)
        mid = (nf - 1) // 2
        down_proj(list(range(mid)) + [nf - 1] + list(range(mid, nf - 1)))

    @pl.when(st > nf)
    def _():
        down_proj(range(nf))


def _ffn(x2d, g, layer, w_up_bf, conv_w, conv_b, w_down_bf, seq, tm=1024, tf=512, tn=256):
    t, d = x2d.shape
    d_ff = w_down_bf.shape[1]
    nf = d_ff // tf
    nn = d // tn
    half = CONV_ROWS // 2
    kern = functools.partial(_ffn_kernel, tm=tm, nf=nf, tiles_per_seq=seq // tm)
    conv_w_rows = jnp.repeat(conv_w, half, axis=0)
    conv_b_rows = jnp.broadcast_to(conv_b.reshape(1, -1), (half, conv_b.shape[-1]))

    def up(st):
        return jnp.minimum(st, nf - 1)

    def prev(st):
        return jnp.clip(st - 1, 0, nf - 1)

    def down(st):
        return jnp.maximum(st - nf, 0)

    u_buf = pltpu.VMEM((tf // LANES, tm + CONV_HALO, LANES), F32)
    return pl.pallas_call(
        kern,
        grid=(t // tm, nf + nn),
        in_specs=[pl.BlockSpec((tm, d), lambda i, st: (i, 0), pipeline_mode=pl.Buffered(1)),
                  pl.BlockSpec((1, d), lambda i, st: (0, 0)),
                  pl.BlockSpec((None, d, tf), lambda i, st: (layer, 0, up(st))),
                  pl.BlockSpec((None, d, tf), lambda i, st: (layer, 0, up(st) + nf)),
                  pl.BlockSpec((CONV_WIDTH * half, tf), lambda i, st: (0, prev(st))),
                  pl.BlockSpec((CONV_WIDTH * half, tf), lambda i, st: (0, prev(st) + nf)),
                  pl.BlockSpec((half, tf), lambda i, st: (0, prev(st))),
                  pl.BlockSpec((half, tf), lambda i, st: (0, prev(st) + nf)),
                  pl.BlockSpec((None, nf, tf, tn), lambda i, st: (layer, 0, 0, down(st))),
                  pl.BlockSpec((tm, tn), lambda i, st: (i, down(st)))],
        out_specs=pl.BlockSpec((tm, tn), lambda i, st: (i, down(st))),
        out_shape=jax.ShapeDtypeStruct((t, d), F32),
        scratch_shapes=[pltpu.VMEM((tm, d), BF16),
                        pltpu.VMEM((nf - 1, tm, tf), BF16),
                        pltpu.VMEM((tm, tf), BF16)]
        + [u_buf] * 4
        + [pltpu.VMEM((nf, CONV_HALO, tf), F32)] * 2
        + [pltpu.VMEM((tn // LANES, tm, LANES), F32)],
        compiler_params=_params(("arbitrary", "arbitrary"), 56),
        name="conv_glu_ffn",
    )(x2d, g.reshape(1, d), w_up_bf, w_up_bf, conv_w_rows, conv_w_rows,
      conv_b_rows, conv_b_rows, w_down_bf.reshape(w_down_bf.shape[0], nf, tf, d), x2d)


def _rmsnorm_kernel(x_ref, g_ref, o_ref):
    o_ref[...] = _rms_rows(x_ref[...], g_ref[...])


def _rmsnorm(x2d, g, tm=512):
    t, d = x2d.shape
    return pl.pallas_call(
        _rmsnorm_kernel,
        grid=(t // tm,),
        in_specs=[pl.BlockSpec((tm, d), lambda i: (i, 0)), pl.BlockSpec((1, d), lambda i: (0, 0))],
        out_specs=pl.BlockSpec((tm, d), lambda i: (i, 0)),
        out_shape=jax.ShapeDtypeStruct((t, d), F32),
        compiler_params=_params(("parallel",), 32),
        name="final_rmsnorm",
    )(x2d, g.reshape(1, d))


def _rope_table_kernel(pos_ref, inv_ref, c_ref, sa_ref, sb_ref):
    ang = pos_ref[0].astype(F32) * inv_ref[...]
    lane = lax.broadcasted_iota(jnp.int32, ang.shape, 1)
    cos = jnp.cos(ang)
    sin = jnp.sin(ang)
    c_ref[0] = jnp.where(lane < ROPE_DIM, cos, 1.0)
    sa_ref[0] = jnp.where((lane >= ROPE_HALF) & (lane < ROPE_DIM), sin, 0.0)
    sb_ref[0] = jnp.where(lane < ROPE_HALF, -sin, 0.0)


def _rope_tables(positions):
    b, s = positions.shape
    inv = jnp.power(ROPE_THETA, -jnp.arange(0, ROPE_DIM, 2, dtype=F32) / ROPE_DIM)
    inv_lane = jnp.concatenate([inv, inv, jnp.zeros((LANES - ROPE_DIM,), F32)]).reshape(1, LANES)
    spec = pl.BlockSpec((1, s, LANES), lambda i: (i, 0, 0))
    shape = jax.ShapeDtypeStruct((b, s, LANES), F32)
    return pl.pallas_call(
        _rope_table_kernel,
        grid=(b,),
        in_specs=[pl.BlockSpec((1, s, 1), lambda i: (i, 0, 0)),
                  pl.BlockSpec((1, LANES), lambda i: (0, 0))],
        out_specs=[spec, spec, spec],
        out_shape=[shape, shape, shape],
        compiler_params=_params(("parallel",), 32),
        name="rope_tables",
    )(positions.reshape(b, s, 1), inv_lane)


def _rope(x, c, sa, sb):
    return x * c + pltpu.roll(x, ROPE_HALF, 1) * sa + pltpu.roll(x, LANES - ROPE_HALF, 1) * sb


def _cmp_mlp_kernel(x_ref, pe_ref, w1_ref, w2_ref, o_ref):
    n = o_ref.shape[-2]
    acc_lo = jnp.zeros((n, NSA_HEAD_DIM), F32)
    acc_hi = jnp.zeros((n, NSA_HEAD_DIM), F32)
    for l in range(CMP_STRIDE):
        xs = x_ref[0, pl.ds(l, n, stride=CMP_STRIDE), :]
        acc_lo += _dot((xs + pe_ref[0, l:l + 1, :]).astype(BF16), w1_ref[0, l])
        acc_hi += _dot((xs + pe_ref[0, CMP_STRIDE + l:CMP_STRIDE + l + 1, :]).astype(BF16),
                       w1_ref[0, CMP_STRIDE + l])
    hid = jax.nn.gelu(acc_lo + pltpu.roll(acc_hi, n - 1, 0))
    o_ref[0, 0, 0] = _dot(hid.astype(BF16), w2_ref[0])


def _cmp_mlp(proj, pe, w1_bf, w2_bf, col_block0):
    b, s, _ = proj.shape
    g, hd = NSA_KV_GROUPS, NSA_HEAD_DIM
    n = s // CMP_STRIDE
    return pl.pallas_call(
        _cmp_mlp_kernel,
        grid=(b, 2, g),
        in_specs=[pl.BlockSpec((1, s, hd), lambda bi, c, gi: (bi, 0, col_block0 + c * g + gi)),
                  pl.BlockSpec((1, CMP_BLOCK, hd), lambda bi, c, gi: (c, 0, 0)),
                  pl.BlockSpec((1, CMP_BLOCK, hd, hd), lambda bi, c, gi: (c, 0, 0, 0)),
                  pl.BlockSpec((1, hd, hd), lambda bi, c, gi: (c, 0, 0))],
        out_specs=pl.BlockSpec((1, 1, 1, n, hd), lambda bi, c, gi: (bi, c, gi, 0, 0)),
        out_shape=jax.ShapeDtypeStruct((b, 2, g, n, hd), F32),
        compiler_params=_params(("parallel", "parallel", "parallel"), 32),
        name="nsa_cmp_mlp",
    )(proj, pe, w1_bf, w2_bf)


def _cmp_attn_kernel(q_ref, kc_ref, vc_ref, gl_ref, gb_ref, ovt_ref, o_ref, sel_ref, *, ts, n_slc):
    g = pl.program_id(1)
    si = pl.program_id(2)
    hd = NSA_HEAD_DIM
    scale = hd ** -0.5
    lane = lax.broadcasted_iota(jnp.int32, (ts, LANES), 1)
    t = si * ts + lax.broadcasted_iota(jnp.int32, (ts, LANES), 0)
    cmask = (lane * CMP_STRIDE + (CMP_BLOCK - 1)) <= t
    cmask_f = cmask.astype(F32)
    kc = kc_ref[0, 0, 0].astype(BF16)
    vc = vc_ref[0, 0, 0].astype(BF16)
    gates = jax.nn.sigmoid(gl_ref[0] + gb_ref[...])
    psum = jnp.zeros((ts, LANES), F32)
    for r in range(NSA_Q_PER_GROUP):
        q = q_ref[0, :, r * hd:(r + 1) * hd].astype(BF16)
        s = jnp.where(cmask, _dot_nt(q, kc) * scale, NEG_INF)
        e = jnp.exp(s - jnp.max(s, axis=-1, keepdims=True))
        p = e / jnp.sum(e, axis=-1, keepdims=True) * cmask_f
        psum = psum + p
        o = _dot(p.astype(BF16), vc)
        o_ref[0, :, r * hd:(r + 1) * hd] = o * _lane_col(gates, g * NSA_Q_PER_GROUP + r)

    ovt = ovt_ref[...]
    hi, mid, lo = _split3(psum)
    imp_t = (_dot_nt(ovt, hi) + _dot_nt(ovt, mid) + _dot_nt(ovt, lo))[:n_slc, :]
    blk = lax.broadcasted_iota(jnp.int32, (n_slc, ts), 0)
    q_blk = (si * ts + lax.broadcasted_iota(jnp.int32, (n_slc, ts), 1)) // SLC_BLOCK
    forced = (blk == 0) | (blk == q_blk)
    score = jnp.where(forced, jnp.inf, jnp.where(blk <= q_blk, imp_t, -jnp.inf))
    blk_f = blk.astype(F32)
    taken = jnp.zeros((n_slc, ts), jnp.int32)
    for _ in range(min(N_SELECT, n_slc)):
        free = taken == 0
        best = jnp.max(jnp.where(free, score, -jnp.inf), axis=0, keepdims=True)
        first = jnp.min(jnp.where(free & (score == best), blk_f, float(n_slc)), axis=0, keepdims=True)
        taken = jnp.where(blk_f == first, 1, taken)
    sel_t = jnp.concatenate([taken.astype(F32), jnp.zeros((LANES - n_slc, ts), F32)], axis=0)
    sel_ref[0, 0] = sel_t.T


def _cmp_attn(proj, kv_cmp, gate_b_lane, overlap_t_bf, gate_col_block, ts=256):
    b, s, _ = proj.shape
    g, hd, r = NSA_KV_GROUPS, NSA_HEAD_DIM, NSA_Q_PER_GROUP
    n = kv_cmp.shape[-2]
    kern = functools.partial(_cmp_attn_kernel, ts=ts, n_slc=s // SLC_BLOCK)
    return pl.pallas_call(
        kern,
        grid=(b, g, s // ts),
        in_specs=[pl.BlockSpec((1, ts, r * hd), lambda bi, gi, si: (bi, si, gi)),
                  pl.BlockSpec((1, 1, 1, n, hd), lambda bi, gi, si: (bi, 0, gi, 0, 0)),
                  pl.BlockSpec((1, 1, 1, n, hd), lambda bi, gi, si: (bi, 1, gi, 0, 0)),
                  pl.BlockSpec((1, ts, LANES), lambda bi, gi, si: (bi, si, gate_col_block)),
                  pl.BlockSpec((1, LANES), lambda bi, gi, si: (0, 0)),
                  pl.BlockSpec((LANES, n), lambda bi, gi, si: (0, 0))],
        out_specs=[pl.BlockSpec((1, ts, r * hd), lambda bi, gi, si: (bi, si, gi)),
                   pl.BlockSpec((1, 1, ts, LANES), lambda bi, gi, si: (bi, gi, si, 0))],
        out_shape=[jax.ShapeDtypeStruct((b, s, g * r * hd), F32),
                   jax.ShapeDtypeStruct((b, g, s, LANES), F32)],
        compiler_params=_params(("parallel", "parallel", "parallel"), 32),
        name="nsa_cmp_attn_topk",
    )(proj, kv_cmp, kv_cmp, proj, gate_b_lane, overlap_t_bf)


def _sel_win_kernel(q_ref, ks_ref, vs_ref, kw_ref, vw_ref, oc_ref, sm_ref, ex_ref, gl_ref, gb_ref,
                    c_ref, sa_ref, sb_ref, o_ref, ks_rot, vs_ext, kw_rot, vw_ext, q_st, *, tq, tk):
    g = pl.program_id(1)
    qi = pl.program_id(2)
    hd, nr = NSA_HEAD_DIM, NSA_Q_PER_GROUP
    s_len = ks_rot.shape[0]
    scale = hd ** -0.5

    @pl.when(qi == 0)
    def _():
        c, sa, sb = c_ref[0], sa_ref[0], sb_ref[0]
        ks_rot[...] = _rope(ks_ref[0], c, sa, sb).astype(BF16)
        kw_rot[...] = _rope(kw_ref[0], c, sa, sb).astype(BF16)
        ones = jnp.ones((s_len, hd), BF16)
        vs_ext[...] = jnp.concatenate([vs_ref[0].astype(BF16), ones], axis=1)
        vw_ext[...] = jnp.concatenate([vw_ref[0].astype(BF16), ones], axis=1)

    q0 = pl.multiple_of(qi * tq, tq)
    ropes = (c_ref[0, pl.ds(q0, tq), :], sa_ref[0, pl.ds(q0, tq), :], sb_ref[0, pl.ds(q0, tq), :])
    for r in range(nr):
        q = _rope(q_ref[0, :, r * hd:(r + 1) * hd], *ropes) * scale
        q_st[r * tq:(r + 1) * tq, :] = q.astype(BF16)
    gates = jax.nn.sigmoid(gl_ref[0] + gb_ref[...])

    def attend(k, v_ext, bias):
        n = k.shape[0]
        s = _dot_nt(q_st[...], k).reshape(nr, tq, n) + bias[None]
        p = jnp.exp(s - jnp.max(s, axis=-1, keepdims=True))
        o_ext = _dot(p.reshape(nr * tq, n).astype(BF16), v_ext)
        return o_ext[:, :hd] / o_ext[:, hd:hd + 1]

    wk = WINDOW + tq
    start = pl.multiple_of(jnp.maximum(q0 - WINDOW, 0), tq)
    dist = (q0 - start) + lax.broadcasted_iota(jnp.int32, (tq, wk), 0) - lax.broadcasted_iota(jnp.int32, (tq, wk), 1)
    sm = sm_ref[0, 0].astype(BF16)
    n_chunks = (q0 + tq + tk - 1) // tk
    for n in range(1, s_len // tk + 1):
        @pl.when(n_chunks == n)
        def _():
            nk = n * tk
            bias_w = jnp.where((dist >= 0) & (dist < WINDOW), 0.0, NEG_INF)
            o_win = attend(kw_rot[pl.ds(start, wk), :], vw_ext[pl.ds(start, wk), :], bias_w)
            member = _dot(sm, ex_ref[:, :nk])
            t = q0 + lax.broadcasted_iota(jnp.int32, (tq, nk), 0)
            kpos = lax.broadcasted_iota(jnp.int32, (tq, nk), 1)
            bias_s = jnp.where((member > 0.5) & (kpos <= t), 0.0, NEG_INF)
            o_sel = attend(ks_rot[:nk, :], vs_ext[:nk, :], bias_s)
            for r in range(nr):
                rows = slice(r * tq, (r + 1) * tq)
                total = (oc_ref[0, :, r * hd:(r + 1) * hd]
                         + o_sel[rows, :] * _lane_col(gates, NSA_HEADS + g * nr + r)
                         + o_win[rows, :] * _lane_col(gates, 2 * NSA_HEADS + g * nr + r))
                o_ref[0, :, r * hd:(r + 1) * hd] = total.astype(o_ref.dtype)


def _sel_win_attn(proj, o_cmp, sel_mask, expand_bf, gate_b_lane, tables, sel_col_block0, win_col_block0,
                  gate_col_block, tq=128, tk=512):
    b, s, _ = proj.shape
    g, hd, r = NSA_KV_GROUPS, NSA_HEAD_DIM, NSA_Q_PER_GROUP
    q_spec = pl.BlockSpec((1, tq, r * hd), lambda bi, gi, qi: (bi, qi, gi))

    def kv_spec(col_block):
        return pl.BlockSpec((1, s, hd), lambda bi, gi, qi: (bi, 0, col_block + gi))

    tab_spec = pl.BlockSpec((1, s, LANES), lambda bi, gi, qi: (bi, 0, 0))
    return pl.pallas_call(
        functools.partial(_sel_win_kernel, tq=tq, tk=tk),
        grid=(b, g, s // tq),
        in_specs=[q_spec, kv_spec(sel_col_block0), kv_spec(sel_col_block0 + g),
                  kv_spec(win_col_block0), kv_spec(win_col_block0 + g), q_spec,
                  pl.BlockSpec((1, 1, tq, LANES), lambda bi, gi, qi: (bi, gi, qi, 0)),
                  pl.BlockSpec((LANES, s), lambda bi, gi, qi: (0, 0)),
                  pl.BlockSpec((1, tq, LANES), lambda bi, gi, qi: (bi, qi, gate_col_block)),
                  pl.BlockSpec((1, LANES), lambda bi, gi, qi: (0, 0)),
                  tab_spec, tab_spec, tab_spec],
        out_specs=q_spec,
        out_shape=jax.ShapeDtypeStruct((b, s, g * r * hd), BF16),
        scratch_shapes=[pltpu.VMEM((s, hd), BF16), pltpu.VMEM((s, 2 * hd), BF16)] * 2
        + [pltpu.VMEM((r * tq, hd), BF16)],
        compiler_params=_params(("parallel", "parallel", "arbitrary"), 56),
        name="nsa_sel_win_attn",
    )(proj, proj, proj, proj, proj, o_cmp, sel_mask, expand_bf, proj, gate_b_lane, *tables)


def _pad_cols(w, mult):
    n = w.shape[-1]
    return jnp.pad(w, ((0, 0),) * (w.ndim - 1) + ((0, (-n) % mult),))


def _lane_row(v):
    v = v.reshape(-1)
    return jnp.pad(v, (0, LANES - v.shape[0])).reshape(1, LANES)


def _nsa_mixer(x2d, b, s, norm_g, tables, layer, w_in_bf, gate_b, cmp_pe, cmp_w1, cmp_w2, w_out_bf):
    g, hd, h = NSA_KV_GROUPS, NSA_HEAD_DIM, NSA_HEADS
    proj = _norm_matmul(x2d, norm_g, layer, w_in_bf)
    proj = proj.reshape(b, s, -1)
    q_blocks = h
    cmp0, sel0, win0 = q_blocks, q_blocks + 2 * g, q_blocks + 4 * g
    gate_blk = q_blocks + 6 * g
    gate_b_lane = _lane_row(gate_b)
    kv_cmp = _cmp_mlp(proj, cmp_pe, cmp_w1.astype(BF16), cmp_w2.astype(BF16), cmp0)

    n_cmp_pad = s // CMP_STRIDE
    n_slc = s // SLC_BLOCK
    jj = np.arange(n_cmp_pad)[:, None]
    ss = np.arange(LANES)[None, :]
    lo = np.maximum(jj * CMP_STRIDE, ss * SLC_BLOCK)
    hi = np.minimum(jj * CMP_STRIDE + CMP_BLOCK, ss * SLC_BLOCK + SLC_BLOCK)
    overlap = np.clip(hi - lo, 0, None).astype(np.float32) / CMP_BLOCK
    overlap[(s - CMP_BLOCK) // CMP_STRIDE + 1:, :] = 0.0
    overlap[:, n_slc:] = 0.0
    expand = (np.arange(s)[None, :] // SLC_BLOCK == np.arange(LANES)[:, None]).astype(np.float32)

    o_cmp, sel_mask = _cmp_attn(proj, kv_cmp, gate_b_lane, jnp.asarray(overlap.T, BF16), gate_blk)
    o_all = _sel_win_attn(proj, o_cmp, sel_mask, jnp.asarray(expand, BF16), gate_b_lane, tables, sel0, win0,
                          gate_blk)
    return _out_proj([o_all.reshape(b * s, -1)], layer, w_out_bf, x2d)


def _mlstm_gate_kernel(gl_ref, gb_ref, tri_ref, li_ref, f_ref):
    pre = gl_ref[0] + gb_ref[...]
    capped = GATE_SOFTCAP * jnp.tanh(pre / GATE_SOFTCAP)
    li_ref[0] = capped
    lf = jax.nn.log_sigmoid(capped)
    rows = tri_ref.shape[0]
    tri = tri_ref[...]
    for sl in range(lf.shape[0] // rows):
        hi, mid, lo = _split3(lf[sl * rows:(sl + 1) * rows, :])
        f_ref[0, pl.ds(sl * rows, rows), :] = _dot(tri, hi) + _dot(tri, mid) + _dot(tri, lo)


def _mlstm_gates(proj, gate_b_lane, gate_col_block, slab=256):
    b, s, _ = proj.shape
    idx = np.arange(slab)
    tri = ((idx[:, None] >= idx[None, :]) &
           (idx[:, None] // MLSTM_CHUNK == idx[None, :] // MLSTM_CHUNK)).astype(np.float32)
    spec = pl.BlockSpec((1, s, LANES), lambda bi: (bi, 0, 0))
    shape = jax.ShapeDtypeStruct((b, s, LANES), F32)
    return pl.pallas_call(
        _mlstm_gate_kernel,
        grid=(b,),
        in_specs=[pl.BlockSpec((1, s, LANES), lambda bi: (bi, 0, gate_col_block)),
                  pl.BlockSpec((1, LANES), lambda bi: (0, 0)),
                  pl.BlockSpec((slab, slab), lambda bi: (0, 0))],
        out_specs=[spec, spec],
        out_shape=[shape, shape],
        compiler_params=_params(("parallel",), 32),
        name="mlstm_gates",
    )(proj, gate_b_lane, jnp.asarray(tri, BF16))


def _mlstm_kernel(q_ref, k_ref, v_ref, og_ref, li_ref, f_ref, hn_ref, o_ref, state_ref):
    hp = pl.program_id(1)
    L, dk, dv = MLSTM_CHUNK, MLSTM_QK_DIM, MLSTM_V_DIM
    s = q_ref.shape[1]
    state_ref[...] = jnp.zeros_like(state_ref)
    lane = lax.broadcasted_iota(jnp.int32, (L, LANES), 1)
    causal = lax.broadcasted_iota(jnp.int32, (L, L), 1) <= lax.broadcasted_iota(jnp.int32, (L, L), 0)
    ones_col = (lane < 3).astype(F32)
    ones_row = ((lane >= 3) & (lane < 6)).astype(F32)

    def chunk(j, rows, m):
        h = hp * MLSTM_HEADS_PER_STEP + j
        qb = (q_ref[0, rows, j * dk:(j + 1) * dk] * dk ** -0.5).astype(BF16)
        k = k_ref[0, rows, j * dk:(j + 1) * dk]
        vb = v_ref[0, rows, j * dv:(j + 1) * dv].astype(BF16)
        li = _lane_col(li_ref[0, rows, :], h)
        fc = _lane_col(f_ref[0, rows, :], MLSTM_HEADS + h)
        f_end = fc[L - 1:L, :]
        a = f_end - fc + li
        a_max = jnp.max(a, axis=0, keepdims=True)
        w = jnp.exp(a - a_max)

        fh, fm, fl = _split3(fc)
        gh, gm, gl = _split3(li - fc)
        lhs = jnp.where(lane == 0, fh.astype(F32), jnp.where(lane == 1, fm.astype(F32),
              jnp.where(lane == 2, fl.astype(F32), ones_row)))
        rhs = jnp.where(lane == 3, gh.astype(F32), jnp.where(lane == 4, gm.astype(F32),
              jnp.where(lane == 5, gl.astype(F32), ones_col)))
        d = jnp.where(causal, _dot_nt(lhs.astype(BF16), rhs.astype(BF16)), -jnp.inf)

        inter = fc + m
        m_t = jnp.maximum(inter, jnp.max(d, axis=-1, keepdims=True))
        g_inter = jnp.exp(inter - m_t)
        sqk = _dot_nt(qb, k.astype(BF16)) * jnp.exp(d - m_t)
        state = state_ref[j]
        from_state = _dot(qb, state.astype(BF16))
        num = _dot(sqk.astype(BF16), vb) + g_inter * from_state[:, :dv]
        den = jnp.sum(sqk, axis=-1, keepdims=True) + g_inter * from_state[:, dv:dv + 1]
        hv = num / jnp.maximum(jnp.abs(den), jnp.exp(-m_t))
        hv = hv * lax.rsqrt(jnp.mean(hv * hv, axis=-1, keepdims=True) + NORM_EPS) * hn_ref[j]
        gated = hv * jax.nn.sigmoid(og_ref[0, rows, j * dv:(j + 1) * dv])
        o_ref[0, rows, j * dv:(j + 1) * dv] = gated.astype(o_ref.dtype)

        v_ext = jnp.concatenate([vb, jnp.ones((L, LANES), BF16)], axis=1)
        d_state = _dot_tn((w * k).astype(BF16), v_ext)
        m_new = jnp.maximum(f_end + m, a_max)
        state_ref[j] = jnp.exp(f_end + m - m_new) * state + jnp.exp(a_max - m_new) * d_state
        return m_new

    def body(c, ms):
        rows = pl.ds(pl.multiple_of(c * L, L), L)
        return tuple(chunk(j, rows, m) for j, m in enumerate(ms))

    lax.fori_loop(0, s // L, body, (jnp.zeros((1, 1), F32),) * MLSTM_HEADS_PER_STEP,
                  unroll=4 // MLSTM_HEADS_PER_STEP)


def _mlstm_core(proj, li, fcum, head_norm):
    b, s, _ = proj.shape
    hh, dk, dv, hps = MLSTM_HEADS, MLSTM_QK_DIM, MLSTM_V_DIM, MLSTM_HEADS_PER_STEP
    k_blk0 = hh // hps
    v_blk0 = 2 * hh * dk // (hps * dv)
    og_blk0 = v_blk0 + hh // hps
    gate_spec = pl.BlockSpec((1, s, LANES), lambda bi, hi: (bi, 0, 0))
    return pl.pallas_call(
        _mlstm_kernel,
        grid=(b, hh // hps),
        in_specs=[pl.BlockSpec((1, s, hps * dk), lambda bi, hi: (bi, 0, hi)),
                  pl.BlockSpec((1, s, hps * dk), lambda bi, hi: (bi, 0, k_blk0 + hi)),
                  pl.BlockSpec((1, s, hps * dv), lambda bi, hi: (bi, 0, v_blk0 + hi)),
                  pl.BlockSpec((1, s, hps * dv), lambda bi, hi: (bi, 0, og_blk0 + hi)),
                  gate_spec, gate_spec,
                  pl.BlockSpec((hps, 1, dv), lambda bi, hi: (hi, 0, 0))],
        out_specs=pl.BlockSpec((1, s, hps * dv), lambda bi, hi: (bi, 0, hi)),
        out_shape=jax.ShapeDtypeStruct((b, s, hh * dv), BF16),
        scratch_shapes=[pltpu.VMEM((hps, dk, dv + LANES), F32)],
        compiler_params=_params(("parallel", "parallel"), 48),
        name="mlstm_chunkwise",
    )(proj, proj, proj, proj, li, fcum, head_norm.reshape(hh, 1, dv))


def _mlstm_mixer(x2d, b, s, norm_g, layer, w_in_bf, gate_b, head_norm, w_out_bf):
    hh, dk, dv = MLSTM_HEADS, MLSTM_QK_DIM, MLSTM_V_DIM
    proj = _norm_matmul(x2d, norm_g, layer, w_in_bf)
    proj = proj.reshape(b, s, -1)
    gate_blk = (2 * hh * dk + 2 * hh * dv) // LANES
    li, fcum = _mlstm_gates(proj, _lane_row(gate_b), gate_blk)
    hcat = _mlstm_core(proj, li, fcum, head_norm)
    return _out_proj([hcat.reshape(b * s, -1)], layer, w_out_bf, x2d)


def kernel(x, positions, nsa_w_in, nsa_gate_b, nsa_cmp_pe, nsa_cmp_w1, nsa_cmp_w2, nsa_w_out,
           mlstm_w_in, mlstm_gate_b, mlstm_head_norm, mlstm_w_out,
           norm_mix, norm_ffn, ffn_w_up, ffn_conv_w, ffn_conv_b, ffn_w_down, norm_final):
    b, s, d = x.shape
    depth = norm_mix.shape[0]
    tables = _rope_tables(positions)
    ffn_up_bf = ffn_w_up.astype(BF16)
    ffn_down_bf = ffn_w_down.astype(BF16)
    nsa_in_bf = _pad_cols(nsa_w_in.astype(BF16), 512)
    nsa_out_bf = nsa_w_out.astype(BF16)
    mlstm_in_bf = _pad_cols(mlstm_w_in.astype(BF16), 512)
    mlstm_out_bf = mlstm_w_out.astype(BF16)
    x2d = x.reshape(b * s, d)
    for i in range(depth):
        j = i // 2
        if i % 2 == 0:
            x2d = _nsa_mixer(x2d, b, s, norm_mix[i], tables, j, nsa_in_bf, nsa_gate_b[j], nsa_cmp_pe[j],
                             nsa_cmp_w1[j], nsa_cmp_w2[j], nsa_out_bf)
        else:
            x2d = _mlstm_mixer(x2d, b, s, norm_mix[i], j, mlstm_in_bf, mlstm_gate_b[j],
                               mlstm_head_norm[j], mlstm_out_bf)
        x2d = _ffn(x2d, norm_ffn[i], i, ffn_up_bf, ffn_conv_w[i], ffn_conv_b[i], ffn_down_bf, s)
    return _rmsnorm(x2d, norm_final).reshape(b, s, d)
```

```python
import functools

import numpy as np
import jax
import jax.numpy as jnp
from jax import lax
from jax.experimental import pallas as pl
from jax.experimental.pallas import tpu as pltpu

F32 = jnp.float32
BF16 = jnp.bfloat16

LANES = 128
NORM_EPS = 1e-6
NEG_INF = -1e30

NSA_HEADS = 16
NSA_KV_GROUPS = 2
NSA_HEAD_DIM = 128
NSA_Q_PER_GROUP = NSA_HEADS // NSA_KV_GROUPS
CMP_BLOCK = 32
CMP_STRIDE = 16
SLC_BLOCK = 64
N_SELECT = 8
WINDOW = 512
ROPE_THETA = 500000.0
ROPE_DIM = NSA_HEAD_DIM // 4
ROPE_HALF = ROPE_DIM // 2

MLSTM_HEADS = 8
MLSTM_QK_DIM = 128
MLSTM_V_DIM = 256
MLSTM_CHUNK = 256
MLSTM_HEADS_PER_STEP = 1
GATE_SOFTCAP = 15.0

CONV_WIDTH = 3
CONV_HALO = 8
CONV_ROWS = 16

MIB = 1024 * 1024


def _params(semantics, vmem_mib):
    return pltpu.CompilerParams(dimension_semantics=semantics, vmem_limit_bytes=vmem_mib * MIB)


def _dot(a, b):
    return jnp.dot(a, b, preferred_element_type=F32)


def _dot_nt(a, b):
    return lax.dot_general(a, b, (((1,), (1,)), ((), ())), preferred_element_type=F32)


def _dot_tn(a, b):
    return lax.dot_general(a, b, (((0,), (0,)), ((), ())), preferred_element_type=F32)


def _split3(x):
    hi = x.astype(BF16)
    r1 = x - hi.astype(F32)
    mid = r1.astype(BF16)
    lo = (r1 - mid.astype(F32)).astype(BF16)
    return hi, mid, lo


def _rms_rows(x, g):
    ms = jnp.mean(x * x, axis=-1, keepdims=True)
    return x * lax.rsqrt(ms + NORM_EPS) * g


def _lane_col(x, idx):
    lane = lax.broadcasted_iota(jnp.int32, x.shape, 1)
    return jnp.sum(jnp.where(lane == idx, x, 0.0), axis=-1, keepdims=True)


def _norm_matmul_kernel(x_ref, g_ref, w_ref, o_ref, xn_ref):
    @pl.when(pl.program_id(1) == 0)
    def _():
        xn_ref[...] = _rms_rows(x_ref[...], g_ref[...]).astype(BF16)

    o_ref[...] = _dot(xn_ref[...], w_ref[...])


def _norm_matmul(x2d, g, layer, w_bf, tm=1024, tn=512):
    t, d = x2d.shape
    n = w_bf.shape[2]
    return pl.pallas_call(
        _norm_matmul_kernel,
        grid=(t // tm, n // tn),
        in_specs=[pl.BlockSpec((tm, d), lambda i, j: (i, 0)),
                  pl.BlockSpec((1, d), lambda i, j: (0, 0)),
                  pl.BlockSpec((None, d, tn), lambda i, j: (layer, 0, j))],
        out_specs=pl.BlockSpec((tm, tn), lambda i, j: (i, j)),
        out_shape=jax.ShapeDtypeStruct((t, n), F32),
        scratch_shapes=[pltpu.VMEM((tm, d), BF16)],
        compiler_params=_params(("parallel", "arbitrary"), 40),
        name="norm_matmul",
    )(x2d, g.reshape(1, d), w_bf)


def _out_proj_kernel(*refs, n_in):
    a_refs = refs[:n_in]
    w_ref, r_ref, o_ref, a_bf = refs[n_in:]

    @pl.when(pl.program_id(1) == 0)
    def _():
        a = a_refs[0][...]
        for ar in a_refs[1:]:
            a = a + ar[...]
        a_bf[...] = a.astype(BF16)

    o_ref[...] = r_ref[...] + _dot(a_bf[...], w_ref[...])


def _out_proj(a_list, layer, w_bf, resid, tm=1024, tn=512):
    t, k = a_list[0].shape
    n = w_bf.shape[2]
    n_in = len(a_list)
    return pl.pallas_call(
        functools.partial(_out_proj_kernel, n_in=n_in),
        grid=(t // tm, n // tn),
        in_specs=[pl.BlockSpec((tm, k), lambda i, j: (i, 0))] * n_in
        + [pl.BlockSpec((None, k, tn), lambda i, j: (layer, 0, j)),
           pl.BlockSpec((tm, tn), lambda i, j: (i, j))],
        out_specs=pl.BlockSpec((tm, tn), lambda i, j: (i, j)),
        out_shape=jax.ShapeDtypeStruct((t, n), F32),
        scratch_shapes=[pltpu.VMEM((tm, k), BF16)],
        compiler_params=_params(("parallel", "arbitrary"), 56),
        name="out_proj",
    )(*a_list, w_bf, resid)


def _ffn_kernel(x_ref, g_ref, wg_ref, wv_ref, cwg_ref, cwv_ref, cbg_ref, cbv_ref, wd_ref, xr_ref, o_ref,
                xn_ref, h_ref, h_last, ug0, uv0, ug1, uv1, carry_g, carry_v, unperm, *, tm, nf, tiles_per_seq):
    i = pl.program_id(0)
    st = pl.program_id(1)
    u_bufs = ((ug0, uv0), (ug1, uv1))
    n_slab = ug0.shape[0]
    half = CONV_ROWS // 2

    @pl.when(st == 0)
    def _():
        xn_ref[...] = _rms_rows(x_ref[...], g_ref[...]).astype(BF16)

        @pl.when(i == 0)
        def _():
            for ref in (ug0, uv0, ug1, uv1, carry_g, carry_v):
                ref[...] = jnp.zeros_like(ref)

    def conv_glu(bufs, dst):
        for sl in range(n_slab):
            lanes = slice(sl * LANES, (sl + 1) * LANES)

            def weights(cw_ref, cb_ref):
                return [cw_ref[k * half:(k + 1) * half, lanes] for k in range(CONV_WIDTH)] + [cb_ref[:, lanes]]

            wts = (weights(cwg_ref, cbg_ref), weights(cwv_ref, cbv_ref))
            for r0 in range(0, tm, CONV_ROWS):
                base = r0 + CONV_HALO

                def branch(u_ref, w):
                    cur_e = u_ref[sl, pl.ds(base, half, stride=2), :]
                    cur_o = u_ref[sl, pl.ds(base + 1, half, stride=2), :]
                    prev_e = u_ref[sl, pl.ds(base - 1, half, stride=2), :]
                    prev2_e = u_ref[sl, pl.ds(base - 2, half, stride=2), :]
                    even = ((w[3] + w[0] * prev2_e) + w[1] * prev_e) + w[2] * cur_e
                    odd = ((w[3] + w[0] * prev_e) + w[1] * cur_e) + w[2] * cur_o
                    return even, odd

                gates = branch(bufs[0], wts[0])
                vals = branch(bufs[1], wts[1])
                hs = [g * v * (1.0 / (1.0 + jnp.exp(-g))) for g, v in zip(gates, vals)]
                dst[pl.ds(r0, CONV_ROWS), lanes] = jnp.concatenate(hs, axis=0).astype(BF16)

    def up_proj(bufs):
        xn = xn_ref[...]
        seq_start = (i % tiles_per_seq) == 0
        for w_ref, u_ref, carry in ((wg_ref, bufs[0], carry_g), (wv_ref, bufs[1], carry_v)):
            u = _dot(xn, w_ref[...])
            halo = jnp.where(seq_start, 0.0, carry[st])
            for sl in range(n_slab):
                lanes = slice(sl * LANES, (sl + 1) * LANES)
                u_ref[sl, pl.ds(CONV_HALO, tm), :] = u[:, lanes]
                u_ref[sl, pl.ds(0, CONV_HALO), :] = halo[:, lanes]
            carry[st] = u[tm - CONV_HALO:, :]

    for parity in range(2):
        @pl.when((st < nf) & (st % 2 == parity))
        def _():
            conv_glu(u_bufs[1 - parity], h_ref.at[jnp.maximum(st - 1, 0)])
            up_proj(u_bufs[parity])

    def down_proj(order):
        acc = None
        for f in order:
            d = _dot(h_last[...] if f == nf - 1 else h_ref[f], wd_ref[f])
            acc = d if acc is None else acc + d
        for sl in range(acc.shape[1] // LANES):
            lanes = slice(sl * LANES, (sl + 1) * LANES)
            for r0 in range(0, tm, CONV_ROWS):
                unperm[sl, pl.ds(r0, half, stride=2), :] = acc[r0:r0 + half, lanes]
                unperm[sl, pl.ds(r0 + 1, half, stride=2), :] = acc[r0 + half:r0 + CONV_ROWS, lanes]
            o_ref[:, lanes] = xr_ref[:, lanes] + unperm[sl]

    @pl.when(st == nf)
    def _():
        conv_glu(u_bufs[(nf - 1) % 2], h_last)
        mid = (nf - 1) // 2
        down_proj(list(range(mid)) + [nf - 1] + list(range(mid, nf - 1)))

    @pl.when(st > nf)
    def _():
        down_proj(range(nf))


def _ffn(x2d, g, layer, w_up_bf, conv_w, conv_b, w_down_bf, seq, tm=1024, tf=512, tn=256):
    t, d = x2d.shape
    d_ff = w_down_bf.shape[1]
    nf = d_ff // tf
    nn = d // tn
    half = CONV_ROWS // 2
    kern = functools.partial(_ffn_kernel, tm=tm, nf=nf, tiles_per_seq=seq // tm)
    conv_w_rows = jnp.repeat(conv_w, half, axis=0)
    conv_b_rows = jnp.broadcast_to(conv_b.reshape(1, -1), (half, conv_b.shape[-1]))

    def up(st):
        return jnp.minimum(st, nf - 1)

    def prev(st):
        return jnp.clip(st - 1, 0, nf - 1)

    def down(st):
        return jnp.maximum(st - nf, 0)

    u_buf = pltpu.VMEM((tf // LANES, tm + CONV_HALO, LANES), F32)
    return pl.pallas_call(
        kern,
        grid=(t // tm, nf + nn),
        in_specs=[pl.BlockSpec((tm, d), lambda i, st: (i, 0), pipeline_mode=pl.Buffered(1)),
                  pl.BlockSpec((1, d), lambda i, st: (0, 0)),
                  pl.BlockSpec((None, d, tf), lambda i, st: (layer, 0, up(st))),
                  pl.BlockSpec((None, d, tf), lambda i, st: (layer, 0, up(st) + nf)),
                  pl.BlockSpec((CONV_WIDTH * half, tf), lambda i, st: (0, prev(st))),
                  pl.BlockSpec((CONV_WIDTH * half, tf), lambda i, st: (0, prev(st) + nf)),
                  pl.BlockSpec((half, tf), lambda i, st: (0, prev(st))),
                  pl.BlockSpec((half, tf), lambda i, st: (0, prev(st) + nf)),
                  pl.BlockSpec((None, nf, tf, tn), lambda i, st: (layer, 0, 0, down(st))),
                  pl.BlockSpec((tm, tn), lambda i, st: (i, down(st)))],
        out_specs=pl.BlockSpec((tm, tn), lambda i, st: (i, down(st))),
        out_shape=jax.ShapeDtypeStruct((t, d), F32),
        scratch_shapes=[pltpu.VMEM((tm, d), BF16),
                        pltpu.VMEM((nf - 1, tm, tf), BF16),
                        pltpu.VMEM((tm, tf), BF16)]
        + [u_buf] * 4
        + [pltpu.VMEM((nf, CONV_HALO, tf), F32)] * 2
        + [pltpu.VMEM((tn // LANES, tm, LANES), F32)],
        compiler_params=_params(("arbitrary", "arbitrary"), 56),
        name="conv_glu_ffn",
    )(x2d, g.reshape(1, d), w_up_bf, w_up_bf, conv_w_rows, conv_w_rows,
      conv_b_rows, conv_b_rows, w_down_bf.reshape(w_down_bf.shape[0], nf, tf, d), x2d)


def _rmsnorm_kernel(x_ref, g_ref, o_ref):
    o_ref[...] = _rms_rows(x_ref[...], g_ref[...])


def _rmsnorm(x2d, g, tm=512):
    t, d = x2d.shape
    return pl.pallas_call(
        _rmsnorm_kernel,
        grid=(t // tm,),
        in_specs=[pl.BlockSpec((tm, d), lambda i: (i, 0)), pl.BlockSpec((1, d), lambda i: (0, 0))],
        out_specs=pl.BlockSpec((tm, d), lambda i: (i, 0)),
        out_shape=jax.ShapeDtypeStruct((t, d), F32),
        compiler_params=_params(("parallel",), 32),
        name="final_rmsnorm",
    )(x2d, g.reshape(1, d))


def _rope_table_kernel(pos_ref, inv_ref, c_ref, sa_ref, sb_ref):
    ang = pos_ref[0].astype(F32) * inv_ref[...]
    lane = lax.broadcasted_iota(jnp.int32, ang.shape, 1)
    cos = jnp.cos(ang)
    sin = jnp.sin(ang)
    c_ref[0] = jnp.where(lane < ROPE_DIM, cos, 1.0)
    sa_ref[0] = jnp.where((lane >= ROPE_HALF) & (lane < ROPE_DIM), sin, 0.0)
    sb_ref[0] = jnp.where(lane < ROPE_HALF, -sin, 0.0)


def _rope_tables(positions):
    b, s = positions.shape
    inv = jnp.power(ROPE_THETA, -jnp.arange(0, ROPE_DIM, 2, dtype=F32) / ROPE_DIM)
    inv_lane = jnp.concatenate([inv, inv, jnp.zeros((LANES - ROPE_DIM,), F32)]).reshape(1, LANES)
    spec = pl.BlockSpec((1, s, LANES), lambda i: (i, 0, 0))
    shape = jax.ShapeDtypeStruct((b, s, LANES), F32)
    return pl.pallas_call(
        _rope_table_kernel,
        grid=(b,),
        in_specs=[pl.BlockSpec((1, s, 1), lambda i: (i, 0, 0)),
                  pl.BlockSpec((1, LANES), lambda i: (0, 0))],
        out_specs=[spec, spec, spec],
        out_shape=[shape, shape, shape],
        compiler_params=_params(("parallel",), 32),
        name="rope_tables",
    )(positions.reshape(b, s, 1), inv_lane)


def _rope(x, c, sa, sb):
    return x * c + pltpu.roll(x, ROPE_HALF, 1) * sa + pltpu.roll(x, LANES - ROPE_HALF, 1) * sb


def _cmp_mlp_kernel(x_ref, pe_ref, w1_ref, w2_ref, o_ref):
    n = o_ref.shape[-2]
    acc_lo = jnp.zeros((n, NSA_HEAD_DIM), F32)
    acc_hi = jnp.zeros((n, NSA_HEAD_DIM), F32)
    for l in range(CMP_STRIDE):
        xs = x_ref[0, pl.ds(l, n, stride=CMP_STRIDE), :]
        acc_lo += _dot((xs + pe_ref[0, l:l + 1, :]).astype(BF16), w1_ref[0, l])
        acc_hi += _dot((xs + pe_ref[0, CMP_STRIDE + l:CMP_STRIDE + l + 1, :]).astype(BF16),
                       w1_ref[0, CMP_STRIDE + l])
    hid = jax.nn.gelu(acc_lo + pltpu.roll(acc_hi, n - 1, 0))
    o_ref[0, 0, 0] = _dot(hid.astype(BF16), w2_ref[0])


def _cmp_mlp(proj, pe, w1_bf, w2_bf, col_block0):
    b, s, _ = proj.shape
    g, hd = NSA_KV_GROUPS, NSA_HEAD_DIM
    n = s // CMP_STRIDE
    return pl.pallas_call(
        _cmp_mlp_kernel,
        grid=(b, 2, g),
        in_specs=[pl.BlockSpec((1, s, hd), lambda bi, c, gi: (bi, 0, col_block0 + c * g + gi)),
                  pl.BlockSpec((1, CMP_BLOCK, hd), lambda bi, c, gi: (c, 0, 0)),
                  pl.BlockSpec((1, CMP_BLOCK, hd, hd), lambda bi, c, gi: (c, 0, 0, 0)),
                  pl.BlockSpec((1, hd, hd), lambda bi, c, gi: (c, 0, 0))],
        out_specs=pl.BlockSpec((1, 1, 1, n, hd), lambda bi, c, gi: (bi, c, gi, 0, 0)),
        out_shape=jax.ShapeDtypeStruct((b, 2, g, n, hd), F32),
        compiler_params=_params(("parallel", "parallel", "parallel"), 32),
        name="nsa_cmp_mlp",
    )(proj, pe, w1_bf, w2_bf)


def _cmp_attn_kernel(q_ref, kc_ref, vc_ref, gl_ref, gb_ref, ovt_ref, o_ref, sel_ref, *, ts, n_slc):
    g = pl.program_id(1)
    si = pl.program_id(2)
    hd = NSA_HEAD_DIM
    scale = hd ** -0.5
    lane = lax.broadcasted_iota(jnp.int32, (ts, LANES), 1)
    t = si * ts + lax.broadcasted_iota(jnp.int32, (ts, LANES), 0)
    cmask = (lane * CMP_STRIDE + (CMP_BLOCK - 1)) <= t
    cmask_f = cmask.astype(F32)
    kc = kc_ref[0, 0, 0].astype(BF16)
    vc = vc_ref[0, 0, 0].astype(BF16)
    gates = jax.nn.sigmoid(gl_ref[0] + gb_ref[...])
    psum = jnp.zeros((ts, LANES), F32)
    for r in range(NSA_Q_PER_GROUP):
        q = q_ref[0, :, r * hd:(r + 1) * hd].astype(BF16)
        s = jnp.where(cmask, _dot_nt(q, kc) * scale, NEG_INF)
        e = jnp.exp(s - jnp.max(s, axis=-1, keepdims=True))
        p = e / jnp.sum(e, axis=-1, keepdims=True) * cmask_f
        psum = psum + p
        o = _dot(p.astype(BF16), vc)
        o_ref[0, :, r * hd:(r + 1) * hd] = o * _lane_col(gates, g * NSA_Q_PER_GROUP + r)

    ovt = ovt_ref[...]
    hi, mid, lo = _split3(psum)
    imp_t = (_dot_nt(ovt, hi) + _dot_nt(ovt, mid) + _dot_nt(ovt, lo))[:n_slc, :]
    blk = lax.broadcasted_iota(jnp.int32, (n_slc, ts), 0)
    q_blk = (si * ts + lax.broadcasted_iota(jnp.int32, (n_slc, ts), 1)) // SLC_BLOCK
    forced = (blk == 0) | (blk == q_blk)
    score = jnp.where(forced, jnp.inf, jnp.where(blk <= q_blk, imp_t, -jnp.inf))
    blk_f = blk.astype(F32)
    taken = jnp.zeros((n_slc, ts), jnp.int32)
    for _ in range(min(N_SELECT, n_slc)):
        free = taken == 0
        best = jnp.max(jnp.where(free, score, -jnp.inf), axis=0, keepdims=True)
        first = jnp.min(jnp.where(free & (score == best), blk_f, float(n_slc)), axis=0, keepdims=True)
        taken = jnp.where(blk_f == first, 1, taken)
    sel_t = jnp.concatenate([taken.astype(F32), jnp.zeros((LANES - n_slc, ts), F32)], axis=0)
    sel_ref[0, 0] = sel_t.T


def _cmp_attn(proj, kv_cmp, gate_b_lane, overlap_t_bf, gate_col_block, ts=256):
    b, s, _ = proj.shape
    g, hd, r = NSA_KV_GROUPS, NSA_HEAD_DIM, NSA_Q_PER_GROUP
    n = kv_cmp.shape[-2]
    kern = functools.partial(_cmp_attn_kernel, ts=ts, n_slc=s // SLC_BLOCK)
    return pl.pallas_call(
        kern,
        grid=(b, g, s // ts),
        in_specs=[pl.BlockSpec((1, ts, r * hd), lambda bi, gi, si: (bi, si, gi)),
                  pl.BlockSpec((1, 1, 1, n, hd), lambda bi, gi, si: (bi, 0, gi, 0, 0)),
                  pl.BlockSpec((1, 1, 1, n, hd), lambda bi, gi, si: (bi, 1, gi, 0, 0)),
                  pl.BlockSpec((1, ts, LANES), lambda bi, gi, si: (bi, si, gate_col_block)),
                  pl.BlockSpec((1, LANES), lambda bi, gi, si: (0, 0)),
                  pl.BlockSpec((LANES, n), lambda bi, gi, si: (0, 0))],
        out_specs=[pl.BlockSpec((1, ts, r * hd), lambda bi, gi, si: (bi, si, gi)),
                   pl.BlockSpec((1, 1, ts, LANES), lambda bi, gi, si: (bi, gi, si, 0))],
        out_shape=[jax.ShapeDtypeStruct((b, s, g * r * hd), F32),
                   jax.ShapeDtypeStruct((b, g, s, LANES), F32)],
        compiler_params=_params(("parallel", "parallel", "parallel"), 32),
        name="nsa_cmp_attn_topk",
    )(proj, kv_cmp, kv_cmp, proj, gate_b_lane, overlap_t_bf)


def _sel_win_kernel(q_ref, ks_ref, vs_ref, kw_ref, vw_ref, oc_ref, sm_ref, ex_ref, gl_ref, gb_ref,
                    c_ref, sa_ref, sb_ref, o_ref, ks_rot, vs_ext, kw_rot, vw_ext, q_st, *, tq, tk):
    g = pl.program_id(1)
    qi = pl.program_id(2)
    hd, nr = NSA_HEAD_DIM, NSA_Q_PER_GROUP
    s_len = ks_rot.shape[0]
    scale = hd ** -0.5

    @pl.when(qi == 0)
    def _():
        c, sa, sb = c_ref[0], sa_ref[0], sb_ref[0]
        ks_rot[...] = _rope(ks_ref[0], c, sa, sb).astype(BF16)
        kw_rot[...] = _rope(kw_ref[0], c, sa, sb).astype(BF16)
        ones = jnp.ones((s_len, hd), BF16)
        vs_ext[...] = jnp.concatenate([vs_ref[0].astype(BF16), ones], axis=1)
        vw_ext[...] = jnp.concatenate([vw_ref[0].astype(BF16), ones], axis=1)

    q0 = pl.multiple_of(qi * tq, tq)
    ropes = (c_ref[0, pl.ds(q0, tq), :], sa_ref[0, pl.ds(q0, tq), :], sb_ref[0, pl.ds(q0, tq), :])
    for r in range(nr):
        q = _rope(q_ref[0, :, r * hd:(r + 1) * hd], *ropes) * scale
        q_st[r * tq:(r + 1) * tq, :] = q.astype(BF16)
    gates = jax.nn.sigmoid(gl_ref[0] + gb_ref[...])

    def attend(k, v_ext, bias):
        n = k.shape[0]
        s = _dot_nt(q_st[...], k).reshape(nr, tq, n) + bias[None]
        p = jnp.exp(s - jnp.max(s, axis=-1, keepdims=True))
        o_ext = _dot(p.reshape(nr * tq, n).astype(BF16), v_ext)
        return o_ext[:, :hd] / o_ext[:, hd:hd + 1]

    wk = WINDOW + tq
    start = pl.multiple_of(jnp.maximum(q0 - WINDOW, 0), tq)
    dist = (q0 - start) + lax.broadcasted_iota(jnp.int32, (tq, wk), 0) - lax.broadcasted_iota(jnp.int32, (tq, wk), 1)
    sm = sm_ref[0, 0].astype(BF16)
    n_chunks = (q0 + tq + tk - 1) // tk
    for n in range(1, s_len // tk + 1):
        @pl.when(n_chunks == n)
        def _():
            nk = n * tk
            bias_w = jnp.where((dist >= 0) & (dist < WINDOW), 0.0, NEG_INF)
            o_win = attend(kw_rot[pl.ds(start, wk), :], vw_ext[pl.ds(start, wk), :], bias_w)
            member = _dot(sm, ex_ref[:, :nk])
            t = q0 + lax.broadcasted_iota(jnp.int32, (tq, nk), 0)
            kpos = lax.broadcasted_iota(jnp.int32, (tq, nk), 1)
            bias_s = jnp.where((member > 0.5) & (kpos <= t), 0.0, NEG_INF)
            o_sel = attend(ks_rot[:nk, :], vs_ext[:nk, :], bias_s)
            for r in range(nr):
                rows = slice(r * tq, (r + 1) * tq)
                total = (oc_ref[0, :, r * hd:(r + 1) * hd]
                         + o_sel[rows, :] * _lane_col(gates, NSA_HEADS + g * nr + r)
                         + o_win[rows, :] * _lane_col(gates, 2 * NSA_HEADS + g * nr + r))
                o_ref[0, :, r * hd:(r + 1) * hd] = total.astype(o_ref.dtype)


def _sel_win_attn(proj, o_cmp, sel_mask, expand_bf, gate_b_lane, tables, sel_col_block0, win_col_block0,
                  gate_col_block, tq=128, tk=512):
    b, s, _ = proj.shape
    g, hd, r = NSA_KV_GROUPS, NSA_HEAD_DIM, NSA_Q_PER_GROUP
    q_spec = pl.BlockSpec((1, tq, r * hd), lambda bi, gi, qi: (bi, qi, gi))

    def kv_spec(col_block):
        return pl.BlockSpec((1, s, hd), lambda bi, gi, qi: (bi, 0, col_block + gi))

    tab_spec = pl.BlockSpec((1, s, LANES), lambda bi, gi, qi: (bi, 0, 0))
    return pl.pallas_call(
        functools.partial(_sel_win_kernel, tq=tq, tk=tk),
        grid=(b, g, s // tq),
        in_specs=[q_spec, kv_spec(sel_col_block0), kv_spec(sel_col_block0 + g),
                  kv_spec(win_col_block0), kv_spec(win_col_block0 + g), q_spec,
                  pl.BlockSpec((1, 1, tq, LANES), lambda bi, gi, qi: (bi, gi, qi, 0)),
                  pl.BlockSpec((LANES, s), lambda bi, gi, qi: (0, 0)),
                  pl.BlockSpec((1, tq, LANES), lambda bi, gi, qi: (bi, qi, gate_col_block)),
                  pl.BlockSpec((1, LANES), lambda bi, gi, qi: (0, 0)),
                  tab_spec, tab_spec, tab_spec],
        out_specs=q_spec,
        out_shape=jax.ShapeDtypeStruct((b, s, g * r * hd), BF16),
        scratch_shapes=[pltpu.VMEM((s, hd), BF16), pltpu.VMEM((s, 2 * hd), BF16)] * 2
        + [pltpu.VMEM((r * tq, hd), BF16)],
        compiler_params=_params(("parallel", "parallel", "arbitrary"), 56),
        name="nsa_sel_win_attn",
    )(proj, proj, proj, proj, proj, o_cmp, sel_mask, expand_bf, proj, gate_b_lane, *tables)


def _pad_cols(w, mult):
    n = w.shape[-1]
    return jnp.pad(w, ((0, 0),) * (w.ndim - 1) + ((0, (-n) % mult),))


def _lane_row(v):
    v = v.reshape(-1)
    return jnp.pad(v, (0, LANES - v.shape[0])).reshape(1, LANES)


def _nsa_mixer(x2d, b, s, norm_g, tables, layer, w_in_bf, gate_b, cmp_pe, cmp_w1, cmp_w2, w_out_bf):
    g, hd, h = NSA_KV_GROUPS, NSA_HEAD_DIM, NSA_HEADS
    proj = _norm_matmul(x2d, norm_g, layer, w_in_bf)
    proj = proj.reshape(b, s, -1)
    q_blocks = h
    cmp0, sel0, win0 = q_blocks, q_blocks + 2 * g, q_blocks + 4 * g
    gate_blk = q_blocks + 6 * g
    gate_b_lane = _lane_row(gate_b)
    kv_cmp = _cmp_mlp(proj, cmp_pe, cmp_w1.astype(BF16), cmp_w2.astype(BF16), cmp0)

    n_cmp_pad = s // CMP_STRIDE
    n_slc = s // SLC_BLOCK
    jj = np.arange(n_cmp_pad)[:, None]
    ss = np.arange(LANES)[None, :]
    lo = np.maximum(jj * CMP_STRIDE, ss * SLC_BLOCK)
    hi = np.minimum(jj * CMP_STRIDE + CMP_BLOCK, ss * SLC_BLOCK + SLC_BLOCK)
    overlap = np.clip(hi - lo, 0, None).astype(np.float32) / CMP_BLOCK
    overlap[(s - CMP_BLOCK) // CMP_STRIDE + 1:, :] = 0.0
    overlap[:, n_slc:] = 0.0
    expand = (np.arange(s)[None, :] // SLC_BLOCK == np.arange(LANES)[:, None]).astype(np.float32)

    o_cmp, sel_mask = _cmp_attn(proj, kv_cmp, gate_b_lane, jnp.asarray(overlap.T, BF16), gate_blk)
    o_all = _sel_win_attn(proj, o_cmp, sel_mask, jnp.asarray(expand, BF16), gate_b_lane, tables, sel0, win0,
                          gate_blk)
    return _out_proj([o_all.reshape(b * s, -1)], layer, w_out_bf, x2d)


def _mlstm_gate_kernel(gl_ref, gb_ref, tri_ref, li_ref, f_ref):
    pre = gl_ref[0] + gb_ref[...]
    capped = GATE_SOFTCAP * jnp.tanh(pre / GATE_SOFTCAP)
    li_ref[0] = capped
    lf = jax.nn.log_sigmoid(capped)
    rows = tri_ref.shape[0]
    tri = tri_ref[...]
    for sl in range(lf.shape[0] // rows):
        hi, mid, lo = _split3(lf[sl * rows:(sl + 1) * rows, :])
        f_ref[0, pl.ds(sl * rows, rows), :] = _dot(tri, hi) + _dot(tri, mid) + _dot(tri, lo)


def _mlstm_gates(proj, gate_b_lane, gate_col_block, slab=256):
    b, s, _ = proj.shape
    idx = np.arange(slab)
    tri = ((idx[:, None] >= idx[None, :]) &
           (idx[:, None] // MLSTM_CHUNK == idx[None, :] // MLSTM_CHUNK)).astype(np.float32)
    spec = pl.BlockSpec((1, s, LANES), lambda bi: (bi, 0, 0))
    shape = jax.ShapeDtypeStruct((b, s, LANES), F32)
    return pl.pallas_call(
        _mlstm_gate_kernel,
        grid=(b,),
        in_specs=[pl.BlockSpec((1, s, LANES), lambda bi: (bi, 0, gate_col_block)),
                  pl.BlockSpec((1, LANES), lambda bi: (0, 0)),
                  pl.BlockSpec((slab, slab), lambda bi: (0, 0))],
        out_specs=[spec, spec],
        out_shape=[shape, shape],
        compiler_params=_params(("parallel",), 32),
        name="mlstm_gates",
    )(proj, gate_b_lane, jnp.asarray(tri, BF16))


def _mlstm_kernel(q_ref, k_ref, v_ref, og_ref, li_ref, f_ref, hn_ref, o_ref, state_ref):
    hp = pl.program_id(1)
    L, dk, dv = MLSTM_CHUNK, MLSTM_QK_DIM, MLSTM_V_DIM
    s = q_ref.shape[1]
    state_ref[...] = jnp.zeros_like(state_ref)
    lane = lax.broadcasted_iota(jnp.int32, (L, LANES), 1)
    causal = lax.broadcasted_iota(jnp.int32, (L, L), 1) <= lax.broadcasted_iota(jnp.int32, (L, L), 0)
    ones_col = (lane < 3).astype(F32)
    ones_row = ((lane >= 3) & (lane < 6)).astype(F32)

    def chunk(j, rows, m):
        h = hp * MLSTM_HEADS_PER_STEP + j
        qb = (q_ref[0, rows, j * dk:(j + 1) * dk] * dk ** -0.5).astype(BF16)
        k = k_ref[0, rows, j * dk:(j + 1) * dk]
        vb = v_ref[0, rows, j * dv:(j + 1) * dv].astype(BF16)
        li = _lane_col(li_ref[0, rows, :], h)
        fc = _lane_col(f_ref[0, rows, :], MLSTM_HEADS + h)
        f_end = fc[L - 1:L, :]
        a = f_end - fc + li
        a_max = jnp.max(a, axis=0, keepdims=True)
        w = jnp.exp(a - a_max)

        fh, fm, fl = _split3(fc)
        gh, gm, gl = _split3(li - fc)
        lhs = jnp.where(lane == 0, fh.astype(F32), jnp.where(lane == 1, fm.astype(F32),
              jnp.where(lane == 2, fl.astype(F32), ones_row)))
        rhs = jnp.where(lane == 3, gh.astype(F32), jnp.where(lane == 4, gm.astype(F32),
              jnp.where(lane == 5, gl.astype(F32), ones_col)))
        d = jnp.where(causal, _dot_nt(lhs.astype(BF16), rhs.astype(BF16)), -jnp.inf)

        inter = fc + m
        m_t = jnp.maximum(inter, jnp.max(d, axis=-1, keepdims=True))
        g_inter = jnp.exp(inter - m_t)
        sqk = _dot_nt(qb, k.astype(BF16)) * jnp.exp(d - m_t)
        state = state_ref[j]
        from_state = _dot(qb, state.astype(BF16))
        num = _dot(sqk.astype(BF16), vb) + g_inter * from_state[:, :dv]
        den = jnp.sum(sqk, axis=-1, keepdims=True) + g_inter * from_state[:, dv:dv + 1]
        hv = num / jnp.maximum(jnp.abs(den), jnp.exp(-m_t))
        hv = hv * lax.rsqrt(jnp.mean(hv * hv, axis=-1, keepdims=True) + NORM_EPS) * hn_ref[j]
        gated = hv * jax.nn.sigmoid(og_ref[0, rows, j * dv:(j + 1) * dv])
        o_ref[0, rows, j * dv:(j + 1) * dv] = gated.astype(o_ref.dtype)

        v_ext = jnp.concatenate([vb, jnp.ones((L, LANES), BF16)], axis=1)
        d_state = _dot_tn((w * k).astype(BF16), v_ext)
        m_new = jnp.maximum(f_end + m, a_max)
        state_ref[j] = jnp.exp(f_end + m - m_new) * state + jnp.exp(a_max - m_new) * d_state
        return m_new

    def body(c, ms):
        rows = pl.ds(pl.multiple_of(c * L, L), L)
        return tuple(chunk(j, rows, m) for j, m in enumerate(ms))

    lax.fori_loop(0, s // L, body, (jnp.zeros((1, 1), F32),) * MLSTM_HEADS_PER_STEP,
                  unroll=4 // MLSTM_HEADS_PER_STEP)


def _mlstm_core(proj, li, fcum, head_norm):
    b, s, _ = proj.shape
    hh, dk, dv, hps = MLSTM_HEADS, MLSTM_QK_DIM, MLSTM_V_DIM, MLSTM_HEADS_PER_STEP
    k_blk0 = hh // hps
    v_blk0 = 2 * hh * dk // (hps * dv)
    og_blk0 = v_blk0 + hh // hps
    gate_spec = pl.BlockSpec((1, s, LANES), lambda bi, hi: (bi, 0, 0))
    return pl.pallas_call(
        _mlstm_kernel,
        grid=(b, hh // hps),
        in_specs=[pl.BlockSpec((1, s, hps * dk), lambda bi, hi: (bi, 0, hi)),
                  pl.BlockSpec((1, s, hps * dk), lambda bi, hi: (bi, 0, k_blk0 + hi)),
                  pl.BlockSpec((1, s, hps * dv), lambda bi, hi: (bi, 0, v_blk0 + hi)),
                  pl.BlockSpec((1, s, hps * dv), lambda bi, hi: (bi, 0, og_blk0 + hi)),
                  gate_spec, gate_spec,
                  pl.BlockSpec((hps, 1, dv), lambda bi, hi: (hi, 0, 0))],
        out_specs=pl.BlockSpec((1, s, hps * dv), lambda bi, hi: (bi, 0, hi)),
        out_shape=jax.ShapeDtypeStruct((b, s, hh * dv), BF16),
        scratch_shapes=[pltpu.VMEM((hps, dk, dv + LANES), F32)],
        compiler_params=_params(("parallel", "parallel"), 48),
        name="mlstm_chunkwise",
    )(proj, proj, proj, proj, li, fcum, head_norm.reshape(hh, 1, dv))


def _mlstm_mixer(x2d, b, s, norm_g, layer, w_in_bf, gate_b, head_norm, w_out_bf):
    hh, dk, dv = MLSTM_HEADS, MLSTM_QK_DIM, MLSTM_V_DIM
    proj = _norm_matmul(x2d, norm_g, layer, w_in_bf)
    proj = proj.reshape(b, s, -1)
    gate_blk = (2 * hh * dk + 2 * hh * dv) // LANES
    li, fcum = _mlstm_gates(proj, _lane_row(gate_b), gate_blk)
    hcat = _mlstm_core(proj, li, fcum, head_norm)
    return _out_proj([hcat.reshape(b * s, -1)], layer, w_out_bf, x2d)


def kernel(x, positions, nsa_w_in, nsa_gate_b, nsa_cmp_pe, nsa_cmp_w1, nsa_cmp_w2, nsa_w_out,
           mlstm_w_in, mlstm_gate_b, mlstm_head_norm, mlstm_w_out,
           norm_mix, norm_ffn, ffn_w_up, ffn_conv_w, ffn_conv_b, ffn_w_down, norm_final):
    b, s, d = x.shape
    depth = norm_mix.shape[0]
    tables = _rope_tables(positions)
    ffn_up_bf = ffn_w_up.astype(BF16)
    ffn_down_bf = ffn_w_down.astype(BF16)
    nsa_in_bf = _pad_cols(nsa_w_in.astype(BF16), 512)
    nsa_out_bf = nsa_w_out.astype(BF16)
    mlstm_in_bf = _pad_cols(mlstm_w_in.astype(BF16), 512)
    mlstm_out_bf = mlstm_w_out.astype(BF16)
    x2d = x.reshape(b * s, d)
    for i in range(depth):
        j = i // 2
        if i % 2 == 0:
            x2d = _nsa_mixer(x2d, b, s, norm_mix[i], tables, j, nsa_in_bf, nsa_gate_b[j], nsa_cmp_pe[j],
                             nsa_cmp_w1[j], nsa_cmp_w2[j], nsa_out_bf)
        else:
            x2d = _mlstm_mixer(x2d, b, s, norm_mix[i], j, mlstm_in_bf, mlstm_gate_b[j],
                               mlstm_head_norm[j], mlstm_out_bf)
        x2d = _ffn(x2d, norm_ffn[i], i, ffn_up_bf, ffn_conv_w[i], ffn_conv_b[i], ffn_down_bf, s)
    return _rmsnorm(x2d, norm_final).reshape(b, s, d)
```

```python
import functools

import numpy as np
import jax
import jax.numpy as jnp
from jax import lax
from jax.experimental import pallas as pl
from jax.experimental.pallas import tpu as pltpu

F32 = jnp.float32
BF16 = jnp.bfloat16

LANES = 128
NORM_EPS = 1e-6
NEG_INF = -1e30
MASK_BIG = 2.0 ** 100

NSA_HEADS = 16
NSA_KV_GROUPS = 2
NSA_HEAD_DIM = 128
NSA_Q_PER_GROUP = NSA_HEADS // NSA_KV_GROUPS
CMP_BLOCK = 32
CMP_STRIDE = 16
SLC_BLOCK = 64
N_SELECT = 8
WINDOW = 512
ROPE_THETA = 500000.0
ROPE_DIM = NSA_HEAD_DIM // 4
ROPE_HALF = ROPE_DIM // 2

MLSTM_HEADS = 8
MLSTM_QK_DIM = 128
MLSTM_V_DIM = 256
MLSTM_CHUNK = 256
MLSTM_HEADS_PER_STEP = 1
GATE_SOFTCAP = 15.0

CONV_WIDTH = 3
CONV_HALO = 8
CONV_ROWS = 16

MIB = 1024 * 1024


def _params(semantics, vmem_mib):
    return pltpu.CompilerParams(dimension_semantics=semantics, vmem_limit_bytes=vmem_mib * MIB)


def _dot(a, b):
    return jnp.dot(a, b, preferred_element_type=F32)


def _dot_nt(a, b):
    return lax.dot_general(a, b, (((1,), (1,)), ((), ())), preferred_element_type=F32)


def _dot_tn(a, b):
    return lax.dot_general(a, b, (((0,), (0,)), ((), ())), preferred_element_type=F32)


def _split3(x):
    hi = x.astype(BF16)
    r1 = x - hi.astype(F32)
    mid = r1.astype(BF16)
    lo = (r1 - mid.astype(F32)).astype(BF16)
    return hi, mid, lo


def _rms_rows(x, g):
    ms = jnp.mean(x * x, axis=-1, keepdims=True)
    return x * lax.rsqrt(ms + NORM_EPS) * g


def _lane_col(x, idx):
    lane = lax.broadcasted_iota(jnp.int32, x.shape, 1)
    return jnp.sum(jnp.where(lane == idx, x, 0.0), axis=-1, keepdims=True)


def _norm_matmul_kernel(x_ref, g_ref, w_ref, o_ref, xn_ref):
    @pl.when(pl.program_id(1) == 0)
    def _():
        xn_ref[...] = _rms_rows(x_ref[...], g_ref[...]).astype(BF16)

    o_ref[...] = _dot(xn_ref[...], w_ref[...])


def _norm_matmul(x2d, g, layer, w_bf, tm=1024, tn=512):
    t, d = x2d.shape
    n = w_bf.shape[2]
    return pl.pallas_call(
        _norm_matmul_kernel,
        grid=(t // tm, n // tn),
        in_specs=[pl.BlockSpec((tm, d), lambda i, j: (i, 0)),
                  pl.BlockSpec((1, d), lambda i, j: (0, 0)),
                  pl.BlockSpec((None, d, tn), lambda i, j: (layer, 0, j))],
        out_specs=pl.BlockSpec((tm, tn), lambda i, j: (i, j)),
        out_shape=jax.ShapeDtypeStruct((t, n), F32),
        scratch_shapes=[pltpu.VMEM((tm, d), BF16)],
        compiler_params=_params(("parallel", "arbitrary"), 40),
        name="norm_matmul",
    )(x2d, g.reshape(1, d), w_bf)


def _out_proj_kernel(*refs, n_in):
    a_refs = refs[:n_in]
    w_ref, r_ref, o_ref, a_bf = refs[n_in:]

    @pl.when(pl.program_id(1) == 0)
    def _():
        a = a_refs[0][...]
        for ar in a_refs[1:]:
            a = a + ar[...]
        a_bf[...] = a.astype(BF16)

    o_ref[...] = r_ref[...] + _dot(a_bf[...], w_ref[...])


def _out_proj(a_list, layer, w_bf, resid, tm=1024, tn=512):
    t, k = a_list[0].shape
    n = w_bf.shape[2]
    n_in = len(a_list)
    return pl.pallas_call(
        functools.partial(_out_proj_kernel, n_in=n_in),
        grid=(t // tm, n // tn),
        in_specs=[pl.BlockSpec((tm, k), lambda i, j: (i, 0))] * n_in
        + [pl.BlockSpec((None, k, tn), lambda i, j: (layer, 0, j)),
           pl.BlockSpec((tm, tn), lambda i, j: (i, j))],
        out_specs=pl.BlockSpec((tm, tn), lambda i, j: (i, j)),
        out_shape=jax.ShapeDtypeStruct((t, n), F32),
        scratch_shapes=[pltpu.VMEM((tm, k), BF16)],
        compiler_params=_params(("parallel", "arbitrary"), 56),
        name="out_proj",
    )(*a_list, w_bf, resid)


def _ffn_kernel(x_ref, g_ref, wg_ref, wv_ref, cwg_ref, cwv_ref, cbg_ref, cbv_ref, wd_ref, xr_ref, o_ref,
                xn_ref, h_ref, h_last, ug0, uv0, ug1, uv1, carry_g, carry_v, unperm, *, tm, nf, tiles_per_seq):
    i = pl.program_id(0)
    st = pl.program_id(1)
    u_bufs = ((ug0, uv0), (ug1, uv1))
    n_slab = ug0.shape[0]
    half = CONV_ROWS // 2

    @pl.when(st == 0)
    def _():
        xn_ref[...] = _rms_rows(x_ref[...], g_ref[...]).astype(BF16)

        @pl.when(i == 0)
        def _():
            for ref in (ug0, uv0, ug1, uv1, carry_g, carry_v):
                ref[...] = jnp.zeros_like(ref)

    def conv_glu(bufs, dst):
        for sl in range(n_slab):
            lanes = slice(sl * LANES, (sl + 1) * LANES)

            def weights(cw_ref, cb_ref):
                return [cw_ref[k * half:(k + 1) * half, lanes] for k in range(CONV_WIDTH)] + [cb_ref[:, lanes]]

            wts = (weights(cwg_ref, cbg_ref), weights(cwv_ref, cbv_ref))
            for r0 in range(0, tm, CONV_ROWS):
                base = r0 + CONV_HALO

                def branch(u_ref, w):
                    cur_e = u_ref[sl, pl.ds(base, half, stride=2), :]
                    cur_o = u_ref[sl, pl.ds(base + 1, half, stride=2), :]
                    prev_e = u_ref[sl, pl.ds(base - 1, half, stride=2), :]
                    prev2_e = u_ref[sl, pl.ds(base - 2, half, stride=2), :]
                    even = ((w[3] + w[0] * prev2_e) + w[1] * prev_e) + w[2] * cur_e
                    odd = ((w[3] + w[0] * prev_e) + w[1] * cur_e) + w[2] * cur_o
                    return even, odd

                gates = branch(bufs[0], wts[0])
                vals = branch(bufs[1], wts[1])
                hs = [g * v * (1.0 / (1.0 + jnp.exp(-g))) for g, v in zip(gates, vals)]
                dst[pl.ds(r0, CONV_ROWS), lanes] = jnp.concatenate(hs, axis=0).astype(BF16)

    def up_proj(bufs):
        xn = xn_ref[...]
        seq_start = (i % tiles_per_seq) == 0
        for w_ref, u_ref, carry in ((wg_ref, bufs[0], carry_g), (wv_ref, bufs[1], carry_v)):
            u = _dot(xn, w_ref[...])
            halo = jnp.where(seq_start, 0.0, carry[st])
            for sl in range(n_slab):
                lanes = slice(sl * LANES, (sl + 1) * LANES)
                u_ref[sl, pl.ds(CONV_HALO, tm), :] = u[:, lanes]
                u_ref[sl, pl.ds(0, CONV_HALO), :] = halo[:, lanes]
            carry[st] = u[tm - CONV_HALO:, :]

    for parity in range(2):
        @pl.when((st < nf) & (st % 2 == parity))
        def _():
            conv_glu(u_bufs[1 - parity], h_ref.at[jnp.maximum(st - 1, 0)])
            up_proj(u_bufs[parity])

    def down_proj(order):
        acc = None
        for f in order:
            d = _dot(h_last[...] if f == nf - 1 else h_ref[f], wd_ref[f])
            acc = d if acc is None else acc + d
        for sl in range(acc.shape[1] // LANES):
            lanes = slice(sl * LANES, (sl + 1) * LANES)
            for r0 in range(0, tm, CONV_ROWS):
                unperm[sl, pl.ds(r0, half, stride=2), :] = acc[r0:r0 + half, lanes]
                unperm[sl, pl.ds(r0 + 1, half, stride=2), :] = acc[r0 + half:r0 + CONV_ROWS, lanes]
            o_ref[:, lanes] = xr_ref[:, lanes] + unperm[sl]

    @pl.when(st == nf)
    def _():
        conv_glu(u_bufs[(nf - 1) % 2], h_last)
        mid = (nf - 1) // 2
        down_proj(list(range(mid)) + [nf - 1] + list(range(mid, nf - 1)))

    @pl.when(st > nf)
    def _():
        down_proj(range(nf))


def _ffn(x2d, g, layer, w_up_bf, conv_w, conv_b, w_down_bf, seq, tm=1024, tf=512, tn=256):
    t, d = x2d.shape
    d_ff = w_down_bf.shape[1]
    nf = d_ff // tf
    nn = d // tn
    half = CONV_ROWS // 2
    kern = functools.partial(_ffn_kernel, tm=tm, nf=nf, tiles_per_seq=seq // tm)
    conv_w_rows = jnp.repeat(conv_w, half, axis=0)
    conv_b_rows = jnp.broadcast_to(conv_b.reshape(1, -1), (half, conv_b.shape[-1]))

    def up(st):
        return jnp.minimum(st, nf - 1)

    def prev(st):
        return jnp.clip(st - 1, 0, nf - 1)

    def down(st):
        return jnp.maximum(st - nf, 0)

    u_buf = pltpu.VMEM((tf // LANES, tm + CONV_HALO, LANES), F32)
    return pl.pallas_call(
        kern,
        grid=(t // tm, nf + nn),
        in_specs=[pl.BlockSpec((tm, d), lambda i, st: (i, 0), pipeline_mode=pl.Buffered(1)),
                  pl.BlockSpec((1, d), lambda i, st: (0, 0)),
                  pl.BlockSpec((None, d, tf), lambda i, st: (layer, 0, up(st))),
                  pl.BlockSpec((None, d, tf), lambda i, st: (layer, 0, up(st) + nf)),
                  pl.BlockSpec((CONV_WIDTH * half, tf), lambda i, st: (0, prev(st))),
                  pl.BlockSpec((CONV_WIDTH * half, tf), lambda i, st: (0, prev(st) + nf)),
                  pl.BlockSpec((half, tf), lambda i, st: (0, prev(st))),
                  pl.BlockSpec((half, tf), lambda i, st: (0, prev(st) + nf)),
                  pl.BlockSpec((None, nf, tf, tn), lambda i, st: (layer, 0, 0, down(st))),
                  pl.BlockSpec((tm, tn), lambda i, st: (i, down(st)))],
        out_specs=pl.BlockSpec((tm, tn), lambda i, st: (i, down(st))),
        out_shape=jax.ShapeDtypeStruct((t, d), F32),
        scratch_shapes=[pltpu.VMEM((tm, d), BF16),
                        pltpu.VMEM((nf - 1, tm, tf), BF16),
                        pltpu.VMEM((tm, tf), BF16)]
        + [u_buf] * 4
        + [pltpu.VMEM((nf, CONV_HALO, tf), F32)] * 2
        + [pltpu.VMEM((tn // LANES, tm, LANES), F32)],
        compiler_params=_params(("arbitrary", "arbitrary"), 56),
        name="conv_glu_ffn",
    )(x2d, g.reshape(1, d), w_up_bf, w_up_bf, conv_w_rows, conv_w_rows,
      conv_b_rows, conv_b_rows, w_down_bf.reshape(w_down_bf.shape[0], nf, tf, d), x2d)


def _rmsnorm_kernel(x_ref, g_ref, o_ref):
    o_ref[...] = _rms_rows(x_ref[...], g_ref[...])


def _rmsnorm(x2d, g, tm=512):
    t, d = x2d.shape
    return pl.pallas_call(
        _rmsnorm_kernel,
        grid=(t // tm,),
        in_specs=[pl.BlockSpec((tm, d), lambda i: (i, 0)), pl.BlockSpec((1, d), lambda i: (0, 0))],
        out_specs=pl.BlockSpec((tm, d), lambda i: (i, 0)),
        out_shape=jax.ShapeDtypeStruct((t, d), F32),
        compiler_params=_params(("parallel",), 32),
        name="final_rmsnorm",
    )(x2d, g.reshape(1, d))


def _rope_table_kernel(pos_ref, inv_ref, c_ref, sa_ref, sb_ref):
    ang = pos_ref[0].astype(F32) * inv_ref[...]
    lane = lax.broadcasted_iota(jnp.int32, ang.shape, 1)
    cos = jnp.cos(ang)
    sin = jnp.sin(ang)
    c_ref[0] = jnp.where(lane < ROPE_DIM, cos, 1.0)
    sa_ref[0] = jnp.where((lane >= ROPE_HALF) & (lane < ROPE_DIM), sin, 0.0)
    sb_ref[0] = jnp.where(lane < ROPE_HALF, -sin, 0.0)


def _rope_tables(positions):
    b, s = positions.shape
    inv = jnp.power(ROPE_THETA, -jnp.arange(0, ROPE_DIM, 2, dtype=F32) / ROPE_DIM)
    inv_lane = jnp.concatenate([inv, inv, jnp.zeros((LANES - ROPE_DIM,), F32)]).reshape(1, LANES)
    spec = pl.BlockSpec((1, s, LANES), lambda i: (i, 0, 0))
    shape = jax.ShapeDtypeStruct((b, s, LANES), F32)
    return pl.pallas_call(
        _rope_table_kernel,
        grid=(b,),
        in_specs=[pl.BlockSpec((1, s, 1), lambda i: (i, 0, 0)),
                  pl.BlockSpec((1, LANES), lambda i: (0, 0))],
        out_specs=[spec, spec, spec],
        out_shape=[shape, shape, shape],
        compiler_params=_params(("parallel",), 32),
        name="rope_tables",
    )(positions.reshape(b, s, 1), inv_lane)


def _rope(x, c, sa, sb):
    return x * c + pltpu.roll(x, ROPE_HALF, 1) * sa + pltpu.roll(x, LANES - ROPE_HALF, 1) * sb


def _cmp_mlp_kernel(x_ref, pe_ref, w1_ref, w2_ref, o_ref):
    n = o_ref.shape[-2]
    acc_lo = jnp.zeros((n, NSA_HEAD_DIM), F32)
    acc_hi = jnp.zeros((n, NSA_HEAD_DIM), F32)
    for l in range(CMP_STRIDE):
        xs = x_ref[0, pl.ds(l, n, stride=CMP_STRIDE), :]
        acc_lo += _dot((xs + pe_ref[0, l:l + 1, :]).astype(BF16), w1_ref[0, l])
        acc_hi += _dot((xs + pe_ref[0, CMP_STRIDE + l:CMP_STRIDE + l + 1, :]).astype(BF16),
                       w1_ref[0, CMP_STRIDE + l])
    hid = jax.nn.gelu(acc_lo + pltpu.roll(acc_hi, n - 1, 0))
    o_ref[0, 0, 0] = _dot(hid.astype(BF16), w2_ref[0])


def _cmp_mlp(proj, pe, w1_bf, w2_bf, col_block0):
    b, s, _ = proj.shape
    g, hd = NSA_KV_GROUPS, NSA_HEAD_DIM
    n = s // CMP_STRIDE
    return pl.pallas_call(
        _cmp_mlp_kernel,
        grid=(b, 2, g),
        in_specs=[pl.BlockSpec((1, s, hd), lambda bi, c, gi: (bi, 0, col_block0 + c * g + gi)),
                  pl.BlockSpec((1, CMP_BLOCK, hd), lambda bi, c, gi: (c, 0, 0)),
                  pl.BlockSpec((1, CMP_BLOCK, hd, hd), lambda bi, c, gi: (c, 0, 0, 0)),
                  pl.BlockSpec((1, hd, hd), lambda bi, c, gi: (c, 0, 0))],
        out_specs=pl.BlockSpec((1, 1, 1, n, hd), lambda bi, c, gi: (bi, c, gi, 0, 0)),
        out_shape=jax.ShapeDtypeStruct((b, 2, g, n, hd), F32),
        compiler_params=_params(("parallel", "parallel", "parallel"), 32),
        name="nsa_cmp_mlp",
    )(proj, pe, w1_bf, w2_bf)


def _cmp_attn_kernel(q_ref, kc_ref, vc_ref, gl_ref, gb_ref, ovt_ref, o_ref, sel_ref, *, ts, n_slc):
    g = pl.program_id(1)
    si = pl.program_id(2)
    hd = NSA_HEAD_DIM
    scale = hd ** -0.5
    lane = lax.broadcasted_iota(jnp.int32, (ts, LANES), 1)
    t = si * ts + lax.broadcasted_iota(jnp.int32, (ts, LANES), 0)
    cmask = (lane * CMP_STRIDE + (CMP_BLOCK - 1)) <= t
    cmask_f = cmask.astype(F32)
    kc = kc_ref[0, 0, 0].astype(BF16)
    vc = vc_ref[0, 0, 0].astype(BF16)
    gates = jax.nn.sigmoid(gl_ref[0] + gb_ref[...])
    psum = jnp.zeros((ts, LANES), F32)
    for r in range(NSA_Q_PER_GROUP):
        q = q_ref[0, :, r * hd:(r + 1) * hd].astype(BF16)
        s = jnp.where(cmask, _dot_nt(q, kc) * scale, NEG_INF)
        e = jnp.exp(s - jnp.max(s, axis=-1, keepdims=True))
        p = e / jnp.sum(e, axis=-1, keepdims=True) * cmask_f
        psum = psum + p
        o = _dot(p.astype(BF16), vc)
        o_ref[0, :, r * hd:(r + 1) * hd] = o * _lane_col(gates, g * NSA_Q_PER_GROUP + r)

    ovt = ovt_ref[...]
    hi, mid, lo = _split3(psum)
    imp_t = (_dot_nt(ovt, hi) + _dot_nt(ovt, mid) + _dot_nt(ovt, lo))[:n_slc, :]
    blk = lax.broadcasted_iota(jnp.int32, (n_slc, ts), 0)
    q_blk = (si * ts + lax.broadcasted_iota(jnp.int32, (n_slc, ts), 1)) // SLC_BLOCK
    forced = (blk == 0) | (blk == q_blk)
    score = jnp.where(forced, jnp.inf, jnp.where(blk <= q_blk, imp_t, -jnp.inf))
    blk_f = blk.astype(F32)
    taken = jnp.zeros((n_slc, ts), jnp.int32)
    for _ in range(min(N_SELECT, n_slc)):
        free = taken == 0
        best = jnp.max(jnp.where(free, score, -jnp.inf), axis=0, keepdims=True)
        first = jnp.min(jnp.where(free & (score == best), blk_f, float(n_slc)), axis=0, keepdims=True)
        taken = jnp.where(blk_f == first, 1, taken)
    sel_t = jnp.concatenate([taken.astype(F32), jnp.zeros((LANES - n_slc, ts), F32)], axis=0)
    sel_ref[0, 0] = sel_t.T


def _cmp_attn(proj, kv_cmp, gate_b_lane, overlap_t_bf, gate_col_block, ts=256):
    b, s, _ = proj.shape
    g, hd, r = NSA_KV_GROUPS, NSA_HEAD_DIM, NSA_Q_PER_GROUP
    n = kv_cmp.shape[-2]
    kern = functools.partial(_cmp_attn_kernel, ts=ts, n_slc=s // SLC_BLOCK)
    return pl.pallas_call(
        kern,
        grid=(b, g, s // ts),
        in_specs=[pl.BlockSpec((1, ts, r * hd), lambda bi, gi, si: (bi, si, gi)),
                  pl.BlockSpec((1, 1, 1, n, hd), lambda bi, gi, si: (bi, 0, gi, 0, 0)),
                  pl.BlockSpec((1, 1, 1, n, hd), lambda bi, gi, si: (bi, 1, gi, 0, 0)),
                  pl.BlockSpec((1, ts, LANES), lambda bi, gi, si: (bi, si, gate_col_block)),
                  pl.BlockSpec((1, LANES), lambda bi, gi, si: (0, 0)),
                  pl.BlockSpec((LANES, n), lambda bi, gi, si: (0, 0))],
        out_specs=[pl.BlockSpec((1, ts, r * hd), lambda bi, gi, si: (bi, si, gi)),
                   pl.BlockSpec((1, 1, ts, LANES), lambda bi, gi, si: (bi, gi, si, 0))],
        out_shape=[jax.ShapeDtypeStruct((b, s, g * r * hd), F32),
                   jax.ShapeDtypeStruct((b, g, s, LANES), F32)],
        compiler_params=_params(("parallel", "parallel", "parallel"), 32),
        name="nsa_cmp_attn_topk",
    )(proj, kv_cmp, kv_cmp, proj, gate_b_lane, overlap_t_bf)


def _sel_win_kernel(q_ref, ks_ref, vs_ref, kw_ref, vw_ref, oc_ref, sm_ref, gl_ref, gb_ref,
                    c_ref, sa_ref, sb_ref, o_ref, ks_aug, vs_ext, kw_rot, vw_ext, q_st, *, tq, tk):
    g = pl.program_id(1)
    qi = pl.program_id(2)
    hd, nr = NSA_HEAD_DIM, NSA_Q_PER_GROUP
    s_len = ks_aug.shape[0]
    n_slc = s_len // SLC_BLOCK
    scale = hd ** -0.5

    @pl.when(qi == 0)
    def _():
        c, sa, sb = c_ref[0], sa_ref[0], sb_ref[0]
        key_blk = lax.broadcasted_iota(jnp.int32, (s_len, hd), 0) // SLC_BLOCK
        own = key_blk == lax.broadcasted_iota(jnp.int32, (s_len, hd), 1)
        ks_aug[:, :hd] = _rope(ks_ref[0], c, sa, sb).astype(BF16)
        ks_aug[:, hd:] = jnp.where(own, -MASK_BIG, 0.0).astype(BF16)
        kw_rot[...] = _rope(kw_ref[0], c, sa, sb).astype(BF16)
        ones = jnp.ones((s_len, hd), BF16)
        vs_ext[...] = jnp.concatenate([vs_ref[0].astype(BF16), ones], axis=1)
        vw_ext[...] = jnp.concatenate([vw_ref[0].astype(BF16), ones], axis=1)

    q0 = pl.multiple_of(qi * tq, tq)
    ropes = (c_ref[0, pl.ds(q0, tq), :], sa_ref[0, pl.ds(q0, tq), :], sb_ref[0, pl.ds(q0, tq), :])
    lane = lax.broadcasted_iota(jnp.int32, (tq, LANES), 1)
    q_blk = (q0 + lax.broadcasted_iota(jnp.int32, (tq, LANES), 0)) // SLC_BLOCK
    readable = (sm_ref[0, 0] > 0.5) & (lane <= q_blk)
    excluded = jnp.where((lane < n_slc) & jnp.logical_not(readable), 1.0, 0.0).astype(BF16)
    for r in range(nr):
        q = _rope(q_ref[0, :, r * hd:(r + 1) * hd], *ropes) * scale
        q_st[r * tq:(r + 1) * tq, :hd] = q.astype(BF16)
        q_st[r * tq:(r + 1) * tq, hd:] = excluded
    gates = jax.nn.sigmoid(gl_ref[0] + gb_ref[...])

    def normalised(o_ext):
        return o_ext[:, :hd] / o_ext[:, hd:hd + 1]

    def attend_window(k, v_ext, bias):
        n = k.shape[0]
        s = _dot_nt(q_st[:, :hd], k).reshape(nr, tq, n) + bias[None]
        p = jnp.exp(s - jnp.max(s, axis=-1, keepdims=True))
        return normalised(_dot(p.reshape(nr * tq, n).astype(BF16), v_ext))

    def attend_selected(nk):
        lo = nk - tk
        t = q0 + lax.broadcasted_iota(jnp.int32, (tq, tk), 0)
        kpos = lo + lax.broadcasted_iota(jnp.int32, (tq, tk), 1)
        causal = jnp.where(kpos <= t, 0.0, NEG_INF)
        s_hi = _dot_nt(q_st[...], ks_aug[lo:nk, :]).reshape(nr, tq, tk) + causal[None]
        m = jnp.max(s_hi, axis=-1, keepdims=True)
        if lo > 0:
            s_lo = _dot_nt(q_st[...], ks_aug[:lo, :]).reshape(nr, tq, lo)
            m = jnp.maximum(m, jnp.max(s_lo, axis=-1, keepdims=True))
            p_lo = jnp.exp(s_lo - m).reshape(nr * tq, lo).astype(BF16)
        p_hi = jnp.exp(s_hi - m).reshape(nr * tq, tk).astype(BF16)
        o_ext = _dot(p_hi, vs_ext[lo:nk, :])
        if lo > 0:
            o_ext = o_ext + _dot(p_lo, vs_ext[:lo, :])
        return normalised(o_ext)

    wk = WINDOW + tq
    start = pl.multiple_of(jnp.maximum(q0 - WINDOW, 0), tq)
    dist = (q0 - start) + lax.broadcasted_iota(jnp.int32, (tq, wk), 0) - lax.broadcasted_iota(jnp.int32, (tq, wk), 1)
    n_chunks = (q0 + tq + tk - 1) // tk
    for n in range(1, s_len // tk + 1):
        @pl.when(n_chunks == n)
        def _():
            bias_w = jnp.where((dist >= 0) & (dist < WINDOW), 0.0, NEG_INF)
            o_win = attend_window(kw_rot[pl.ds(start, wk), :], vw_ext[pl.ds(start, wk), :], bias_w)
            o_sel = attend_selected(n * tk)
            for r in range(nr):
                rows = slice(r * tq, (r + 1) * tq)
                total = (oc_ref[0, :, r * hd:(r + 1) * hd]
                         + o_sel[rows, :] * _lane_col(gates, NSA_HEADS + g * nr + r)
                         + o_win[rows, :] * _lane_col(gates, 2 * NSA_HEADS + g * nr + r))
                o_ref[0, :, r * hd:(r + 1) * hd] = total.astype(o_ref.dtype)


def _sel_win_attn(proj, o_cmp, sel_mask, gate_b_lane, tables, sel_col_block0, win_col_block0,
                  gate_col_block, tq=128, tk=512):
    b, s, _ = proj.shape
    g, hd, r = NSA_KV_GROUPS, NSA_HEAD_DIM, NSA_Q_PER_GROUP
    q_spec = pl.BlockSpec((1, tq, r * hd), lambda bi, gi, qi: (bi, qi, gi))

    def kv_spec(col_block):
        return pl.BlockSpec((1, s, hd), lambda bi, gi, qi: (bi, 0, col_block + gi))

    tab_spec = pl.BlockSpec((1, s, LANES), lambda bi, gi, qi: (bi, 0, 0))
    return pl.pallas_call(
        functools.partial(_sel_win_kernel, tq=tq, tk=tk),
        grid=(b, g, s // tq),
        in_specs=[q_spec, kv_spec(sel_col_block0), kv_spec(sel_col_block0 + g),
                  kv_spec(win_col_block0), kv_spec(win_col_block0 + g), q_spec,
                  pl.BlockSpec((1, 1, tq, LANES), lambda bi, gi, qi: (bi, gi, qi, 0)),
                  pl.BlockSpec((1, tq, LANES), lambda bi, gi, qi: (bi, qi, gate_col_block)),
                  pl.BlockSpec((1, LANES), lambda bi, gi, qi: (0, 0)),
                  tab_spec, tab_spec, tab_spec],
        out_specs=q_spec,
        out_shape=jax.ShapeDtypeStruct((b, s, g * r * hd), BF16),
        scratch_shapes=[pltpu.VMEM((s, 2 * hd), BF16),
                        pltpu.VMEM((s, 2 * hd), BF16),
                        pltpu.VMEM((s, hd), BF16),
                        pltpu.VMEM((s, 2 * hd), BF16),
                        pltpu.VMEM((r * tq, 2 * hd), BF16)],
        compiler_params=_params(("parallel", "parallel", "arbitrary"), 56),
        name="nsa_sel_win_attn",
    )(proj, proj, proj, proj, proj, o_cmp, sel_mask, proj, gate_b_lane, *tables)


def _pad_cols(w, mult):
    n = w.shape[-1]
    return jnp.pad(w, ((0, 0),) * (w.ndim - 1) + ((0, (-n) % mult),))


def _lane_row(v):
    v = v.reshape(-1)
    return jnp.pad(v, (0, LANES - v.shape[0])).reshape(1, LANES)


def _nsa_mixer(x2d, b, s, norm_g, tables, layer, w_in_bf, gate_b, cmp_pe, cmp_w1, cmp_w2, w_out_bf):
    g, hd, h = NSA_KV_GROUPS, NSA_HEAD_DIM, NSA_HEADS
    proj = _norm_matmul(x2d, norm_g, layer, w_in_bf)
    proj = proj.reshape(b, s, -1)
    q_blocks = h
    cmp0, sel0, win0 = q_blocks, q_blocks + 2 * g, q_blocks + 4 * g
    gate_blk = q_blocks + 6 * g
    gate_b_lane = _lane_row(gate_b)
    kv_cmp = _cmp_mlp(proj, cmp_pe, cmp_w1.astype(BF16), cmp_w2.astype(BF16), cmp0)

    n_cmp_pad = s // CMP_STRIDE
    n_slc = s // SLC_BLOCK
    jj = np.arange(n_cmp_pad)[:, None]
    ss = np.arange(LANES)[None, :]
    lo = np.maximum(jj * CMP_STRIDE, ss * SLC_BLOCK)
    hi = np.minimum(jj * CMP_STRIDE + CMP_BLOCK, ss * SLC_BLOCK + SLC_BLOCK)
    overlap = np.clip(hi - lo, 0, None).astype(np.float32) / CMP_BLOCK
    overlap[(s - CMP_BLOCK) // CMP_STRIDE + 1:, :] = 0.0
    overlap[:, n_slc:] = 0.0

    o_cmp, sel_mask = _cmp_attn(proj, kv_cmp, gate_b_lane, jnp.asarray(overlap.T, BF16), gate_blk)
    o_all = _sel_win_attn(proj, o_cmp, sel_mask, gate_b_lane, tables, sel0, win0, gate_blk)
    return _out_proj([o_all.reshape(b * s, -1)], layer, w_out_bf, x2d)


def _mlstm_gate_kernel(gl_ref, gb_ref, tri_ref, li_ref, f_ref):
    pre = gl_ref[0] + gb_ref[...]
    capped = GATE_SOFTCAP * jnp.tanh(pre / GATE_SOFTCAP)
    li_ref[0] = capped
    lf = jax.nn.log_sigmoid(capped)
    rows = tri_ref.shape[0]
    tri = tri_ref[...]
    for sl in range(lf.shape[0] // rows):
        hi, mid, lo = _split3(lf[sl * rows:(sl + 1) * rows, :])
        f_ref[0, pl.ds(sl * rows, rows), :] = _dot(tri, hi) + _dot(tri, mid) + _dot(tri, lo)


def _mlstm_gates(proj, gate_b_lane, gate_col_block, slab=256):
    b, s, _ = proj.shape
    idx = np.arange(slab)
    tri = ((idx[:, None] >= idx[None, :]) &
           (idx[:, None] // MLSTM_CHUNK == idx[None, :] // MLSTM_CHUNK)).astype(np.float32)
    spec = pl.BlockSpec((1, s, LANES), lambda bi: (bi, 0, 0))
    shape = jax.ShapeDtypeStruct((b, s, LANES), F32)
    return pl.pallas_call(
        _mlstm_gate_kernel,
        grid=(b,),
        in_specs=[pl.BlockSpec((1, s, LANES), lambda bi: (bi, 0, gate_col_block)),
                  pl.BlockSpec((1, LANES), lambda bi: (0, 0)),
                  pl.BlockSpec((slab, slab), lambda bi: (0, 0))],
        out_specs=[spec, spec],
        out_shape=[shape, shape],
        compiler_params=_params(("parallel",), 32),
        name="mlstm_gates",
    )(proj, gate_b_lane, jnp.asarray(tri, BF16))


def _mlstm_kernel(q_ref, k_ref, v_ref, og_ref, li_ref, f_ref, hn_ref, o_ref, state_ref):
    hp = pl.program_id(1)
    L, dk, dv = MLSTM_CHUNK, MLSTM_QK_DIM, MLSTM_V_DIM
    s = q_ref.shape[1]
    state_ref[...] = jnp.zeros_like(state_ref)
    lane = lax.broadcasted_iota(jnp.int32, (L, LANES), 1)
    causal = lax.broadcasted_iota(jnp.int32, (L, L), 1) <= lax.broadcasted_iota(jnp.int32, (L, L), 0)
    ones_col = (lane < 3).astype(F32)
    ones_row = ((lane >= 3) & (lane < 6)).astype(F32)

    def chunk(j, rows, m):
        h = hp * MLSTM_HEADS_PER_STEP + j
        qb = (q_ref[0, rows, j * dk:(j + 1) * dk] * dk ** -0.5).astype(BF16)
        k = k_ref[0, rows, j * dk:(j + 1) * dk]
        vb = v_ref[0, rows, j * dv:(j + 1) * dv].astype(BF16)
        li = _lane_col(li_ref[0, rows, :], h)
        fc = _lane_col(f_ref[0, rows, :], MLSTM_HEADS + h)
        f_end = fc[L - 1:L, :]
        a = f_end - fc + li
        a_max = jnp.max(a, axis=0, keepdims=True)
        w = jnp.exp(a - a_max)

        fh, fm, fl = _split3(fc)
        gh, gm, gl = _split3(li - fc)
        lhs = jnp.where(lane == 0, fh.astype(F32), jnp.where(lane == 1, fm.astype(F32),
              jnp.where(lane == 2, fl.astype(F32), ones_row)))
        rhs = jnp.where(lane == 3, gh.astype(F32), jnp.where(lane == 4, gm.astype(F32),
              jnp.where(lane == 5, gl.astype(F32), ones_col)))
        d = jnp.where(causal, _dot_nt(lhs.astype(BF16), rhs.astype(BF16)), -jnp.inf)

        inter = fc + m
        m_t = jnp.maximum(inter, jnp.max(d, axis=-1, keepdims=True))
        g_inter = jnp.exp(inter - m_t)
        sqk = _dot_nt(qb, k.astype(BF16)) * jnp.exp(d - m_t)
        state = state_ref[j]
        from_state = _dot(qb, state.astype(BF16))
        num = _dot(sqk.astype(BF16), vb) + g_inter * from_state[:, :dv]
        den = jnp.sum(sqk, axis=-1, keepdims=True) + g_inter * from_state[:, dv:dv + 1]
        hv = num / jnp.maximum(jnp.abs(den), jnp.exp(-m_t))
        hv = hv * lax.rsqrt(jnp.mean(hv * hv, axis=-1, keepdims=True) + NORM_EPS) * hn_ref[j]
        gated = hv * jax.nn.sigmoid(og_ref[0, rows, j * dv:(j + 1) * dv])
        o_ref[0, rows, j * dv:(j + 1) * dv] = gated.astype(o_ref.dtype)

        v_ext = jnp.concatenate([vb, jnp.ones((L, LANES), BF16)], axis=1)
        d_state = _dot_tn((w * k).astype(BF16), v_ext)
        m_new = jnp.maximum(f_end + m, a_max)
        state_ref[j] = jnp.exp(f_end + m - m_new) * state + jnp.exp(a_max - m_new) * d_state
        return m_new

    def body(c, ms):
        rows = pl.ds(pl.multiple_of(c * L, L), L)
        return tuple(chunk(j, rows, m) for j, m in enumerate(ms))

    lax.fori_loop(0, s // L, body, (jnp.zeros((1, 1), F32),) * MLSTM_HEADS_PER_STEP,
                  unroll=4 // MLSTM_HEADS_PER_STEP)


def _mlstm_core(proj, li, fcum, head_norm):
    b, s, _ = proj.shape
    hh, dk, dv, hps = MLSTM_HEADS, MLSTM_QK_DIM, MLSTM_V_DIM, MLSTM_HEADS_PER_STEP
    k_blk0 = hh // hps
    v_blk0 = 2 * hh * dk // (hps * dv)
    og_blk0 = v_blk0 + hh // hps
    gate_spec = pl.BlockSpec((1, s, LANES), lambda bi, hi: (bi, 0, 0))
    return pl.pallas_call(
        _mlstm_kernel,
        grid=(b, hh // hps),
        in_specs=[pl.BlockSpec((1, s, hps * dk), lambda bi, hi: (bi, 0, hi)),
                  pl.BlockSpec((1, s, hps * dk), lambda bi, hi: (bi, 0, k_blk0 + hi)),
                  pl.BlockSpec((1, s, hps * dv), lambda bi, hi: (bi, 0, v_blk0 + hi)),
                  pl.BlockSpec((1, s, hps * dv), lambda bi, hi: (bi, 0, og_blk0 + hi)),
                  gate_spec, gate_spec,
                  pl.BlockSpec((hps, 1, dv), lambda bi, hi: (hi, 0, 0))],
        out_specs=pl.BlockSpec((1, s, hps * dv), lambda bi, hi: (bi, 0, hi)),
        out_shape=jax.ShapeDtypeStruct((b, s, hh * dv), BF16),
        scratch_shapes=[pltpu.VMEM((hps, dk, dv + LANES), F32)],
        compiler_params=_params(("parallel", "parallel"), 48),
        name="mlstm_chunkwise",
    )(proj, proj, proj, proj, li, fcum, head_norm.reshape(hh, 1, dv))


def _mlstm_mixer(x2d, b, s, norm_g, layer, w_in_bf, gate_b, head_norm, w_out_bf):
    hh, dk, dv = MLSTM_HEADS, MLSTM_QK_DIM, MLSTM_V_DIM
    proj = _norm_matmul(x2d, norm_g, layer, w_in_bf)
    proj = proj.reshape(b, s, -1)
    gate_blk = (2 * hh * dk + 2 * hh * dv) // LANES
    li, fcum = _mlstm_gates(proj, _lane_row(gate_b), gate_blk)
    hcat = _mlstm_core(proj, li, fcum, head_norm)
    return _out_proj([hcat.reshape(b * s, -1)], layer, w_out_bf, x2d)


def kernel(x, positions, nsa_w_in, nsa_gate_b, nsa_cmp_pe, nsa_cmp_w1, nsa_cmp_w2, nsa_w_out,
           mlstm_w_in, mlstm_gate_b, mlstm_head_norm, mlstm_w_out,
           norm_mix, norm_ffn, ffn_w_up, ffn_conv_w, ffn_conv_b, ffn_w_down, norm_final):
    b, s, d = x.shape
    depth = norm_mix.shape[0]
    tables = _rope_tables(positions)
    ffn_up_bf = ffn_w_up.astype(BF16)
    ffn_down_bf = ffn_w_down.astype(BF16)
    nsa_in_bf = _pad_cols(nsa_w_in.astype(BF16), 512)
    nsa_out_bf = nsa_w_out.astype(BF16)
    mlstm_in_bf = _pad_cols(mlstm_w_in.astype(BF16), 512)
    mlstm_out_bf = mlstm_w_out.astype(BF16)
    x2d = x.reshape(b * s, d)
    for i in range(depth):
        j = i // 2
        if i % 2 == 0:
            x2d = _nsa_mixer(x2d, b, s, norm_mix[i], tables, j, nsa_in_bf, nsa_gate_b[j], nsa_cmp_pe[j],
                             nsa_cmp_w1[j], nsa_cmp_w2[j], nsa_out_bf)
        else:
            x2d = _mlstm_mixer(x2d, b, s, norm_mix[i], j, mlstm_in_bf, mlstm_gate_b[j],
                               mlstm_head_norm[j], mlstm_out_bf)
        x2d = _ffn(x2d, norm_ffn[i], i, ffn_up_bf, ffn_conv_w[i], ffn_conv_b[i], ffn_down_bf, s)
    return _rmsnorm(x2d, norm_final).reshape(b, s, d)
```

```python
import functools

import numpy as np
import jax
import jax.numpy as jnp
from jax import lax
from jax.experimental import pallas as pl
from jax.experimental.pallas import tpu as pltpu

F32 = jnp.float32
BF16 = jnp.bfloat16

LANES = 128
NORM_EPS = 1e-6
NEG_INF = -1e30
MASK_BIG = 2.0 ** 100

NSA_HEADS = 16
NSA_KV_GROUPS = 2
NSA_HEAD_DIM = 128
NSA_Q_PER_GROUP = NSA_HEADS // NSA_KV_GROUPS
CMP_BLOCK = 32
CMP_STRIDE = 16
SLC_BLOCK = 64
N_SELECT = 8
WINDOW = 512
ROPE_THETA = 500000.0
ROPE_DIM = NSA_HEAD_DIM // 4
ROPE_HALF = ROPE_DIM // 2

MLSTM_HEADS = 8
MLSTM_QK_DIM = 128
MLSTM_V_DIM = 256
MLSTM_CHUNK = 256
MLSTM_HEADS_PER_STEP = 1
GATE_SOFTCAP = 15.0

CONV_WIDTH = 3
CONV_HALO = 8
CONV_ROWS = 16

MIB = 1024 * 1024
VMEM_SMALL_MIB = 32
VMEM_PROJ_MIB = 40
VMEM_MLSTM_MIB = 48
VMEM_LARGE_MIB = 56

ROW_TILE = 1024
COL_TILE = 512
FFN_OUT_TILE = 256
NORM_ROW_TILE = 512
ATTN_Q_TILE = 128
ATTN_KEY_CHUNK = 512
CMP_Q_TILE = 256
CUMSUM_SLAB = 256


def _params(semantics, vmem_mib):
    return pltpu.CompilerParams(dimension_semantics=semantics, vmem_limit_bytes=vmem_mib * MIB)


def _dot(a, b):
    return jnp.dot(a, b, preferred_element_type=F32)


def _dot_nt(a, b):
    return lax.dot_general(a, b, (((1,), (1,)), ((), ())), preferred_element_type=F32)


def _dot_tn(a, b):
    return lax.dot_general(a, b, (((0,), (0,)), ((), ())), preferred_element_type=F32)


def _split3(x):
    hi = x.astype(BF16)
    r1 = x - hi.astype(F32)
    mid = r1.astype(BF16)
    lo = (r1 - mid.astype(F32)).astype(BF16)
    return hi, mid, lo


def _rms_rows(x, g):
    ms = jnp.mean(x * x, axis=-1, keepdims=True)
    return x * lax.rsqrt(ms + NORM_EPS) * g


def _lane_col(x, idx):
    lane = lax.broadcasted_iota(jnp.int32, x.shape, 1)
    return jnp.sum(jnp.where(lane == idx, x, 0.0), axis=-1, keepdims=True)


def _norm_matmul_kernel(x_ref, g_ref, w_ref, o_ref, xn_ref):
    @pl.when(pl.program_id(1) == 0)
    def _():
        xn_ref[...] = _rms_rows(x_ref[...], g_ref[...]).astype(BF16)

    o_ref[...] = _dot(xn_ref[...], w_ref[...])


def _norm_matmul(x2d, g, layer, w_bf, tm=ROW_TILE, tn=COL_TILE):
    t, d = x2d.shape
    n = w_bf.shape[2]
    return pl.pallas_call(
        _norm_matmul_kernel,
        grid=(t // tm, n // tn),
        in_specs=[pl.BlockSpec((tm, d), lambda i, j: (i, 0)),
                  pl.BlockSpec((1, d), lambda i, j: (0, 0)),
                  pl.BlockSpec((None, d, tn), lambda i, j: (layer, 0, j))],
        out_specs=pl.BlockSpec((tm, tn), lambda i, j: (i, j)),
        out_shape=jax.ShapeDtypeStruct((t, n), F32),
        scratch_shapes=[pltpu.VMEM((tm, d), BF16)],
        compiler_params=_params(("parallel", "arbitrary"), VMEM_PROJ_MIB),
        name="norm_matmul",
    )(x2d, g.reshape(1, d), w_bf)


def _out_proj_kernel(*refs, n_in):
    a_refs = refs[:n_in]
    w_ref, r_ref, o_ref, a_bf = refs[n_in:]

    @pl.when(pl.program_id(1) == 0)
    def _():
        a = a_refs[0][...]
        for ar in a_refs[1:]:
            a = a + ar[...]
        a_bf[...] = a.astype(BF16)

    o_ref[...] = r_ref[...] + _dot(a_bf[...], w_ref[...])


def _out_proj(a_list, layer, w_bf, resid, tm=ROW_TILE, tn=COL_TILE):
    t, k = a_list[0].shape
    n = w_bf.shape[2]
    n_in = len(a_list)
    return pl.pallas_call(
        functools.partial(_out_proj_kernel, n_in=n_in),
        grid=(t // tm, n // tn),
        in_specs=[pl.BlockSpec((tm, k), lambda i, j: (i, 0))] * n_in
        + [pl.BlockSpec((None, k, tn), lambda i, j: (layer, 0, j)),
           pl.BlockSpec((tm, tn), lambda i, j: (i, j))],
        out_specs=pl.BlockSpec((tm, tn), lambda i, j: (i, j)),
        out_shape=jax.ShapeDtypeStruct((t, n), F32),
        scratch_shapes=[pltpu.VMEM((tm, k), BF16)],
        compiler_params=_params(("parallel", "arbitrary"), VMEM_LARGE_MIB),
        name="out_proj",
    )(*a_list, w_bf, resid)


def _ffn_kernel(x_ref, g_ref, wg_ref, wv_ref, cwg_ref, cwv_ref, cbg_ref, cbv_ref, wd_ref, xr_ref, o_ref,
                xn_ref, h_ref, h_last, ug0, uv0, ug1, uv1, carry_g, carry_v, unperm, *, tm, nf, tiles_per_seq):
    i = pl.program_id(0)
    st = pl.program_id(1)
    u_bufs = ((ug0, uv0), (ug1, uv1))
    n_slab = ug0.shape[0]
    half = CONV_ROWS // 2

    @pl.when(st == 0)
    def _():
        xn_ref[...] = _rms_rows(x_ref[...], g_ref[...]).astype(BF16)

        @pl.when(i == 0)
        def _():
            for ref in (ug0, uv0, ug1, uv1, carry_g, carry_v):
                ref[...] = jnp.zeros_like(ref)

    def conv_glu(bufs, dst):
        for sl in range(n_slab):
            lanes = slice(sl * LANES, (sl + 1) * LANES)

            def weights(cw_ref, cb_ref):
                return [cw_ref[k * half:(k + 1) * half, lanes] for k in range(CONV_WIDTH)] + [cb_ref[:, lanes]]

            wts = (weights(cwg_ref, cbg_ref), weights(cwv_ref, cbv_ref))
            for r0 in range(0, tm, CONV_ROWS):
                base = r0 + CONV_HALO

                def branch(u_ref, w):
                    cur_e = u_ref[sl, pl.ds(base, half, stride=2), :]
                    cur_o = u_ref[sl, pl.ds(base + 1, half, stride=2), :]
                    prev_e = u_ref[sl, pl.ds(base - 1, half, stride=2), :]
                    prev2_e = u_ref[sl, pl.ds(base - 2, half, stride=2), :]
                    even = ((w[3] + w[0] * prev2_e) + w[1] * prev_e) + w[2] * cur_e
                    odd = ((w[3] + w[0] * prev_e) + w[1] * cur_e) + w[2] * cur_o
                    return even, odd

                gates = branch(bufs[0], wts[0])
                vals = branch(bufs[1], wts[1])
                hs = [g * v * (1.0 / (1.0 + jnp.exp(-g))) for g, v in zip(gates, vals)]
                dst[pl.ds(r0, CONV_ROWS), lanes] = jnp.concatenate(hs, axis=0).astype(BF16)

    def up_proj(bufs):
        xn = xn_ref[...]
        seq_start = (i % tiles_per_seq) == 0
        for w_ref, u_ref, carry in ((wg_ref, bufs[0], carry_g), (wv_ref, bufs[1], carry_v)):
            u = _dot(xn, w_ref[...])
            halo = jnp.where(seq_start, 0.0, carry[st])
            for sl in range(n_slab):
                lanes = slice(sl * LANES, (sl + 1) * LANES)
                u_ref[sl, pl.ds(CONV_HALO, tm), :] = u[:, lanes]
                u_ref[sl, pl.ds(0, CONV_HALO), :] = halo[:, lanes]
            carry[st] = u[tm - CONV_HALO:, :]

    for parity in range(2):
        @pl.when((st < nf) & (st % 2 == parity))
        def _():
            conv_glu(u_bufs[1 - parity], h_ref.at[jnp.maximum(st - 1, 0)])
            up_proj(u_bufs[parity])

    def down_proj(order):
        acc = None
        for f in order:
            d = _dot(h_last[...] if f == nf - 1 else h_ref[f], wd_ref[f])
            acc = d if acc is None else acc + d
        for sl in range(acc.shape[1] // LANES):
            lanes = slice(sl * LANES, (sl + 1) * LANES)
            for r0 in range(0, tm, CONV_ROWS):
                unperm[sl, pl.ds(r0, half, stride=2), :] = acc[r0:r0 + half, lanes]
                unperm[sl, pl.ds(r0 + 1, half, stride=2), :] = acc[r0 + half:r0 + CONV_ROWS, lanes]
            o_ref[:, lanes] = xr_ref[:, lanes] + unperm[sl]

    @pl.when(st == nf)
    def _():
        conv_glu(u_bufs[(nf - 1) % 2], h_last)
        mid = (nf - 1) // 2
        down_proj(list(range(mid)) + [nf - 1] + list(range(mid, nf - 1)))

    @pl.when(st > nf)
    def _():
        down_proj(range(nf))


def _ffn(x2d, g, layer, w_up_bf, conv_w, conv_b, w_down_bf, seq, tm=ROW_TILE, tf=COL_TILE, tn=FFN_OUT_TILE):
    t, d = x2d.shape
    d_ff = w_down_bf.shape[1]
    nf = d_ff // tf
    nn = d // tn
    half = CONV_ROWS // 2
    kern = functools.partial(_ffn_kernel, tm=tm, nf=nf, tiles_per_seq=seq // tm)
    conv_w_rows = jnp.repeat(conv_w, half, axis=0)
    conv_b_rows = jnp.broadcast_to(conv_b.reshape(1, -1), (half, conv_b.shape[-1]))

    def up(st):
        return jnp.minimum(st, nf - 1)

    def prev(st):
        return jnp.clip(st - 1, 0, nf - 1)

    def down(st):
        return jnp.maximum(st - nf, 0)

    u_buf = pltpu.VMEM((tf // LANES, tm + CONV_HALO, LANES), F32)
    return pl.pallas_call(
        kern,
        grid=(t // tm, nf + nn),
        in_specs=[pl.BlockSpec((tm, d), lambda i, st: (i, 0), pipeline_mode=pl.Buffered(1)),
                  pl.BlockSpec((1, d), lambda i, st: (0, 0)),
                  pl.BlockSpec((None, d, tf), lambda i, st: (layer, 0, up(st))),
                  pl.BlockSpec((None, d, tf), lambda i, st: (layer, 0, up(st) + nf)),
                  pl.BlockSpec((CONV_WIDTH * half, tf), lambda i, st: (0, prev(st))),
                  pl.BlockSpec((CONV_WIDTH * half, tf), lambda i, st: (0, prev(st) + nf)),
                  pl.BlockSpec((half, tf), lambda i, st: (0, prev(st))),
                  pl.BlockSpec((half, tf), lambda i, st: (0, prev(st) + nf)),
                  pl.BlockSpec((None, nf, tf, tn), lambda i, st: (layer, 0, 0, down(st))),
                  pl.BlockSpec((tm, tn), lambda i, st: (i, down(st)))],
        out_specs=pl.BlockSpec((tm, tn), lambda i, st: (i, down(st))),
        out_shape=jax.ShapeDtypeStruct((t, d), F32),
        scratch_shapes=[pltpu.VMEM((tm, d), BF16),
                        pltpu.VMEM((nf - 1, tm, tf), BF16),
                        pltpu.VMEM((tm, tf), BF16)]
        + [u_buf] * 4
        + [pltpu.VMEM((nf, CONV_HALO, tf), F32)] * 2
        + [pltpu.VMEM((tn // LANES, tm, LANES), F32)],
        compiler_params=_params(("arbitrary", "arbitrary"), VMEM_LARGE_MIB),
        name="conv_glu_ffn",
    )(x2d, g.reshape(1, d), w_up_bf, w_up_bf, conv_w_rows, conv_w_rows,
      conv_b_rows, conv_b_rows, w_down_bf.reshape(w_down_bf.shape[0], nf, tf, d), x2d)


def _rmsnorm_kernel(x_ref, g_ref, o_ref):
    o_ref[...] = _rms_rows(x_ref[...], g_ref[...])


def _rmsnorm(x2d, g, tm=NORM_ROW_TILE):
    t, d = x2d.shape
    return pl.pallas_call(
        _rmsnorm_kernel,
        grid=(t // tm,),
        in_specs=[pl.BlockSpec((tm, d), lambda i: (i, 0)), pl.BlockSpec((1, d), lambda i: (0, 0))],
        out_specs=pl.BlockSpec((tm, d), lambda i: (i, 0)),
        out_shape=jax.ShapeDtypeStruct((t, d), F32),
        compiler_params=_params(("parallel",), VMEM_SMALL_MIB),
        name="final_rmsnorm",
    )(x2d, g.reshape(1, d))


def _rope_table_kernel(pos_ref, inv_ref, c_ref, sa_ref, sb_ref):
    ang = pos_ref[0].astype(F32) * inv_ref[...]
    lane = lax.broadcasted_iota(jnp.int32, ang.shape, 1)
    cos = jnp.cos(ang)
    sin = jnp.sin(ang)
    c_ref[0] = jnp.where(lane < ROPE_DIM, cos, 1.0)
    sa_ref[0] = jnp.where((lane >= ROPE_HALF) & (lane < ROPE_DIM), sin, 0.0)
    sb_ref[0] = jnp.where(lane < ROPE_HALF, -sin, 0.0)


def _rope_tables(positions):
    b, s = positions.shape
    inv = jnp.power(ROPE_THETA, -jnp.arange(0, ROPE_DIM, 2, dtype=F32) / ROPE_DIM)
    inv_lane = jnp.concatenate([inv, inv, jnp.zeros((LANES - ROPE_DIM,), F32)]).reshape(1, LANES)
    spec = pl.BlockSpec((1, s, LANES), lambda i: (i, 0, 0))
    shape = jax.ShapeDtypeStruct((b, s, LANES), F32)
    return pl.pallas_call(
        _rope_table_kernel,
        grid=(b,),
        in_specs=[pl.BlockSpec((1, s, 1), lambda i: (i, 0, 0)),
                  pl.BlockSpec((1, LANES), lambda i: (0, 0))],
        out_specs=[spec, spec, spec],
        out_shape=[shape, shape, shape],
        compiler_params=_params(("parallel",), VMEM_SMALL_MIB),
        name="rope_tables",
    )(positions.reshape(b, s, 1), inv_lane)


def _rope(x, c, sa, sb):
    return x * c + pltpu.roll(x, ROPE_HALF, 1) * sa + pltpu.roll(x, LANES - ROPE_HALF, 1) * sb


def _cmp_mlp_kernel(x_ref, pe_ref, w1_ref, w2_ref, o_ref):
    n = o_ref.shape[-2]
    acc_lo = jnp.zeros((n, NSA_HEAD_DIM), F32)
    acc_hi = jnp.zeros((n, NSA_HEAD_DIM), F32)
    for l in range(CMP_STRIDE):
        xs = x_ref[0, pl.ds(l, n, stride=CMP_STRIDE), :]
        acc_lo += _dot((xs + pe_ref[0, l:l + 1, :]).astype(BF16), w1_ref[0, l])
        acc_hi += _dot((xs + pe_ref[0, CMP_STRIDE + l:CMP_STRIDE + l + 1, :]).astype(BF16),
                       w1_ref[0, CMP_STRIDE + l])
    hid = jax.nn.gelu(acc_lo + pltpu.roll(acc_hi, n - 1, 0))
    o_ref[0, 0, 0] = _dot(hid.astype(BF16), w2_ref[0])


def _cmp_mlp(proj, pe, w1_bf, w2_bf, col_block0):
    b, s, _ = proj.shape
    g, hd = NSA_KV_GROUPS, NSA_HEAD_DIM
    n = s // CMP_STRIDE
    return pl.pallas_call(
        _cmp_mlp_kernel,
        grid=(b, 2, g),
        in_specs=[pl.BlockSpec((1, s, hd), lambda bi, c, gi: (bi, 0, col_block0 + c * g + gi)),
                  pl.BlockSpec((1, CMP_BLOCK, hd), lambda bi, c, gi: (c, 0, 0)),
                  pl.BlockSpec((1, CMP_BLOCK, hd, hd), lambda bi, c, gi: (c, 0, 0, 0)),
                  pl.BlockSpec((1, hd, hd), lambda bi, c, gi: (c, 0, 0))],
        out_specs=pl.BlockSpec((1, 1, 1, n, hd), lambda bi, c, gi: (bi, c, gi, 0, 0)),
        out_shape=jax.ShapeDtypeStruct((b, 2, g, n, hd), F32),
        compiler_params=_params(("parallel", "parallel", "parallel"), VMEM_SMALL_MIB),
        name="nsa_cmp_mlp",
    )(proj, pe, w1_bf, w2_bf)


def _cmp_attn_kernel(q_ref, kc_ref, vc_ref, gl_ref, gb_ref, ovt_ref, o_ref, sel_ref, *, ts, n_slc):
    g = pl.program_id(1)
    si = pl.program_id(2)
    hd = NSA_HEAD_DIM
    scale = hd ** -0.5
    lane = lax.broadcasted_iota(jnp.int32, (ts, LANES), 1)
    t = si * ts + lax.broadcasted_iota(jnp.int32, (ts, LANES), 0)
    cmask = (lane * CMP_STRIDE + (CMP_BLOCK - 1)) <= t
    cmask_f = cmask.astype(F32)
    kc = kc_ref[0, 0, 0].astype(BF16)
    vc = vc_ref[0, 0, 0].astype(BF16)
    gates = jax.nn.sigmoid(gl_ref[0] + gb_ref[...])
    psum = jnp.zeros((ts, LANES), F32)
    for r in range(NSA_Q_PER_GROUP):
        q = q_ref[0, :, r * hd:(r + 1) * hd].astype(BF16)
        s = jnp.where(cmask, _dot_nt(q, kc) * scale, NEG_INF)
        e = jnp.exp(s - jnp.max(s, axis=-1, keepdims=True))
        p = e / jnp.sum(e, axis=-1, keepdims=True) * cmask_f
        psum = psum + p
        o = _dot(p.astype(BF16), vc)
        o_ref[0, :, r * hd:(r + 1) * hd] = o * _lane_col(gates, g * NSA_Q_PER_GROUP + r)

    ovt = ovt_ref[...]
    hi, mid, lo = _split3(psum)
    imp_t = (_dot_nt(ovt, hi) + _dot_nt(ovt, mid) + _dot_nt(ovt, lo))[:n_slc, :]
    blk = lax.broadcasted_iota(jnp.int32, (n_slc, ts), 0)
    q_blk = (si * ts + lax.broadcasted_iota(jnp.int32, (n_slc, ts), 1)) // SLC_BLOCK
    forced = (blk == 0) | (blk == q_blk)
    score = jnp.where(forced, jnp.inf, jnp.where(blk <= q_blk, imp_t, -jnp.inf))
    blk_f = blk.astype(F32)
    taken = jnp.zeros((n_slc, ts), jnp.int32)
    for _ in range(min(N_SELECT, n_slc)):
        free = taken == 0
        best = jnp.max(jnp.where(free, score, -jnp.inf), axis=0, keepdims=True)
        first = jnp.min(jnp.where(free & (score == best), blk_f, float(n_slc)), axis=0, keepdims=True)
        taken = jnp.where(blk_f == first, 1, taken)
    sel_t = jnp.concatenate([taken.astype(F32), jnp.zeros((LANES - n_slc, ts), F32)], axis=0)
    sel_ref[0, 0] = sel_t.T


def _cmp_attn(proj, kv_cmp, gate_b_lane, overlap_t_bf, gate_col_block, ts=CMP_Q_TILE):
    b, s, _ = proj.shape
    g, hd, r = NSA_KV_GROUPS, NSA_HEAD_DIM, NSA_Q_PER_GROUP
    n = kv_cmp.shape[-2]
    kern = functools.partial(_cmp_attn_kernel, ts=ts, n_slc=s // SLC_BLOCK)
    return pl.pallas_call(
        kern,
        grid=(b, g, s // ts),
        in_specs=[pl.BlockSpec((1, ts, r * hd), lambda bi, gi, si: (bi, si, gi)),
                  pl.BlockSpec((1, 1, 1, n, hd), lambda bi, gi, si: (bi, 0, gi, 0, 0)),
                  pl.BlockSpec((1, 1, 1, n, hd), lambda bi, gi, si: (bi, 1, gi, 0, 0)),
                  pl.BlockSpec((1, ts, LANES), lambda bi, gi, si: (bi, si, gate_col_block)),
                  pl.BlockSpec((1, LANES), lambda bi, gi, si: (0, 0)),
                  pl.BlockSpec((LANES, n), lambda bi, gi, si: (0, 0))],
        out_specs=[pl.BlockSpec((1, ts, r * hd), lambda bi, gi, si: (bi, si, gi)),
                   pl.BlockSpec((1, 1, ts, LANES), lambda bi, gi, si: (bi, gi, si, 0))],
        out_shape=[jax.ShapeDtypeStruct((b, s, g * r * hd), F32),
                   jax.ShapeDtypeStruct((b, g, s, LANES), F32)],
        compiler_params=_params(("parallel", "parallel", "parallel"), VMEM_SMALL_MIB),
        name="nsa_cmp_attn_topk",
    )(proj, kv_cmp, kv_cmp, proj, gate_b_lane, overlap_t_bf)


def _sel_win_kernel(q_ref, ks_ref, vs_ref, kw_ref, vw_ref, oc_ref, sm_ref, gl_ref, gb_ref,
                    c_ref, sa_ref, sb_ref, o_ref, ks_aug, vs_ext, kw_rot, vw_ext, q_st, *, tq, tk):
    g = pl.program_id(1)
    qi = pl.program_id(2)
    hd, nr = NSA_HEAD_DIM, NSA_Q_PER_GROUP
    s_len = ks_aug.shape[0]
    n_slc = s_len // SLC_BLOCK
    scale = hd ** -0.5

    @pl.when(qi == 0)
    def _():
        c, sa, sb = c_ref[0], sa_ref[0], sb_ref[0]
        key_blk = lax.broadcasted_iota(jnp.int32, (s_len, hd), 0) // SLC_BLOCK
        own = key_blk == lax.broadcasted_iota(jnp.int32, (s_len, hd), 1)
        ks_aug[:, :hd] = _rope(ks_ref[0], c, sa, sb).astype(BF16)
        ks_aug[:, hd:] = jnp.where(own, -MASK_BIG, 0.0).astype(BF16)
        kw_rot[...] = _rope(kw_ref[0], c, sa, sb).astype(BF16)
        ones = jnp.ones((s_len, hd), BF16)
        vs_ext[...] = jnp.concatenate([vs_ref[0].astype(BF16), ones], axis=1)
        vw_ext[...] = jnp.concatenate([vw_ref[0].astype(BF16), ones], axis=1)

    q0 = pl.multiple_of(qi * tq, tq)
    ropes = (c_ref[0, pl.ds(q0, tq), :], sa_ref[0, pl.ds(q0, tq), :], sb_ref[0, pl.ds(q0, tq), :])
    lane = lax.broadcasted_iota(jnp.int32, (tq, LANES), 1)
    q_blk = (q0 + lax.broadcasted_iota(jnp.int32, (tq, LANES), 0)) // SLC_BLOCK
    readable = (sm_ref[0, 0] > 0.5) & (lane <= q_blk)
    excluded = jnp.where((lane < n_slc) & jnp.logical_not(readable), 1.0, 0.0).astype(BF16)
    for r in range(nr):
        q = _rope(q_ref[0, :, r * hd:(r + 1) * hd], *ropes) * scale
        q_st[r * tq:(r + 1) * tq, :hd] = q.astype(BF16)
        q_st[r * tq:(r + 1) * tq, hd:] = excluded
    gates = jax.nn.sigmoid(gl_ref[0] + gb_ref[...])

    def normalised(o_ext):
        return o_ext[:, :hd] / o_ext[:, hd:hd + 1]

    def attend_window(k, v_ext, bias):
        n = k.shape[0]
        s = _dot_nt(q_st[:, :hd], k).reshape(nr, tq, n) + bias[None]
        p = jnp.exp(s - jnp.max(s, axis=-1, keepdims=True))
        return normalised(_dot(p.reshape(nr * tq, n).astype(BF16), v_ext))

    def attend_selected(nk):
        lo = nk - tk
        t = q0 + lax.broadcasted_iota(jnp.int32, (tq, tk), 0)
        kpos = lo + lax.broadcasted_iota(jnp.int32, (tq, tk), 1)
        causal = jnp.where(kpos <= t, 0.0, NEG_INF)
        s_hi = _dot_nt(q_st[...], ks_aug[lo:nk, :]).reshape(nr, tq, tk) + causal[None]
        m = jnp.max(s_hi, axis=-1, keepdims=True)
        if lo > 0:
            s_lo = _dot_nt(q_st[...], ks_aug[:lo, :]).reshape(nr, tq, lo)
            m = jnp.maximum(m, jnp.max(s_lo, axis=-1, keepdims=True))
            p_lo = jnp.exp(s_lo - m).reshape(nr * tq, lo).astype(BF16)
        p_hi = jnp.exp(s_hi - m).reshape(nr * tq, tk).astype(BF16)
        o_ext = _dot(p_hi, vs_ext[lo:nk, :])
        if lo > 0:
            o_ext = o_ext + _dot(p_lo, vs_ext[:lo, :])
        return normalised(o_ext)

    wk = WINDOW + tq
    start = pl.multiple_of(jnp.maximum(q0 - WINDOW, 0), tq)
    dist = (q0 - start) + lax.broadcasted_iota(jnp.int32, (tq, wk), 0) - lax.broadcasted_iota(jnp.int32, (tq, wk), 1)
    n_chunks = (q0 + tq + tk - 1) // tk
    for n in range(1, s_len // tk + 1):
        @pl.when(n_chunks == n)
        def _():
            bias_w = jnp.where((dist >= 0) & (dist < WINDOW), 0.0, NEG_INF)
            o_win = attend_window(kw_rot[pl.ds(start, wk), :], vw_ext[pl.ds(start, wk), :], bias_w)
            o_sel = attend_selected(n * tk)
            for r in range(nr):
                rows = slice(r * tq, (r + 1) * tq)
                total = (oc_ref[0, :, r * hd:(r + 1) * hd]
                         + o_sel[rows, :] * _lane_col(gates, NSA_HEADS + g * nr + r)
                         + o_win[rows, :] * _lane_col(gates, 2 * NSA_HEADS + g * nr + r))
                o_ref[0, :, r * hd:(r + 1) * hd] = total.astype(o_ref.dtype)


def _sel_win_attn(proj, o_cmp, sel_mask, gate_b_lane, tables, sel_col_block0, win_col_block0,
                  gate_col_block, tq=ATTN_Q_TILE, tk=ATTN_KEY_CHUNK):
    b, s, _ = proj.shape
    g, hd, r = NSA_KV_GROUPS, NSA_HEAD_DIM, NSA_Q_PER_GROUP
    q_spec = pl.BlockSpec((1, tq, r * hd), lambda bi, gi, qi: (bi, qi, gi))

    def kv_spec(col_block):
        return pl.BlockSpec((1, s, hd), lambda bi, gi, qi: (bi, 0, col_block + gi))

    tab_spec = pl.BlockSpec((1, s, LANES), lambda bi, gi, qi: (bi, 0, 0))
    return pl.pallas_call(
        functools.partial(_sel_win_kernel, tq=tq, tk=tk),
        grid=(b, g, s // tq),
        in_specs=[q_spec, kv_spec(sel_col_block0), kv_spec(sel_col_block0 + g),
                  kv_spec(win_col_block0), kv_spec(win_col_block0 + g), q_spec,
                  pl.BlockSpec((1, 1, tq, LANES), lambda bi, gi, qi: (bi, gi, qi, 0)),
                  pl.BlockSpec((1, tq, LANES), lambda bi, gi, qi: (bi, qi, gate_col_block)),
                  pl.BlockSpec((1, LANES), lambda bi, gi, qi: (0, 0)),
                  tab_spec, tab_spec, tab_spec],
        out_specs=q_spec,
        out_shape=jax.ShapeDtypeStruct((b, s, g * r * hd), BF16),
        scratch_shapes=[pltpu.VMEM((s, 2 * hd), BF16),
                        pltpu.VMEM((s, 2 * hd), BF16),
                        pltpu.VMEM((s, hd), BF16),
                        pltpu.VMEM((s, 2 * hd), BF16),
                        pltpu.VMEM((r * tq, 2 * hd), BF16)],
        compiler_params=_params(("parallel", "parallel", "arbitrary"), VMEM_LARGE_MIB),
        name="nsa_sel_win_attn",
    )(proj, proj, proj, proj, proj, o_cmp, sel_mask, proj, gate_b_lane, *tables)


def _pad_cols(w, mult):
    n = w.shape[-1]
    return jnp.pad(w, ((0, 0),) * (w.ndim - 1) + ((0, (-n) % mult),))


def _lane_row(v):
    v = v.reshape(-1)
    return jnp.pad(v, (0, LANES - v.shape[0])).reshape(1, LANES)


def _nsa_mixer(x2d, b, s, norm_g, tables, layer, w_in_bf, gate_b, cmp_pe, cmp_w1, cmp_w2, w_out_bf):
    g, hd, h = NSA_KV_GROUPS, NSA_HEAD_DIM, NSA_HEADS
    proj = _norm_matmul(x2d, norm_g, layer, w_in_bf)
    proj = proj.reshape(b, s, -1)
    q_blocks = h
    cmp0, sel0, win0 = q_blocks, q_blocks + 2 * g, q_blocks + 4 * g
    gate_blk = q_blocks + 6 * g
    gate_b_lane = _lane_row(gate_b)
    kv_cmp = _cmp_mlp(proj, cmp_pe, cmp_w1.astype(BF16), cmp_w2.astype(BF16), cmp0)

    n_cmp_pad = s // CMP_STRIDE
    n_slc = s // SLC_BLOCK
    jj = np.arange(n_cmp_pad)[:, None]
    ss = np.arange(LANES)[None, :]
    lo = np.maximum(jj * CMP_STRIDE, ss * SLC_BLOCK)
    hi = np.minimum(jj * CMP_STRIDE + CMP_BLOCK, ss * SLC_BLOCK + SLC_BLOCK)
    overlap = np.clip(hi - lo, 0, None).astype(np.float32) / CMP_BLOCK
    overlap[(s - CMP_BLOCK) // CMP_STRIDE + 1:, :] = 0.0
    overlap[:, n_slc:] = 0.0

    o_cmp, sel_mask = _cmp_attn(proj, kv_cmp, gate_b_lane, jnp.asarray(overlap.T, BF16), gate_blk)
    o_all = _sel_win_attn(proj, o_cmp, sel_mask, gate_b_lane, tables, sel0, win0, gate_blk)
    return _out_proj([o_all.reshape(b * s, -1)], layer, w_out_bf, x2d)


def _mlstm_gate_kernel(gl_ref, gb_ref, tri_ref, li_ref, f_ref):
    pre = gl_ref[0] + gb_ref[...]
    capped = GATE_SOFTCAP * jnp.tanh(pre / GATE_SOFTCAP)
    li_ref[0] = capped
    lf = jax.nn.log_sigmoid(capped)
    rows = tri_ref.shape[0]
    tri = tri_ref[...]
    for sl in range(lf.shape[0] // rows):
        hi, mid, lo = _split3(lf[sl * rows:(sl + 1) * rows, :])
        f_ref[0, pl.ds(sl * rows, rows), :] = _dot(tri, hi) + _dot(tri, mid) + _dot(tri, lo)


def _mlstm_gates(proj, gate_b_lane, gate_col_block, slab=CUMSUM_SLAB):
    b, s, _ = proj.shape
    idx = np.arange(slab)
    tri = ((idx[:, None] >= idx[None, :]) &
           (idx[:, None] // MLSTM_CHUNK == idx[None, :] // MLSTM_CHUNK)).astype(np.float32)
    spec = pl.BlockSpec((1, s, LANES), lambda bi: (bi, 0, 0))
    shape = jax.ShapeDtypeStruct((b, s, LANES), F32)
    return pl.pallas_call(
        _mlstm_gate_kernel,
        grid=(b,),
        in_specs=[pl.BlockSpec((1, s, LANES), lambda bi: (bi, 0, gate_col_block)),
                  pl.BlockSpec((1, LANES), lambda bi: (0, 0)),
                  pl.BlockSpec((slab, slab), lambda bi: (0, 0))],
        out_specs=[spec, spec],
        out_shape=[shape, shape],
        compiler_params=_params(("parallel",), VMEM_SMALL_MIB),
        name="mlstm_gates",
    )(proj, gate_b_lane, jnp.asarray(tri, BF16))


def _mlstm_kernel(q_ref, k_ref, v_ref, og_ref, li_ref, f_ref, hn_ref, o_ref, state_ref):
    hp = pl.program_id(1)
    L, dk, dv = MLSTM_CHUNK, MLSTM_QK_DIM, MLSTM_V_DIM
    s = q_ref.shape[1]
    state_ref[...] = jnp.zeros_like(state_ref)
    lane = lax.broadcasted_iota(jnp.int32, (L, LANES), 1)
    causal = lax.broadcasted_iota(jnp.int32, (L, L), 1) <= lax.broadcasted_iota(jnp.int32, (L, L), 0)
    ones_col = (lane < 3).astype(F32)
    ones_row = ((lane >= 3) & (lane < 6)).astype(F32)

    def chunk(j, rows, m):
        h = hp * MLSTM_HEADS_PER_STEP + j
        qb = (q_ref[0, rows, j * dk:(j + 1) * dk] * dk ** -0.5).astype(BF16)
        k = k_ref[0, rows, j * dk:(j + 1) * dk]
        vb = v_ref[0, rows, j * dv:(j + 1) * dv].astype(BF16)
        li = _lane_col(li_ref[0, rows, :], h)
        fc = _lane_col(f_ref[0, rows, :], MLSTM_HEADS + h)
        f_end = fc[L - 1:L, :]
        a = f_end - fc + li
        a_max = jnp.max(a, axis=0, keepdims=True)
        w = jnp.exp(a - a_max)

        fh, fm, fl = _split3(fc)
        gh, gm, gl = _split3(li - fc)
        lhs = jnp.where(lane == 0, fh.astype(F32), jnp.where(lane == 1, fm.astype(F32),
              jnp.where(lane == 2, fl.astype(F32), ones_row)))
        rhs = jnp.where(lane == 3, gh.astype(F32), jnp.where(lane == 4, gm.astype(F32),
              jnp.where(lane == 5, gl.astype(F32), ones_col)))
        d = jnp.where(causal, _dot_nt(lhs.astype(BF16), rhs.astype(BF16)), -jnp.inf)

        inter = fc + m
        m_t = jnp.maximum(inter, jnp.max(d, axis=-1, keepdims=True))
        g_inter = jnp.exp(inter - m_t)
        sqk = _dot_nt(qb, k.astype(BF16)) * jnp.exp(d - m_t)
        state = state_ref[j]
        from_state = _dot(qb, state.astype(BF16))
        num = _dot(sqk.astype(BF16), vb) + g_inter * from_state[:, :dv]
        den = jnp.sum(sqk, axis=-1, keepdims=True) + g_inter * from_state[:, dv:dv + 1]
        hv = num / jnp.maximum(jnp.abs(den), jnp.exp(-m_t))
        hv = hv * lax.rsqrt(jnp.mean(hv * hv, axis=-1, keepdims=True) + NORM_EPS) * hn_ref[j]
        gated = hv * jax.nn.sigmoid(og_ref[0, rows, j * dv:(j + 1) * dv])
        o_ref[0, rows, j * dv:(j + 1) * dv] = gated.astype(o_ref.dtype)

        v_ext = jnp.concatenate([vb, jnp.ones((L, LANES), BF16)], axis=1)
        d_state = _dot_tn((w * k).astype(BF16), v_ext)
        m_new = jnp.maximum(f_end + m, a_max)
        state_ref[j] = jnp.exp(f_end + m - m_new) * state + jnp.exp(a_max - m_new) * d_state
        return m_new

    def body(c, ms):
        rows = pl.ds(pl.multiple_of(c * L, L), L)
        return tuple(chunk(j, rows, m) for j, m in enumerate(ms))

    lax.fori_loop(0, s // L, body, (jnp.zeros((1, 1), F32),) * MLSTM_HEADS_PER_STEP,
                  unroll=4 // MLSTM_HEADS_PER_STEP)


def _mlstm_core(proj, li, fcum, head_norm):
    b, s, _ = proj.shape
    hh, dk, dv, hps = MLSTM_HEADS, MLSTM_QK_DIM, MLSTM_V_DIM, MLSTM_HEADS_PER_STEP
    k_blk0 = hh // hps
    v_blk0 = 2 * hh * dk // (hps * dv)
    og_blk0 = v_blk0 + hh // hps
    gate_spec = pl.BlockSpec((1, s, LANES), lambda bi, hi: (bi, 0, 0))
    return pl.pallas_call(
        _mlstm_kernel,
        grid=(b, hh // hps),
        in_specs=[pl.BlockSpec((1, s, hps * dk), lambda bi, hi: (bi, 0, hi)),
                  pl.BlockSpec((1, s, hps * dk), lambda bi, hi: (bi, 0, k_blk0 + hi)),
                  pl.BlockSpec((1, s, hps * dv), lambda bi, hi: (bi, 0, v_blk0 + hi)),
                  pl.BlockSpec((1, s, hps * dv), lambda bi, hi: (bi, 0, og_blk0 + hi)),
                  gate_spec, gate_spec,
                  pl.BlockSpec((hps, 1, dv), lambda bi, hi: (hi, 0, 0))],
        out_specs=pl.BlockSpec((1, s, hps * dv), lambda bi, hi: (bi, 0, hi)),
        out_shape=jax.ShapeDtypeStruct((b, s, hh * dv), BF16),
        scratch_shapes=[pltpu.VMEM((hps, dk, dv + LANES), F32)],
        compiler_params=_params(("parallel", "parallel"), VMEM_MLSTM_MIB),
        name="mlstm_chunkwise",
    )(proj, proj, proj, proj, li, fcum, head_norm.reshape(hh, 1, dv))


def _mlstm_mixer(x2d, b, s, norm_g, layer, w_in_bf, gate_b, head_norm, w_out_bf):
    hh, dk, dv = MLSTM_HEADS, MLSTM_QK_DIM, MLSTM_V_DIM
    proj = _norm_matmul(x2d, norm_g, layer, w_in_bf)
    proj = proj.reshape(b, s, -1)
    gate_blk = (2 * hh * dk + 2 * hh * dv) // LANES
    li, fcum = _mlstm_gates(proj, _lane_row(gate_b), gate_blk)
    hcat = _mlstm_core(proj, li, fcum, head_norm)
    return _out_proj([hcat.reshape(b * s, -1)], layer, w_out_bf, x2d)


def kernel(x, positions, nsa_w_in, nsa_gate_b, nsa_cmp_pe, nsa_cmp_w1, nsa_cmp_w2, nsa_w_out,
           mlstm_w_in, mlstm_gate_b, mlstm_head_norm, mlstm_w_out,
           norm_mix, norm_ffn, ffn_w_up, ffn_conv_w, ffn_conv_b, ffn_w_down, norm_final):
    b, s, d = x.shape
    depth = norm_mix.shape[0]
    tables = _rope_tables(positions)
    ffn_up_bf = ffn_w_up.astype(BF16)
    ffn_down_bf = ffn_w_down.astype(BF16)
    nsa_in_bf = _pad_cols(nsa_w_in.astype(BF16), 512)
    nsa_out_bf = nsa_w_out.astype(BF16)
    mlstm_in_bf = _pad_cols(mlstm_w_in.astype(BF16), 512)
    mlstm_out_bf = mlstm_w_out.astype(BF16)
    x2d = x.reshape(b * s, d)
    for i in range(depth):
        j = i // 2
        if i % 2 == 0:
            x2d = _nsa_mixer(x2d, b, s, norm_mix[i], tables, j, nsa_in_bf, nsa_gate_b[j], nsa_cmp_pe[j],
                             nsa_cmp_w1[j], nsa_cmp_w2[j], nsa_out_bf)
        else:
            x2d = _mlstm_mixer(x2d, b, s, norm_mix[i], j, mlstm_in_bf, mlstm_gate_b[j],
                               mlstm_head_norm[j], mlstm_out_bf)
        x2d = _ffn(x2d, norm_ffn[i], i, ffn_up_bf, ffn_conv_w[i], ffn_conv_b[i], ffn_down_bf, s)
    return _rmsnorm(x2d, norm_final).reshape(b, s, d)
```
